```python
import jax
import jax.numpy as jnp
from jax import lax
import numpy as np

D_MODEL = 2048
BATCH = 2
SEQ = 4096
DEPTH = 2
DEC_BATCH = 32
DEC_SEQ = 4
PAST_LEN = 16384
PAGE_SIZE = 128

N_MIXERS = 2
N_SWA_LAYERS = (DEPTH + 1) // 2
N_RET_LAYERS = DEPTH // 2
SWA_HEADS = 32
SWA_KV_HEADS = 8
SWA_GROUP = SWA_HEADS // SWA_KV_HEADS
SWA_HEAD_DIM = 64
Q_WIDTH = SWA_HEADS * SWA_HEAD_DIM
KV_WIDTH = SWA_KV_HEADS * SWA_HEAD_DIM
WINDOW = 128
BAND_BLOCK = WINDOW
ATTN_SCALE = SWA_HEAD_DIM ** -0.5
RET_HEADS = 8
RET_DK = D_MODEL // RET_HEADS
RET_DV = 2 * D_MODEL // RET_HEADS
RET_K_WIDTH = RET_HEADS * RET_DK
RET_V_WIDTH = RET_HEADS * RET_DV
RET_IN_WIDTH = 2 * RET_K_WIDTH + 2 * RET_V_WIDTH
RET_CHUNK = 128
N_GROUPS = 4
EXPERTS_PER_GROUP = 8
N_EXPERTS = N_GROUPS * EXPERTS_PER_GROUP
EXPERT_TOP_K = 2
D_EXPERT = D_MODEL // 4
PLE_DIM = 256
NORM_EPS = 1e-6
GN_EPS = 1e-5
NEG_INF = -1e30

kernel_name = "hybrid_swa_retention_hmoe_step"


def _rmsnorm(x, g):
    xf = x.astype(jnp.float32)
    y = xf * lax.rsqrt(jnp.mean(xf * xf, axis=-1, keepdims=True) + NORM_EPS)
    return (y * g.astype(jnp.float32)).astype(x.dtype)


def _alibi_slopes():
    s = jnp.exp2(-8.0 * jnp.arange(1, SWA_HEADS + 1, dtype=jnp.float32) / SWA_HEADS)
    return s.reshape(SWA_KV_HEADS, SWA_GROUP)


def _softmax_with_sink(scores, sinks):
    sk = sinks.astype(jnp.float32).reshape(SWA_KV_HEADS, SWA_GROUP, 1, 1)
    m = jnp.maximum(jnp.max(scores, axis=-1, keepdims=True), sk)
    e = jnp.exp(scores - m)
    return e / (jnp.sum(e, axis=-1, keepdims=True) + jnp.exp(sk - m))


def _swa_project(a, w_qkv, g_q, g_k):
    b, s, _ = a.shape
    q, k, v = jnp.split(a @ w_qkv, [Q_WIDTH, Q_WIDTH + KV_WIDTH], axis=-1)
    q = _rmsnorm(q.reshape(b, s, SWA_KV_HEADS, SWA_GROUP, SWA_HEAD_DIM), g_q)
    k = _rmsnorm(k.reshape(b, s, SWA_KV_HEADS, SWA_HEAD_DIM), g_k)
    v = v.reshape(b, s, SWA_KV_HEADS, SWA_HEAD_DIM)
    return q, k, v


def _swa_band(q, k, v, sinks):
    b, s = q.shape[:2]
    nb = s // BAND_BLOCK
    qb = q.reshape(b, nb, BAND_BLOCK, SWA_KV_HEADS, SWA_GROUP, SWA_HEAD_DIM)

    def with_prev(x):
        xb = x.reshape(b, nb, BAND_BLOCK, SWA_KV_HEADS, SWA_HEAD_DIM)
        prev = jnp.pad(xb, ((0, 0), (1, 0), (0, 0), (0, 0), (0, 0)))[:, :-1]
        return jnp.concatenate([prev, xb], axis=2)

    kk, vv = with_prev(k), with_prev(v)
    scores = jnp.einsum('bnqkgd,bnskd->bnkgqs', qb, kk).astype(jnp.float32) * ATTN_SCALE
    qi = jnp.arange(BAND_BLOCK)[:, None]
    kj = jnp.arange(2 * BAND_BLOCK)[None, :]
    dist = qi - kj + BAND_BLOCK
    blk = jnp.arange(nb)[:, None, None]
    valid = (dist >= 0) & (dist <= WINDOW) & ((blk > 0) | (kj >= BAND_BLOCK))
    alibi = -_alibi_slopes()[:, :, None, None] * dist.astype(jnp.float32)
    scores = jnp.where(valid[:, None, None], scores + alibi, NEG_INF)
    probs = _softmax_with_sink(scores, sinks)
    out = jnp.einsum('bnkgqs,bnskd->bnqkgd', probs.astype(vv.dtype), vv)
    return out.reshape(b, s, Q_WIDTH)


def _swa_decode(q, kk, vv, sinks):
    b, t = q.shape[:2]
    s = kk.shape[1]
    scores = jnp.einsum('btkgd,bskd->bkgts', q, kk).astype(jnp.float32) * ATTN_SCALE
    dist = jnp.arange(t)[:, None] + (s - t) - jnp.arange(s)[None, :]
    valid = (dist >= 0) & (dist <= WINDOW)
    alibi = -_alibi_slopes()[:, :, None, None] * dist.astype(jnp.float32)
    scores = jnp.where(valid, scores + alibi, NEG_INF)
    probs = _softmax_with_sink(scores, sinks)
    out = jnp.einsum('bkgts,bskd->btkgd', probs.astype(vv.dtype), vv)
    return out.reshape(b, t, Q_WIDTH)


def _ret_log_gamma():
    return jnp.log(1.0 - jnp.exp2(-5.0 - jnp.arange(RET_HEADS, dtype=jnp.float32)))


def _retention_chunk(state, q, k, v, log_gamma):
    c = q.shape[2]
    dt = q.dtype
    idx = jnp.arange(c, dtype=jnp.float32)
    lg = log_gamma[:, None]
    diff = idx[:, None] - idx[None, :]
    decay = jnp.where(diff >= 0, jnp.exp(lg[:, :, None] * jnp.maximum(diff, 0.0)), 0.0)
    inner = jnp.einsum('bhqd,bhsd->bhqs', q, k) * decay.astype(dt)
    q_dec = q * jnp.exp(lg * (idx + 1.0))[:, :, None].astype(dt)
    k_dec = k * jnp.exp(lg * (c - 1.0 - idx))[:, :, None].astype(dt)
    out = jnp.einsum('bhqs,bhsv->bhqv', inner, v) + jnp.einsum('bhqd,bhdv->bhqv', q_dec, state)
    new_state = jnp.exp(lg * c)[:, :, None].astype(dt) * state + jnp.einsum('bhsd,bhsv->bhdv', k_dec, v)
    return new_state, out


def _retention(q, k, v, state0):
    b, h, s, _ = q.shape
    c = min(RET_CHUNK, s)
    nc = s // c
    log_gamma = _ret_log_gamma()

    def split(x):
        return jnp.moveaxis(x.reshape(b, h, nc, c, x.shape[-1]), 2, 0)

    def step(state, xs):
        return _retention_chunk(state, xs[0], xs[1], xs[2], log_gamma)

    state, outs = lax.scan(step, state0, (split(q), split(k), split(v)))
    out = jnp.moveaxis(outs, 0, 2).reshape(b, h, s, RET_DV)
    return out, state


def _ret_mixer(a, state0, w_in, g_norm, w_o):
    b, s, _ = a.shape
    q, k, v, gt = jnp.split(a @ w_in, [RET_K_WIDTH, 2 * RET_K_WIDTH, 2 * RET_K_WIDTH + RET_V_WIDTH], axis=-1)

    def heads(x, d):
        return x.reshape(b, s, RET_HEADS, d).transpose(0, 2, 1, 3)

    q = heads(q, RET_DK)
    k = heads(k, RET_DK) * (RET_DK ** -0.5)
    v = heads(v, RET_DV)
    o, state = _retention(q, k, v, state0.astype(q.dtype))
    of = o.transpose(0, 2, 1, 3).astype(jnp.float32)
    mu = jnp.mean(of, axis=-1, keepdims=True)
    var = jnp.mean(jnp.square(of - mu), axis=-1, keepdims=True)
    on = ((of - mu) * lax.rsqrt(var + GN_EPS)).reshape(b, s, RET_V_WIDTH) * g_norm.astype(jnp.float32)
    y = (jax.nn.silu(gt.astype(jnp.float32)) * on).astype(a.dtype) @ w_o
    return y, state


def _hmoe(x, w_group, b_group, w_expert, b_expert, w_gate, w_up, w_down):
    shp = x.shape
    xt = x.reshape(-1, D_MODEL)
    n = xt.shape[0]
    lg = (xt @ w_group).astype(jnp.float32) + b_group.astype(jnp.float32)
    pg = jax.nn.softmax(lg, axis=-1)
    gsel = jnp.argmax(lg, axis=-1)
    pg_sel = jnp.take_along_axis(pg, gsel[:, None], axis=-1)
    le = ((xt @ w_expert).astype(jnp.float32) + b_expert.astype(jnp.float32)).reshape(n, N_GROUPS, EXPERTS_PER_GROUP)
    le = jnp.take_along_axis(le, gsel[:, None, None], axis=1)[:, 0]
    top_p, top_i = lax.top_k(jax.nn.softmax(le, axis=-1), EXPERT_TOP_K)
    wts = pg_sel * top_p / jnp.sum(top_p, axis=-1, keepdims=True)
    eid = gsel[:, None] * EXPERTS_PER_GROUP + top_i
    combine = jnp.sum(jax.nn.one_hot(eid, N_EXPERTS, dtype=jnp.float32) * wts[..., None], axis=1)
    hg = jnp.einsum('nd,edf->nef', xt, w_gate)
    hu = jnp.einsum('nd,edf->nef', xt, w_up)
    hid = jax.nn.silu(hg) * hu * combine[..., None].astype(x.dtype)
    y = jnp.einsum('nef,efd->nd', hid, w_down)
    return y.reshape(shp)


def _ple(h, p_i, g, w_proj, w_gate):
    gate = jax.nn.sigmoid((_rmsnorm(h, g) @ w_gate).astype(jnp.float32))
    return h + ((p_i @ w_proj).astype(jnp.float32) * gate).astype(h.dtype)


def _trunk(h, p, cache_k, cache_v, state_ret, g_mix, g_ffn, g_ple, swa_w_qkv, swa_g_q, swa_g_k, swa_sinks,
           swa_w_o, ret_w_in, ret_g_norm, ret_w_o, moe_w_group, moe_b_group, moe_w_expert, moe_b_expert,
           moe_w_gate, moe_w_up, moe_w_down, ple_w_proj, ple_w_gate):
    k_out, v_out, s_out = [], [], []
    for i in range(DEPTH):
        j = i // N_MIXERS
        a = _rmsnorm(h, g_mix[i])
        if i % N_MIXERS == 0:
            q, k, v = _swa_project(a, swa_w_qkv[j], swa_g_q[j], swa_g_k[j])
            if cache_k is None:
                o = _swa_band(q, k, v, swa_sinks[j])
                kb, vb = k[:, -WINDOW:], v[:, -WINDOW:]
            else:
                kk = jnp.concatenate([cache_k[j].astype(k.dtype), k], axis=1)
                vv = jnp.concatenate([cache_v[j].astype(v.dtype), v], axis=1)
                o = _swa_decode(q, kk, vv, swa_sinks[j])
                kb, vb = kk[:, -WINDOW:], vv[:, -WINDOW:]
            h = h + o @ swa_w_o[j]
            k_out.append(kb)
            v_out.append(vb)
        else:
            if state_ret is None:
                st0 = jnp.zeros((h.shape[0], RET_HEADS, RET_DK, RET_DV), h.dtype)
            else:
                st0 = state_ret[j]
            o, st = _ret_mixer(a, st0, ret_w_in[j], ret_g_norm[j], ret_w_o[j])
            h = h + o
            s_out.append(st)
        h = h + _hmoe(_rmsnorm(h, g_ffn[i]), moe_w_group[i], moe_b_group[i], moe_w_expert[i], moe_b_expert[i],
                      moe_w_gate[i], moe_w_up[i], moe_w_down[i])
        h = _ple(h, p[i], g_ple[i], ple_w_proj[i], ple_w_gate[i])
    return h, jnp.stack(k_out), jnp.stack(v_out), jnp.stack(s_out)


def setup_inputs(seed: int = 0) -> dict:
    key = jax.random.key(seed)
    ks = jax.random.split(key, 32)

    def nrm(k, shape, scale):
        return scale * jax.random.normal(k, shape, jnp.float32)

    win = min(WINDOW, PAST_LEN)
    return {
        'x_prompt': nrm(ks[0], (BATCH, SEQ, D_MODEL), 1.0),
        'x_sample': nrm(ks[1], (DEC_BATCH, DEC_SEQ, D_MODEL), 1.0),
        'cache_k_swa': nrm(ks[2], (N_SWA_LAYERS, DEC_BATCH, win, SWA_KV_HEADS, SWA_HEAD_DIM), 1.0),
        'cache_v_swa': nrm(ks[3], (N_SWA_LAYERS, DEC_BATCH, win, SWA_KV_HEADS, SWA_HEAD_DIM), 1.0),
        'state_ret': nrm(ks[4], (N_RET_LAYERS, DEC_BATCH, RET_HEADS, RET_DK, RET_DV), 0.5),
        'p_prompt': nrm(ks[5], (DEPTH, BATCH, SEQ, PLE_DIM), 1.0),
        'p_sample': nrm(ks[6], (DEPTH, DEC_BATCH, DEC_SEQ, PLE_DIM), 1.0),
        'g_mix': 1.0 + nrm(ks[7], (DEPTH, D_MODEL), 0.02),
        'g_ffn': 1.0 + nrm(ks[8], (DEPTH, D_MODEL), 0.02),
        'g_ple': 1.0 + nrm(ks[9], (DEPTH, D_MODEL), 0.02),
        'swa_w_qkv': nrm(ks[10], (N_SWA_LAYERS, D_MODEL, Q_WIDTH + 2 * KV_WIDTH), D_MODEL ** -0.5),
        'swa_g_q': 1.0 + nrm(ks[11], (N_SWA_LAYERS, SWA_HEAD_DIM), 0.02),
        'swa_g_k': 1.0 + nrm(ks[12], (N_SWA_LAYERS, SWA_HEAD_DIM), 0.02),
        'swa_sinks': nrm(ks[13], (N_SWA_LAYERS, SWA_HEADS), 1.0),
        'swa_w_o': nrm(ks[14], (N_SWA_LAYERS, Q_WIDTH, D_MODEL), Q_WIDTH ** -0.5),
        'ret_w_in': nrm(ks[15], (N_RET_LAYERS, D_MODEL, RET_IN_WIDTH), D_MODEL ** -0.5),
        'ret_g_norm': 1.0 + nrm(ks[16], (N_RET_LAYERS, RET_V_WIDTH), 0.02),
        'ret_w_o': nrm(ks[17], (N_RET_LAYERS, RET_V_WIDTH, D_MODEL), RET_V_WIDTH ** -0.5),
        'moe_w_group': nrm(ks[18], (DEPTH, D_MODEL, N_GROUPS), D_MODEL ** -0.5),
        'moe_b_group': nrm(ks[19], (DEPTH, N_GROUPS), 0.01),
        'moe_w_expert': nrm(ks[20], (DEPTH, D_MODEL, N_EXPERTS), D_MODEL ** -0.5),
        'moe_b_expert': nrm(ks[21], (DEPTH, N_EXPERTS), 0.01),
        'moe_w_gate': nrm(ks[22], (DEPTH, N_EXPERTS, D_MODEL, D_EXPERT), D_MODEL ** -0.5),
        'moe_w_up': nrm(ks[23], (DEPTH, N_EXPERTS, D_MODEL, D_EXPERT), D_MODEL ** -0.5),
        'moe_w_down': nrm(ks[24], (DEPTH, N_EXPERTS, D_EXPERT, D_MODEL), D_EXPERT ** -0.5),
        'ple_w_proj': nrm(ks[25], (DEPTH, PLE_DIM, D_MODEL), PLE_DIM ** -0.5),
        'ple_w_gate': nrm(ks[26], (DEPTH, D_MODEL, D_MODEL), D_MODEL ** -0.5),
    }


def reference(x_prompt, x_sample, cache_k_swa, cache_v_swa, state_ret, p_prompt, p_sample, g_mix, g_ffn, g_ple,
              swa_w_qkv, swa_g_q, swa_g_k, swa_sinks, swa_w_o, ret_w_in, ret_g_norm, ret_w_o, moe_w_group,
              moe_b_group, moe_w_expert, moe_b_expert, moe_w_gate, moe_w_up, moe_w_down, ple_w_proj, ple_w_gate):
    weights = (g_mix, g_ffn, g_ple, swa_w_qkv, swa_g_q, swa_g_k, swa_sinks, swa_w_o, ret_w_in, ret_g_norm, ret_w_o,
               moe_w_group, moe_b_group, moe_w_expert, moe_b_expert, moe_w_gate, moe_w_up, moe_w_down,
               ple_w_proj, ple_w_gate)
    y_prompt, k_p, v_p, s_p = _trunk(x_prompt, p_prompt, None, None, None, *weights)
    y_sample, k_s, v_s, s_s = _trunk(x_sample, p_sample, cache_k_swa, cache_v_swa, state_ret, *weights)
    return (y_prompt, y_sample, k_p, v_p, s_p, k_s, v_s, s_s)
```

```python
import functools
import math

import jax
import jax.numpy as jnp
from jax import lax
from jax.experimental import pallas as pl
from jax.experimental.pallas import tpu as pltpu

D_MODEL = 2048
DEPTH = 2
SWA_HEADS = 32
SWA_KV_HEADS = 8
SWA_HEAD_DIM = 64
Q_WIDTH = SWA_HEADS * SWA_HEAD_DIM
KV_WIDTH = SWA_KV_HEADS * SWA_HEAD_DIM
QKV_WIDTH = Q_WIDTH + 2 * KV_WIDTH
WINDOW = 128
ATTN_SCALE = SWA_HEAD_DIM ** -0.5
RET_HEADS = 8
RET_DK = D_MODEL // RET_HEADS
RET_DV = 2 * D_MODEL // RET_HEADS
RET_K_WIDTH = RET_HEADS * RET_DK
RET_V_WIDTH = RET_HEADS * RET_DV
RET_IN_WIDTH = 2 * RET_K_WIDTH + 2 * RET_V_WIDTH
RET_CHUNK = 128
N_GROUPS = 4
EXPERTS_PER_GROUP = 8
N_EXPERTS = N_GROUPS * EXPERTS_PER_GROUP
D_EXPERT = D_MODEL // 4
PLE_DIM = 256
NORM_EPS = 1e-6
GN_EPS = 1e-5
NEG_INF = -1e30

LANES = 128
ROUTE_LANES = LANES
EXPERT_LANE0 = N_GROUPS
MOE_TILE = 256
PAIR_ROWS = 256
MAX_TILES = LANES
VMEM_LIMIT = 56 * 1024 * 1024

F32 = jnp.float32
BF16 = jnp.bfloat16
HIGHEST = lax.Precision.HIGHEST

ALIBI_SLOPES = tuple(2.0 ** (-8.0 * (h + 1) / SWA_HEADS) for h in range(SWA_HEADS))
RET_LOG_GAMMA = tuple(math.log(1.0 - 2.0 ** (-5.0 - h)) for h in range(RET_HEADS))


def _cparams(sem):
    return pltpu.CompilerParams(dimension_semantics=sem, vmem_limit_bytes=VMEM_LIMIT)


def _mm_dtype(hi):
    return F32 if hi else BF16


def _mm_prec(hi):
    return HIGHEST if hi else None


def _rms(x, g):
    return x * lax.rsqrt(jnp.mean(x * x, axis=-1, keepdims=True) + NORM_EPS) * g


def _proj_kernel(x_ref, g_ref, w_ref, o_ref, a_ref, *, hi):
    @pl.when(pl.program_id(1) == 0)
    def _():
        a_ref[...] = _rms(x_ref[...], g_ref[...]).astype(a_ref.dtype)

    o_ref[...] = jnp.dot(a_ref[...], w_ref[...], preferred_element_type=F32,
                         precision=_mm_prec(hi)).astype(o_ref.dtype)


def _norm_proj(x, g, w, *, tm, tn, hi, name):
    m, k = x.shape
    n = w.shape[1]
    return pl.pallas_call(
        functools.partial(_proj_kernel, hi=hi),
        grid=(m // tm, n // tn),
        in_specs=[pl.BlockSpec((tm, k), lambda i, j: (i, 0)),
                  pl.BlockSpec((1, k), lambda i, j: (0, 0)),
                  pl.BlockSpec((k, tn), lambda i, j: (0, j))],
        out_specs=pl.BlockSpec((tm, tn), lambda i, j: (i, j)),
        out_shape=jax.ShapeDtypeStruct((m, n), F32),
        scratch_shapes=[pltpu.VMEM((tm, k), _mm_dtype(hi))],
        compiler_params=_cparams(("parallel", "arbitrary")),
        name=name,
    )(x, g, w)


def _head_norm(x, g2):
    lane = lax.broadcasted_iota(jnp.int32, (1, LANES), 1)
    lo = lane < SWA_HEAD_DIM
    outs = []
    for j in range(x.shape[1] // LANES):
        xb = x[:, j * LANES:(j + 1) * LANES]
        x2 = xb * xb
        s_lo = jnp.sum(jnp.where(lo, x2, 0.0), axis=-1, keepdims=True)
        s_hi = jnp.sum(jnp.where(lo, 0.0, x2), axis=-1, keepdims=True)
        r = jnp.where(lo, lax.rsqrt(s_lo * (1.0 / SWA_HEAD_DIM) + NORM_EPS),
                      lax.rsqrt(s_hi * (1.0 / SWA_HEAD_DIM) + NORM_EPS))
        outs.append(xb * r * g2)
    return jnp.concatenate(outs, axis=1)


def _attn_heads(qs, kn, v, negdist, valid, sink_ref, hi):
    nq, nk = qs.shape[0], kn.shape[0]
    cdt, prec = _mm_dtype(hi), _mm_prec(hi)
    lane = lax.broadcasted_iota(jnp.int32, (1, LANES), 1)
    lo = lane < SWA_HEAD_DIM
    group = SWA_HEADS // SWA_KV_HEADS
    outs = []
    for m in range(KV_WIDTH // LANES):
        kb = kn[:, m * LANES:(m + 1) * LANES]
        vb = v[:, m * LANES:(m + 1) * LANES]
        kr = pltpu.roll(kb, SWA_HEAD_DIM, 1)
        vr = pltpu.roll(vb, SWA_HEAD_DIM, 1)
        for half in range(2):
            kv = 2 * m + half
            k_lo, k_hi = (kb, kr) if half == 0 else (kr, kb)
            v_lo, v_hi = (vb, vr) if half == 0 else (vr, vb)
            kbd = jnp.concatenate([jnp.where(lo, k_lo, 0.0), jnp.where(lo, 0.0, k_hi)], axis=0).astype(cdt)
            vbd = jnp.concatenate([jnp.where(lo, v_lo, 0.0), jnp.where(lo, 0.0, v_hi)], axis=0).astype(cdt)
            c0 = kv * group * SWA_HEAD_DIM
            q2 = jnp.concatenate([qs[:, c0:c0 + LANES], qs[:, c0 + LANES:c0 + 2 * LANES]], axis=0).astype(cdt)
            s_all = lax.dot_general(q2, kbd, (((1,), (1,)), ((), ())), preferred_element_type=F32, precision=prec)
            e_rows, inv = [], []
            for r in range(2):
                e_cols = []
                for c in range(2):
                    h = kv * group + 2 * r + c
                    s = s_all[r * nq:(r + 1) * nq, c * nk:(c + 1) * nk]
                    s = jnp.where(valid, s + ALIBI_SLOPES[h] * negdist, NEG_INF)
                    sk = sink_ref[h]
                    mx = jnp.maximum(jnp.max(s, axis=-1, keepdims=True), sk)
                    e = jnp.exp(s - mx)
                    den = jnp.sum(e, axis=-1, keepdims=True) + jnp.exp(sk - mx)
                    e_cols.append(e.astype(cdt))
                    inv.append(1.0 / den)
                e_rows.append(jnp.concatenate(e_cols, axis=1))
            p2 = jnp.concatenate(e_rows, axis=0)
            o2 = jnp.dot(p2, vbd, preferred_element_type=F32, precision=prec)
            for r in range(2):
                outs.append(o2[r * nq:(r + 1) * nq] * jnp.where(lo, inv[2 * r], inv[2 * r + 1]))
    return jnp.concatenate(outs, axis=1)


def _swa_prompt_kernel(sink_ref, q_ref, k_ref, v_ref, gq_ref, gk_ref, o_ref, ko_ref, vo_ref, kprev, vprev):
    n = pl.program_id(1)

    @pl.when(n == 0)
    def _():
        kprev[...] = jnp.zeros_like(kprev)
        vprev[...] = jnp.zeros_like(vprev)

    qs = _head_norm(q_ref[...], gq_ref[...]) * ATTN_SCALE
    kn = _head_norm(k_ref[...], gk_ref[...])
    v = v_ref[...]
    kcat = jnp.concatenate([kprev[...], kn], axis=0)
    vcat = jnp.concatenate([vprev[...], v], axis=0)
    qi = lax.broadcasted_iota(jnp.int32, (WINDOW, 2 * WINDOW), 0)
    kj = lax.broadcasted_iota(jnp.int32, (WINDOW, 2 * WINDOW), 1)
    dist = qi - kj + WINDOW
    valid = (dist >= 0) & (dist <= WINDOW) & ((n > 0) | (kj >= WINDOW))
    o = _attn_heads(qs, kcat, vcat, -dist.astype(F32), valid, sink_ref, False)
    o_ref[...] = o.astype(o_ref.dtype)
    kprev[...] = kn
    vprev[...] = v

    @pl.when(n == pl.num_programs(1) - 1)
    def _():
        ko_ref[0] = kn
        vo_ref[0] = v


def _swa_prompt(qkv, g_q, g_k, sinks, batch):
    rows = qkv.shape[0]
    nb = rows // batch // WINDOW
    kblk = Q_WIDTH // KV_WIDTH
    return pl.pallas_call(
        _swa_prompt_kernel,
        grid=(batch, nb),
        in_specs=[pl.BlockSpec(memory_space=pltpu.SMEM),
                  pl.BlockSpec((WINDOW, Q_WIDTH), lambda b, n: (b * nb + n, 0)),
                  pl.BlockSpec((WINDOW, KV_WIDTH), lambda b, n: (b * nb + n, kblk)),
                  pl.BlockSpec((WINDOW, KV_WIDTH), lambda b, n: (b * nb + n, kblk + 1)),
                  pl.BlockSpec((1, LANES), lambda b, n: (0, 0)),
                  pl.BlockSpec((1, LANES), lambda b, n: (0, 0))],
        out_specs=[pl.BlockSpec((WINDOW, Q_WIDTH), lambda b, n: (b * nb + n, 0)),
                   pl.BlockSpec((1, WINDOW, KV_WIDTH), lambda b, n: (b, 0, 0)),
                   pl.BlockSpec((1, WINDOW, KV_WIDTH), lambda b, n: (b, 0, 0))],
        out_shape=[jax.ShapeDtypeStruct((rows, Q_WIDTH), BF16),
                   jax.ShapeDtypeStruct((batch, WINDOW, KV_WIDTH), F32),
                   jax.ShapeDtypeStruct((batch, WINDOW, KV_WIDTH), F32)],
        scratch_shapes=[pltpu.VMEM((WINDOW, KV_WIDTH), F32), pltpu.VMEM((WINDOW, KV_WIDTH), F32)],
        compiler_params=_cparams(("parallel", "arbitrary")),
        name="swa_prompt",
    )(sinks, qkv, qkv, qkv, g_q, g_k)


DEC_ROWS = 8


def _swa_decode_kernel(sink_ref, qkv_ref, ck_ref, cv_ref, gq_ref, gk_ref, o_ref, kn_ref, *, t_new):
    x = qkv_ref[0]
    qs = _head_norm(x[:, :Q_WIDTH], gq_ref[...]) * ATTN_SCALE
    kn = _head_norm(x[:, Q_WIDTH:Q_WIDTH + KV_WIDTH], gk_ref[...])
    v = x[:, Q_WIDTH + KV_WIDTH:]
    fill = jnp.zeros((WINDOW - DEC_ROWS, KV_WIDTH), F32)
    kcat = jnp.concatenate([ck_ref[0], kn, fill], axis=0)
    vcat = jnp.concatenate([cv_ref[0], v, fill], axis=0)
    ti = lax.broadcasted_iota(jnp.int32, (DEC_ROWS, 2 * WINDOW), 0)
    sj = lax.broadcasted_iota(jnp.int32, (DEC_ROWS, 2 * WINDOW), 1)
    dist = ti + WINDOW - sj
    valid = (dist >= 0) & (dist <= WINDOW) & (sj < WINDOW + t_new)
    o_ref[0] = _attn_heads(qs, kcat, vcat, -dist.astype(F32), valid, sink_ref, True)
    kn_ref[0] = kn


def _swa_decode(qkv, cache_k, cache_v, g_q, g_k, sinks, t_new):
    b = qkv.shape[0]
    return pl.pallas_call(
        functools.partial(_swa_decode_kernel, t_new=t_new),
        grid=(b,),
        in_specs=[pl.BlockSpec(memory_space=pltpu.SMEM),
                  pl.BlockSpec((1, DEC_ROWS, QKV_WIDTH), lambda i: (i, 0, 0)),
                  pl.BlockSpec((1, WINDOW, KV_WIDTH), lambda i: (i, 0, 0)),
                  pl.BlockSpec((1, WINDOW, KV_WIDTH), lambda i: (i, 0, 0)),
                  pl.BlockSpec((1, LANES), lambda i: (0, 0)),
                  pl.BlockSpec((1, LANES), lambda i: (0, 0))],
        out_specs=[pl.BlockSpec((1, DEC_ROWS, Q_WIDTH), lambda i: (i, 0, 0)),
                   pl.BlockSpec((1, DEC_ROWS, KV_WIDTH), lambda i: (i, 0, 0))],
        out_shape=[jax.ShapeDtypeStruct((b, DEC_ROWS, Q_WIDTH), F32),
                   jax.ShapeDtypeStruct((b, DEC_ROWS, KV_WIDTH), F32)],
        compiler_params=_cparams(("parallel",)),
        name="swa_decode",
    )(sinks, qkv, cache_k, cache_v, g_q, g_k)


def _route(logits):
    lane = lax.broadcasted_iota(jnp.int32, logits.shape, 1).astype(F32)
    big = float(ROUTE_LANES)
    lg = jnp.where(lane < N_GROUPS, logits, NEG_INF)
    mg = jnp.max(lg, axis=-1, keepdims=True)
    gsel = jnp.min(jnp.where(lg == mg, lane, big), axis=-1, keepdims=True)
    pg_sel = 1.0 / jnp.sum(jnp.where(lane < N_GROUPS, jnp.exp(logits - mg), 0.0), axis=-1, keepdims=True)
    first = EXPERT_LANE0 + EXPERTS_PER_GROUP * gsel
    le = jnp.where((lane >= first) & (lane < first + EXPERTS_PER_GROUP), logits, NEG_INF)
    m1 = jnp.max(le, axis=-1, keepdims=True)
    i1 = jnp.min(jnp.where(le == m1, lane, big), axis=-1, keepdims=True)
    le2 = jnp.where(lane == i1, NEG_INF, le)
    m2 = jnp.max(le2, axis=-1, keepdims=True)
    i2 = jnp.min(jnp.where(le2 == m2, lane, big), axis=-1, keepdims=True)
    t = jnp.exp(m2 - m1)
    w0 = pg_sel / (1.0 + t)
    w1 = pg_sel * t / (1.0 + t)
    e0 = i1 - EXPERT_LANE0
    e1 = i2 - EXPERT_LANE0
    return jnp.where(lane == 0, e0, jnp.where(lane == 1, e1, jnp.where(lane == 2, w0, jnp.where(lane == 3, w1, 0.0))))


def _outproj_kernel(*refs, hi, n_tail):
    o_ref, w_ref, h_ref, g_ref, wr_ref, br_ref = refs[:6]
    h1_ref, route_ref, acc_ref = refs[6 + n_tail:]
    i, k = pl.program_id(0), pl.program_id(1)
    n_main = pl.num_programs(0) - (1 if n_tail else 0)
    cdt, prec = _mm_dtype(hi), _mm_prec(hi)

    @pl.when(i < n_main)
    def _():
        @pl.when(k == 0)
        def _():
            acc_ref[...] = h_ref[...]

        acc_ref[...] += jnp.dot(o_ref[...].astype(cdt), w_ref[...], preferred_element_type=F32, precision=prec)

        @pl.when(k == pl.num_programs(1) - 1)
        def _():
            h1 = acc_ref[...]
            h1_ref[...] = h1
            xn = _rms(h1, g_ref[...]).astype(cdt)
            logits = jnp.dot(xn, wr_ref[...], preferred_element_type=F32, precision=prec) + br_ref[...]
            route_ref[...] = _route(logits)

    if n_tail:
        @pl.when((i == n_main) & (k == pl.num_programs(1) - 1))
        def _():
            for src, dst in zip(refs[6:8], (h1_ref, route_ref)):
                rows = src.shape[0]
                dst[:rows] = src[...]
                dst[rows:] = jnp.zeros((dst.shape[0] - rows, dst.shape[1]), F32)


def _outproj_router(o, w, h, g_ffn, w_r, b_r, *, tm, tk, hi, tail, name):
    m, kdim = o.shape
    d = w.shape[1]
    n_main, n_k = m // tm, kdim // tk
    n_tail = 0 if tail is None else 2
    rows_i = lambda i: jnp.minimum(i, n_main - 1)
    k_i = lambda i, k: jnp.where(i < n_main, k, n_k - 1)
    in_specs = [pl.BlockSpec((tm, tk), lambda i, k: (rows_i(i), k_i(i, k))),
                pl.BlockSpec((tk, d), lambda i, k: (k_i(i, k), 0)),
                pl.BlockSpec((tm, d), lambda i, k: (rows_i(i), 0)),
                pl.BlockSpec((1, d), lambda i, k: (0, 0)),
                pl.BlockSpec((d, ROUTE_LANES), lambda i, k: (0, 0)),
                pl.BlockSpec((1, ROUTE_LANES), lambda i, k: (0, 0))]
    args = [o, w, h, g_ffn, w_r, b_r]
    total_rows = m
    if tail is not None:
        assert tail[0].shape[0] <= tm
        in_specs += [pl.BlockSpec(t.shape, lambda i, k: (0, 0)) for t in tail]
        args += list(tail)
        total_rows += tail[0].shape[0]
    return pl.pallas_call(
        functools.partial(_outproj_kernel, hi=hi, n_tail=n_tail),
        grid=(n_main + (1 if n_tail else 0), n_k),
        in_specs=in_specs,
        out_specs=[pl.BlockSpec((tm, d), lambda i, k: (i, 0)),
                   pl.BlockSpec((tm, ROUTE_LANES), lambda i, k: (i, 0))],
        out_shape=[jax.ShapeDtypeStruct((total_rows, d), F32),
                   jax.ShapeDtypeStruct((total_rows, ROUTE_LANES), F32)],
        scratch_shapes=[pltpu.VMEM((tm, d), F32)],
        compiler_params=_cparams(("parallel", "arbitrary")),
        name=name,
    )(*args)


def _meta_kernel(eid_ref, pos_ref, tile_ref):
    eid = eid_ref[...]
    r_i = lax.broadcasted_iota(jnp.int32, (LANES, LANES), 0)
    c_i = lax.broadcasted_iota(jnp.int32, (LANES, LANES), 1)
    upper = (r_i <= c_i).astype(BF16)
    ones = jnp.ones((LANES, LANES), BF16)
    rr = lax.broadcasted_iota(jnp.int32, (PAIR_ROWS, PAIR_ROWS), 0)
    cc = lax.broadcasted_iota(jnp.int32, (PAIR_ROWS, PAIR_ROWS), 1)
    below = (cc < rr).astype(BF16)
    tile_start = lax.broadcasted_iota(jnp.int32, (1, LANES), 1).astype(F32) * MOE_TILE

    pos = jnp.zeros((PAIR_ROWS, LANES), F32)
    hit = jnp.zeros((PAIR_ROWS, LANES), F32)
    start = jnp.zeros((1, LANES), F32)
    tile_e = jnp.zeros((1, LANES), F32)
    last_e = jnp.zeros((1, LANES), F32)
    for e in range(N_EXPERTS):
        mf = (eid == e).astype(F32)
        mb = mf.astype(BF16)
        incl = jnp.dot(mb, upper, preferred_element_type=F32)
        row_tot = jnp.dot(mb, ones, preferred_element_type=F32)
        row_off = jnp.dot(below, row_tot.astype(BF16), preferred_element_type=F32)
        rank = incl - mf + row_off
        cnt = row_off[PAIR_ROWS - 1:PAIR_ROWS, :] + row_tot[PAIR_ROWS - 1:PAIR_ROWS, :]
        padded = jnp.floor((cnt + (MOE_TILE - 1)) * (1.0 / MOE_TILE)) * MOE_TILE
        pos = pos + mf * (start + rank)
        hit = hit + mf
        start = start + padded
        tile_e = tile_e + (start <= tile_start).astype(F32)
        last_e = jnp.where(cnt > 0, float(e), last_e)
    pos_ref[...] = jnp.where(hit > 0, pos, -1.0).astype(jnp.int32)
    n_tiles = start * (1.0 / MOE_TILE)
    lane = lax.broadcasted_iota(jnp.int32, (1, LANES), 1)
    tile_e = jnp.minimum(tile_e, last_e)
    rec = jnp.where(lane == MAX_TILES - 1, n_tiles, tile_e)
    tile_ref[...] = jnp.broadcast_to(rec, (8, LANES)).astype(jnp.int32)


def _moe_meta(eid_pairs):
    return pl.pallas_call(
        _meta_kernel,
        out_shape=[jax.ShapeDtypeStruct((PAIR_ROWS, LANES), jnp.int32),
                   jax.ShapeDtypeStruct((8, LANES), jnp.int32)],
        compiler_params=pltpu.CompilerParams(vmem_limit_bytes=VMEM_LIMIT),
        name="moe_meta",
    )(eid_pairs)


INV_CHUNK = 2048


def _invert_kernel(pos_ref, inv_ref, *, n_tokens):
    n_pairs = pos_ref.shape[1]
    hi_acc = jnp.zeros((MAX_TILES, MOE_TILE), F32)
    lo_acc = jnp.zeros((MAX_TILES, MOE_TILE), F32)
    t_iota = lax.broadcasted_iota(jnp.int32, (MAX_TILES, INV_CHUNK), 0)
    r_iota = lax.broadcasted_iota(jnp.int32, (MOE_TILE, INV_CHUNK), 0)
    for c in range(n_pairs // INV_CHUNK):
        pos = pos_ref[:, c * INV_CHUNK:(c + 1) * INV_CHUNK]
        p = lax.broadcasted_iota(jnp.int32, (1, INV_CHUNK), 1) + c * INV_CHUNK
        tok = jnp.where(p >= n_tokens, p - n_tokens, p)
        in_tile = (pos >> (MOE_TILE.bit_length() - 1)) == t_iota
        a_hi = jnp.where(in_tile, (tok >> 7).astype(F32), 0.0).astype(BF16)
        a_lo = jnp.where(in_tile, (tok & 127).astype(F32), 0.0).astype(BF16)
        b = ((pos & (MOE_TILE - 1)) == r_iota).astype(BF16)
        dn = (((1,), (1,)), ((), ()))
        hi_acc = hi_acc + lax.dot_general(a_hi, b, dn, preferred_element_type=F32)
        lo_acc = lo_acc + lax.dot_general(a_lo, b, dn, preferred_element_type=F32)
    inv_ref[...] = (hi_acc * 128.0 + lo_acc).astype(jnp.int32)


def _moe_invert(pos_row, n_tokens):
    return pl.pallas_call(
        functools.partial(_invert_kernel, n_tokens=n_tokens),
        out_shape=jax.ShapeDtypeStruct((MAX_TILES, MOE_TILE), jnp.int32),
        compiler_params=pltpu.CompilerParams(vmem_limit_bytes=VMEM_LIMIT),
        name="moe_invert",
    )(pos_row)


def _gather_copy(src_hbm, row, dst, r, sem):
    return pltpu.make_async_copy(src_hbm.at[pl.ds(row, 1)], dst.at[pl.ds(r, 1)], sem)


def _experts_kernel(tile_ref, inv_cur, inv_nxt, h_hbm, g_ref, wg_ref, wu_ref, wd_ref, y_ref,
                    xbuf, sems, wg_bf, wu_bf, wd_bf):
    t = pl.program_id(0)
    n_tiles = tile_ref[MAX_TILES - 1]
    slot = t % 2

    def issue(inv_ref, s):
        def body(r, carry):
            _gather_copy(h_hbm, inv_ref[0, 0, r], xbuf.at[s], r, sems.at[s]).start()
            return carry
        lax.fori_loop(0, MOE_TILE, body, 0, unroll=8)

    @pl.when(t == 0)
    def _():
        issue(inv_cur, 0)

    @pl.when(t + 1 < n_tiles)
    def _():
        issue(inv_nxt, 1 - slot)

    @pl.when(t < n_tiles)
    def _():
        pltpu.make_async_copy(h_hbm.at[pl.ds(0, MOE_TILE)], xbuf.at[slot], sems.at[slot]).wait()

        @pl.when((t == 0) | (tile_ref[t] != tile_ref[jnp.maximum(t - 1, 0)]))
        def _():
            wg_bf[...] = wg_ref[...].astype(BF16)
            wu_bf[...] = wu_ref[...].astype(BF16)
            wd_bf[...] = wd_ref[...].astype(BF16)

        xn = _rms(xbuf[slot], g_ref[...]).astype(BF16)
        hg = jnp.dot(xn, wg_bf[...], preferred_element_type=F32)
        hu = jnp.dot(xn, wu_bf[...], preferred_element_type=F32)
        hid = (hg / (1.0 + jnp.exp(-hg)) * hu).astype(BF16)
        y_ref[...] = jnp.dot(hid, wd_bf[...], preferred_element_type=F32)

    @pl.when(t >= n_tiles)
    def _():
        y_ref[...] = jnp.zeros_like(y_ref)


def _moe_experts(tile_tab, inv, h_all, g_ffn, w_gate, w_up, w_down, layer, n_steps):
    d = h_all.shape[1]

    def widx(t, tab):
        return (layer, tab[t], 0, 0)

    grid_spec = pltpu.PrefetchScalarGridSpec(
        num_scalar_prefetch=1,
        grid=(n_steps,),
        in_specs=[pl.BlockSpec((1, 1, MOE_TILE), lambda t, tab: (t, 0, 0), memory_space=pltpu.SMEM),
                  pl.BlockSpec((1, 1, MOE_TILE), lambda t, tab: (t + 1, 0, 0), memory_space=pltpu.SMEM),
                  pl.BlockSpec(memory_space=pl.ANY),
                  pl.BlockSpec((1, d), lambda t, tab: (0, 0)),
                  pl.BlockSpec((None, None, d, D_EXPERT), widx),
                  pl.BlockSpec((None, None, d, D_EXPERT), widx),
                  pl.BlockSpec((None, None, D_EXPERT, d), widx)],
        out_specs=pl.BlockSpec((MOE_TILE, d), lambda t, tab: (t, 0)),
        scratch_shapes=[pltpu.VMEM((2, MOE_TILE, d), F32),
                        pltpu.SemaphoreType.DMA((2,)),
                        pltpu.VMEM((d, D_EXPERT), BF16),
                        pltpu.VMEM((d, D_EXPERT), BF16),
                        pltpu.VMEM((D_EXPERT, d), BF16)],
    )
    return pl.pallas_call(
        _experts_kernel,
        grid_spec=grid_spec,
        out_shape=jax.ShapeDtypeStruct((n_steps * MOE_TILE, d), F32),
        compiler_params=_cparams(("arbitrary",)),
        name="moe_experts",
    )(tile_tab, inv, inv, h_all, g_ffn, w_gate, w_up, w_down)


def _combine_kernel(p0_cur, p0_nxt, p1_cur, p1_nxt, y_hbm, h_ref, route_ref, p_ref, g_ref, wg_ref, wp_ref,
                    o_ref, ybuf, sems, *, tm):
    i = pl.program_id(0)
    slot = i % 2

    def issue(p0, p1, s):
        def body(r, carry):
            _gather_copy(y_hbm, p0[0, 0, r], ybuf.at[s, 0], r, sems.at[s]).start()
            _gather_copy(y_hbm, p1[0, 0, r], ybuf.at[s, 1], r, sems.at[s]).start()
            return carry
        lax.fori_loop(0, tm, body, 0, unroll=8)

    @pl.when(i == 0)
    def _():
        issue(p0_cur, p1_cur, 0)

    @pl.when(i + 1 < pl.num_programs(0))
    def _():
        issue(p0_nxt, p1_nxt, 1 - slot)

    for c in range(2):
        pltpu.make_async_copy(y_hbm.at[pl.ds(0, tm)], ybuf.at[slot, c], sems.at[slot]).wait()
    route = route_ref[...]
    h2 = h_ref[...] + (route[:, 2:3] * ybuf[slot, 0] + route[:, 3:4] * ybuf[slot, 1])
    xn = _rms(h2, g_ref[...]).astype(BF16)
    z = jnp.dot(xn, wg_ref[...], preferred_element_type=F32)
    gate = 1.0 / (1.0 + jnp.exp(-z))
    proj = jnp.dot(p_ref[...].astype(BF16), wp_ref[...], preferred_element_type=F32)
    o_ref[...] = h2 + proj * gate


def _moe_combine_ple(pos0, pos1, y_sorted, h_all, route_all, p, g_ple, w_gate, w_proj, *, layer, tm, rows,
                     row_block0, name):
    d = h_all.shape[1]
    n = rows // tm
    last = n - 1
    smem = functools.partial(pl.BlockSpec, (1, 1, tm), memory_space=pltpu.SMEM)
    return pl.pallas_call(
        functools.partial(_combine_kernel, tm=tm),
        grid=(n,),
        in_specs=[smem(lambda i: (i, 0, 0)),
                  smem(lambda i: (jnp.minimum(i + 1, last), 0, 0)),
                  smem(lambda i: (i, 0, 0)),
                  smem(lambda i: (jnp.minimum(i + 1, last), 0, 0)),
                  pl.BlockSpec(memory_space=pl.ANY),
                  pl.BlockSpec((tm, d), lambda i: (row_block0 + i, 0)),
                  pl.BlockSpec((tm, ROUTE_LANES), lambda i: (row_block0 + i, 0)),
                  pl.BlockSpec((None, tm, PLE_DIM), lambda i: (layer, i, 0)),
                  pl.BlockSpec((1, d), lambda i: (0, 0)),
                  pl.BlockSpec((d, d), lambda i: (0, 0)),
                  pl.BlockSpec((PLE_DIM, d), lambda i: (0, 0))],
        out_specs=pl.BlockSpec((tm, d), lambda i: (i, 0)),
        out_shape=jax.ShapeDtypeStruct((rows, d), F32),
        scratch_shapes=[pltpu.VMEM((2, 2, tm, d), F32), pltpu.SemaphoreType.DMA((2,))],
        compiler_params=_cparams(("arbitrary",)),
        name=name,
    )(pos0, pos0, pos1, pos1, y_sorted, h_all, route_all, p, g_ple, w_gate, w_proj)


def _ret_head(q, k, v, gt, state, gn, lg, c_real):
    c = q.shape[0]
    k = k * (RET_DK ** -0.5)
    ri = lax.broadcasted_iota(jnp.int32, (c, c), 0)
    ci = lax.broadcasted_iota(jnp.int32, (c, c), 1)
    diff = (ri - ci).astype(F32)
    decay = jnp.where(diff >= 0, jnp.exp(lg * jnp.maximum(diff, 0.0)), 0.0)
    idx = lax.broadcasted_iota(jnp.int32, (c, 1), 0).astype(F32)
    qk = lax.dot_general(q.astype(BF16), k.astype(BF16), (((1,), (1,)), ((), ())), preferred_element_type=F32)
    inner = qk * decay
    q_dec = q * jnp.exp(lg * (idx + 1.0))
    k_dec = k * jnp.exp(lg * (c_real - 1.0 - idx))
    vb = v.astype(BF16)
    out = (jnp.dot(inner.astype(BF16), vb, preferred_element_type=F32)
           + jnp.dot(q_dec.astype(BF16), state.astype(BF16), preferred_element_type=F32))
    new_state = math.exp(lg * c_real) * state + lax.dot_general(
        k_dec.astype(BF16), vb, (((0,), (0,)), ((), ())), preferred_element_type=F32)
    mu = jnp.mean(out, axis=-1, keepdims=True)
    cen = out - mu
    var = jnp.mean(cen * cen, axis=-1, keepdims=True)
    on = cen * lax.rsqrt(var + GN_EPS) * gn
    return gt / (1.0 + jnp.exp(-gt)) * on, new_state


def _ret_prompt_kernel(q_ref, k_ref, v_ref, g_ref, gn_ref, y_ref, so_ref, st_ref):
    c = pl.program_id(1)

    @pl.when(c == 0)
    def _():
        st_ref[...] = jnp.zeros_like(st_ref)

    for h in range(RET_HEADS):
        ks, vs = slice(h * RET_DK, (h + 1) * RET_DK), slice(h * RET_DV, (h + 1) * RET_DV)
        y, new_state = _ret_head(q_ref[:, ks], k_ref[:, ks], v_ref[:, vs], g_ref[:, vs], st_ref[h],
                                 gn_ref[:, vs], RET_LOG_GAMMA[h], RET_CHUNK)
        st_ref[h] = new_state
        y_ref[:, vs] = y.astype(y_ref.dtype)

    @pl.when(c == pl.num_programs(1) - 1)
    def _():
        so_ref[0] = st_ref[...]


def _ret_prompt(qkvg, g_norm, batch):
    rows = qkvg.shape[0]
    nc = rows // batch // RET_CHUNK
    return pl.pallas_call(
        _ret_prompt_kernel,
        grid=(batch, nc),
        in_specs=[pl.BlockSpec((RET_CHUNK, RET_K_WIDTH), lambda b, c: (b * nc + c, 0)),
                  pl.BlockSpec((RET_CHUNK, RET_K_WIDTH), lambda b, c: (b * nc + c, 1)),
                  pl.BlockSpec((RET_CHUNK, RET_V_WIDTH), lambda b, c: (b * nc + c, 1)),
                  pl.BlockSpec((RET_CHUNK, RET_V_WIDTH), lambda b, c: (b * nc + c, 2)),
                  pl.BlockSpec((1, RET_V_WIDTH), lambda b, c: (0, 0))],
        out_specs=[pl.BlockSpec((RET_CHUNK, RET_V_WIDTH), lambda b, c: (b * nc + c, 0)),
                   pl.BlockSpec((1, RET_HEADS, RET_DK, RET_DV), lambda b, c: (b, 0, 0, 0))],
        out_shape=[jax.ShapeDtypeStruct((rows, RET_V_WIDTH), BF16),
                   jax.ShapeDtypeStruct((batch, RET_HEADS, RET_DK, RET_DV), F32)],
        scratch_shapes=[pltpu.VMEM((RET_HEADS, RET_DK, RET_DV), F32)],
        compiler_params=_cparams(("parallel", "arbitrary")),
        name="ret_prompt",
    )(qkvg, qkvg, qkvg, qkvg, g_norm)


def _ret_decode_kernel(x_ref, s_ref, gn_ref, y_ref, so_ref, *, t_new):
    x = x_ref[0]
    for h in range(RET_HEADS):
        q = x[:, h * RET_DK:(h + 1) * RET_DK]
        k = x[:, RET_K_WIDTH + h * RET_DK:RET_K_WIDTH + (h + 1) * RET_DK]
        v0 = 2 * RET_K_WIDTH + h * RET_DV
        g0 = 2 * RET_K_WIDTH + RET_V_WIDTH + h * RET_DV
        y, new_state = _ret_head(q, k, x[:, v0:v0 + RET_DV], x[:, g0:g0 + RET_DV], s_ref[0, h],
                                 gn_ref[:, h * RET_DV:(h + 1) * RET_DV], RET_LOG_GAMMA[h], t_new)
        so_ref[0, h] = new_state
        y_ref[0, :, h * RET_DV:(h + 1) * RET_DV] = y


def _ret_decode(qkvg, state, g_norm, t_new):
    b = qkvg.shape[0]
    return pl.pallas_call(
        functools.partial(_ret_decode_kernel, t_new=t_new),
        grid=(b,),
        in_specs=[pl.BlockSpec((1, DEC_ROWS, RET_IN_WIDTH), lambda i: (i, 0, 0)),
                  pl.BlockSpec((1, RET_HEADS, RET_DK, RET_DV), lambda i: (i, 0, 0, 0)),
                  pl.BlockSpec((1, RET_V_WIDTH), lambda i: (0, 0))],
        out_specs=[pl.BlockSpec((1, DEC_ROWS, RET_V_WIDTH), lambda i: (i, 0, 0)),
                   pl.BlockSpec((1, RET_HEADS, RET_DK, RET_DV), lambda i: (i, 0, 0, 0))],
        out_shape=[jax.ShapeDtypeStruct((b, DEC_ROWS, RET_V_WIDTH), F32),
                   jax.ShapeDtypeStruct((b, RET_HEADS, RET_DK, RET_DV), F32)],
        compiler_params=_cparams(("parallel",)),
        name="ret_decode",
    )(qkvg, state, g_norm)


def _pad_rows(x, b, t):
    return jnp.pad(x.reshape(b, t, x.shape[-1]), ((0, 0), (0, DEC_ROWS - t), (0, 0)))


def _moe_layer(h_all, route_all, p_prompt, p_sample, g_ffn, g_ple, w_gate_e, w_up_e, w_down_e, w_ple_gate,
               w_ple_proj, layer, n_prompt, n_sample):
    n_tok = n_prompt + n_sample
    n_pairs = PAIR_ROWS * LANES
    eid = route_all[:, :2].astype(jnp.int32).T.reshape(-1)
    eid = jnp.pad(eid, (0, n_pairs - 2 * n_tok), constant_values=-1).reshape(PAIR_ROWS, LANES)
    pos, tile_tab = _moe_meta(eid)
    inv = _moe_invert(pos.reshape(1, n_pairs), n_tok).reshape(MAX_TILES, 1, MOE_TILE)
    max_tiles = (2 * n_tok + N_EXPERTS * (MOE_TILE - 1)) // MOE_TILE
    y_sorted = _moe_experts(tile_tab[0], inv, h_all, g_ffn, w_gate_e, w_up_e, w_down_e, layer, max_tiles)
    pos2 = pos.reshape(-1)[:2 * n_tok].reshape(2, n_tok)
    tm_p, tm_s = 256, n_sample
    outs = []
    for lo, rows, tm, p, nm in ((0, n_prompt, tm_p, p_prompt, "combine_prompt"),
                                (n_prompt, n_sample, tm_s, p_sample, "combine_sample")):
        pos0 = pos2[0, lo:lo + rows].reshape(rows // tm, 1, tm)
        pos1 = pos2[1, lo:lo + rows].reshape(rows // tm, 1, tm)
        outs.append(_moe_combine_ple(pos0, pos1, y_sorted, h_all, route_all, p, g_ple, w_ple_gate, w_ple_proj,
                                     layer=layer, tm=tm, rows=rows, row_block0=lo // tm, name=nm))
    return outs


def kernel(x_prompt, x_sample, cache_k_swa, cache_v_swa, state_ret, p_prompt, p_sample, g_mix, g_ffn, g_ple,
           swa_w_qkv, swa_g_q, swa_g_k, swa_sinks, swa_w_o, ret_w_in, ret_g_norm, ret_w_o, moe_w_group,
           moe_b_group, moe_w_expert, moe_b_expert, moe_w_gate, moe_w_up, moe_w_down, ple_w_proj, ple_w_gate):
    batch, seq, d = x_prompt.shape
    dec_b, dec_t, _ = x_sample.shape
    n_prompt, n_sample = batch * seq, dec_b * dec_t
    n_tok = n_prompt + n_sample
    assert 2 * n_tok <= PAIR_ROWS * LANES and 2 * n_tok + N_EXPERTS * (MOE_TILE - 1) <= (MAX_TILES - 1) * MOE_TILE
    assert dec_t <= DEC_ROWS and n_prompt % 1024 == 0 and n_sample % 8 == 0

    xp = x_prompt.reshape(n_prompt, d)
    xs = x_sample.reshape(n_sample, d)
    pp = p_prompt.reshape(DEPTH, n_prompt, PLE_DIM)
    ps = p_sample.reshape(DEPTH, n_sample, PLE_DIM)
    row = lambda a: a.reshape(1, -1)

    def router_w(i):
        w = jnp.concatenate([moe_w_group[i], moe_w_expert[i]], axis=1)
        b = jnp.concatenate([moe_b_group[i], moe_b_expert[i]])
        pad = ROUTE_LANES - w.shape[1]
        return jnp.pad(w, ((0, 0), (0, pad))), jnp.pad(b, (0, pad)).reshape(1, ROUTE_LANES)

    g_q2, g_k2 = row(jnp.tile(swa_g_q[0], 2)), row(jnp.tile(swa_g_k[0], 2))
    w_r, b_r = router_w(0)
    qkv_p = _norm_proj(xp, row(g_mix[0]), swa_w_qkv[0].astype(BF16), tm=1024, tn=512, hi=False, name="qkv_prompt")
    o_p, k_p, v_p = _swa_prompt(qkv_p, g_q2, g_k2, swa_sinks[0], batch)

    qkv_s = _norm_proj(xs, row(g_mix[0]), swa_w_qkv[0], tm=n_sample, tn=512, hi=True, name="qkv_sample")
    ck = cache_k_swa[0].reshape(dec_b, WINDOW, KV_WIDTH)
    cv = cache_v_swa[0].reshape(dec_b, WINDOW, KV_WIDTH)
    o_s, kn_s = _swa_decode(_pad_rows(qkv_s, dec_b, dec_t), ck, cv, g_q2, g_k2, swa_sinks[0], dec_t)
    o_s = o_s[:, :dec_t].reshape(n_sample, Q_WIDTH)
    tail = _outproj_router(o_s, swa_w_o[0], xs, row(g_ffn[0]), w_r, b_r, tm=n_sample, tk=1024, hi=True,
                           tail=None, name="swa_out_sample")
    h_all, route_all = _outproj_router(o_p, swa_w_o[0].astype(BF16), xp, row(g_ffn[0]), w_r.astype(BF16), b_r,
                                       tm=512, tk=1024, hi=False, tail=tail, name="swa_out_prompt")
    k_s = jnp.concatenate([ck[:, dec_t:], kn_s[:, :dec_t]], axis=1)
    v_new = qkv_s[:, Q_WIDTH + KV_WIDTH:].reshape(dec_b, dec_t, KV_WIDTH)
    v_s = jnp.concatenate([cv[:, dec_t:], v_new], axis=1)

    h_p, h_s = _moe_layer(h_all, route_all, pp, ps, row(g_ffn[0]), row(g_ple[0]), moe_w_gate, moe_w_up, moe_w_down,
                          ple_w_gate[0].astype(BF16), ple_w_proj[0].astype(BF16), 0, n_prompt, n_sample)

    w_in = ret_w_in[0].astype(BF16)
    w_o = ret_w_o[0].astype(BF16)
    w_r, b_r = router_w(1)
    w_r = w_r.astype(BF16)
    qkvg_p = _norm_proj(h_p, row(g_mix[1]), w_in, tm=1024, tn=512, hi=False, name="ret_in_prompt")
    y_p, s_p = _ret_prompt(qkvg_p, row(ret_g_norm[0]), batch)

    qkvg_s = _norm_proj(h_s, row(g_mix[1]), w_in, tm=n_sample, tn=512, hi=False, name="ret_in_sample")
    y_s, s_s = _ret_decode(_pad_rows(qkvg_s, dec_b, dec_t), state_ret[0], row(ret_g_norm[0]), dec_t)
    y_s = y_s[:, :dec_t].reshape(n_sample, RET_V_WIDTH)
    tail = _outproj_router(y_s, w_o, h_s, row(g_ffn[1]), w_r, b_r, tm=n_sample, tk=1024, hi=False, tail=None,
                           name="ret_out_sample")
    h_all, route_all = _outproj_router(y_p, w_o, h_p, row(g_ffn[1]), w_r, b_r, tm=512, tk=1024, hi=False,
                                       tail=tail, name="ret_out_prompt")

    y_prompt, y_sample = _moe_layer(h_all, route_all, pp, ps, row(g_ffn[1]), row(g_ple[1]), moe_w_gate, moe_w_up,
                                    moe_w_down, ple_w_gate[1].astype(BF16), ple_w_proj[1].astype(BF16), 1,
                                    n_prompt, n_sample)

    kv_shape = (1, -1, WINDOW, SWA_KV_HEADS, SWA_HEAD_DIM)
    return (y_prompt.reshape(batch, seq, d), y_sample.reshape(dec_b, dec_t, d),
            k_p.reshape(kv_shape), v_p.reshape(kv_shape), s_p[None],
            k_s.reshape(kv_shape), v_s.reshape(kv_shape), s_s[None])
```

```python
import functools
import math

import jax
import jax.numpy as jnp
from jax import lax
from jax.experimental import pallas as pl
from jax.experimental.pallas import tpu as pltpu

D_MODEL = 2048
DEPTH = 2
SWA_HEADS = 32
SWA_KV_HEADS = 8
SWA_HEAD_DIM = 64
Q_WIDTH = SWA_HEADS * SWA_HEAD_DIM
KV_WIDTH = SWA_KV_HEADS * SWA_HEAD_DIM
QKV_WIDTH = Q_WIDTH + 2 * KV_WIDTH
WINDOW = 128
ATTN_SCALE = SWA_HEAD_DIM ** -0.5
RET_HEADS = 8
RET_DK = D_MODEL // RET_HEADS
RET_DV = 2 * D_MODEL // RET_HEADS
RET_K_WIDTH = RET_HEADS * RET_DK
RET_V_WIDTH = RET_HEADS * RET_DV
RET_IN_WIDTH = 2 * RET_K_WIDTH + 2 * RET_V_WIDTH
RET_CHUNK = 128
N_GROUPS = 4
EXPERTS_PER_GROUP = 8
N_EXPERTS = N_GROUPS * EXPERTS_PER_GROUP
D_EXPERT = D_MODEL // 4
PLE_DIM = 256
NORM_EPS = 1e-6
GN_EPS = 1e-5
NEG_INF = -1e30

LANES = 128
ROUTE_LANES = LANES
EXPERT_LANE0 = N_GROUPS
MOE_TILE = 256
PAIR_ROWS = 256
MAX_TILES = LANES
VMEM_LIMIT = 56 * 1024 * 1024
ROW_TILES = D_MODEL // LANES
ROW_PITCH = ROW_TILES + 4

F32 = jnp.float32
BF16 = jnp.bfloat16
HIGHEST = lax.Precision.HIGHEST

ALIBI_SLOPES = tuple(2.0 ** (-8.0 * (h + 1) / SWA_HEADS) for h in range(SWA_HEADS))
RET_LOG_GAMMA = tuple(math.log(1.0 - 2.0 ** (-5.0 - h)) for h in range(RET_HEADS))


def _cparams(sem):
    return pltpu.CompilerParams(dimension_semantics=sem, vmem_limit_bytes=VMEM_LIMIT)


def _mm_dtype(hi):
    return F32 if hi else BF16


def _mm_prec(hi):
    return HIGHEST if hi else None


def _rms(x, g):
    return x * lax.rsqrt(jnp.mean(x * x, axis=-1, keepdims=True) + NORM_EPS) * g


def _store_rows(ref, x):
    n = x.shape[0]
    for s in range(ROW_PITCH):
        piece = x[:, s * LANES:(s + 1) * LANES] if s < ROW_TILES else jnp.zeros((n, LANES), F32)
        ref[pl.ds(s, n, stride=ROW_PITCH), :] = piece


def _load_rows(ref, n):
    return jnp.concatenate([ref[pl.ds(s, n, stride=ROW_PITCH), :] for s in range(ROW_TILES)], axis=1)


def _row_copy(src_hbm, src_row, dst, dst_row, sem):
    return pltpu.make_async_copy(src_hbm.at[pl.ds(src_row * ROW_PITCH, ROW_TILES)],
                                 dst.at[pl.ds(dst_row * ROW_PITCH, ROW_TILES)], sem)


def _rows_wait(src_hbm, dst, n, sem):
    pltpu.make_async_copy(src_hbm.at[pl.ds(0, n * ROW_TILES)], dst.at[pl.ds(0, n * ROW_TILES)], sem).wait()


def _proj_kernel(x_ref, g_ref, w_ref, o_ref, a_ref, *, hi):
    @pl.when(pl.program_id(1) == 0)
    def _():
        a_ref[...] = _rms(x_ref[...], g_ref[...]).astype(a_ref.dtype)

    o_ref[...] = jnp.dot(a_ref[...], w_ref[...], preferred_element_type=F32,
                         precision=_mm_prec(hi)).astype(o_ref.dtype)


def _norm_proj(x, g, w, *, tm, tn, hi, name):
    m, k = x.shape
    n = w.shape[1]
    return pl.pallas_call(
        functools.partial(_proj_kernel, hi=hi),
        grid=(m // tm, n // tn),
        in_specs=[pl.BlockSpec((tm, k), lambda i, j: (i, 0)),
                  pl.BlockSpec((1, k), lambda i, j: (0, 0)),
                  pl.BlockSpec((k, tn), lambda i, j: (0, j))],
        out_specs=pl.BlockSpec((tm, tn), lambda i, j: (i, j)),
        out_shape=jax.ShapeDtypeStruct((m, n), F32),
        scratch_shapes=[pltpu.VMEM((tm, k), _mm_dtype(hi))],
        compiler_params=_cparams(("parallel", "arbitrary")),
        name=name,
    )(x, g, w)


def _head_norm(x, g2):
    lane = lax.broadcasted_iota(jnp.int32, (1, LANES), 1)
    lo = lane < SWA_HEAD_DIM
    outs = []
    for j in range(x.shape[1] // LANES):
        xb = x[:, j * LANES:(j + 1) * LANES]
        x2 = xb * xb
        s_lo = jnp.sum(jnp.where(lo, x2, 0.0), axis=-1, keepdims=True)
        s_hi = jnp.sum(jnp.where(lo, 0.0, x2), axis=-1, keepdims=True)
        r = jnp.where(lo, lax.rsqrt(s_lo * (1.0 / SWA_HEAD_DIM) + NORM_EPS),
                      lax.rsqrt(s_hi * (1.0 / SWA_HEAD_DIM) + NORM_EPS))
        outs.append(xb * r * g2)
    return jnp.concatenate(outs, axis=1)


def _attn_heads(qs, kn, v, negdist, valid, sink_ref, hi):
    nq, nk = qs.shape[0], kn.shape[0]
    cdt, prec = _mm_dtype(hi), _mm_prec(hi)
    lane = lax.broadcasted_iota(jnp.int32, (1, LANES), 1)
    lo = lane < SWA_HEAD_DIM
    group = SWA_HEADS // SWA_KV_HEADS
    outs = []
    for m in range(KV_WIDTH // LANES):
        kb = kn[:, m * LANES:(m + 1) * LANES]
        vb = v[:, m * LANES:(m + 1) * LANES]
        kr = pltpu.roll(kb, SWA_HEAD_DIM, 1)
        vr = pltpu.roll(vb, SWA_HEAD_DIM, 1)
        for half in range(2):
            kv = 2 * m + half
            k_lo, k_hi = (kb, kr) if half == 0 else (kr, kb)
            v_lo, v_hi = (vb, vr) if half == 0 else (vr, vb)
            kbd = jnp.concatenate([jnp.where(lo, k_lo, 0.0), jnp.where(lo, 0.0, k_hi)], axis=0).astype(cdt)
            vbd = jnp.concatenate([jnp.where(lo, v_lo, 0.0), jnp.where(lo, 0.0, v_hi)], axis=0).astype(cdt)
            c0 = kv * group * SWA_HEAD_DIM
            q2 = jnp.concatenate([qs[:, c0:c0 + LANES], qs[:, c0 + LANES:c0 + 2 * LANES]], axis=0).astype(cdt)
            s_all = lax.dot_general(q2, kbd, (((1,), (1,)), ((), ())), preferred_element_type=F32, precision=prec)
            e_rows, inv = [], []
            for r in range(2):
                e_cols = []
                for c in range(2):
                    h = kv * group + 2 * r + c
                    s = s_all[r * nq:(r + 1) * nq, c * nk:(c + 1) * nk]
                    s = jnp.where(valid, s + ALIBI_SLOPES[h] * negdist, NEG_INF)
                    sk = sink_ref[h]
                    mx = jnp.maximum(jnp.max(s, axis=-1, keepdims=True), sk)
                    e = jnp.exp(s - mx)
                    den = jnp.sum(e, axis=-1, keepdims=True) + jnp.exp(sk - mx)
                    e_cols.append(e.astype(cdt))
                    inv.append(1.0 / den)
                e_rows.append(jnp.concatenate(e_cols, axis=1))
            p2 = jnp.concatenate(e_rows, axis=0)
            o2 = jnp.dot(p2, vbd, preferred_element_type=F32, precision=prec)
            for r in range(2):
                outs.append(o2[r * nq:(r + 1) * nq] * jnp.where(lo, inv[2 * r], inv[2 * r + 1]))
    return jnp.concatenate(outs, axis=1)


def _swa_prompt_kernel(sink_ref, q_ref, k_ref, v_ref, gq_ref, gk_ref, o_ref, ko_ref, vo_ref, kprev, vprev):
    n = pl.program_id(1)

    @pl.when(n == 0)
    def _():
        kprev[...] = jnp.zeros_like(kprev)
        vprev[...] = jnp.zeros_like(vprev)

    qs = _head_norm(q_ref[...], gq_ref[...]) * ATTN_SCALE
    kn = _head_norm(k_ref[...], gk_ref[...])
    v = v_ref[...]
    kcat = jnp.concatenate([kprev[...], kn], axis=0)
    vcat = jnp.concatenate([vprev[...], v], axis=0)
    qi = lax.broadcasted_iota(jnp.int32, (WINDOW, 2 * WINDOW), 0)
    kj = lax.broadcasted_iota(jnp.int32, (WINDOW, 2 * WINDOW), 1)
    dist = qi - kj + WINDOW
    valid = (dist >= 0) & (dist <= WINDOW) & ((n > 0) | (kj >= WINDOW))
    o = _attn_heads(qs, kcat, vcat, -dist.astype(F32), valid, sink_ref, False)
    o_ref[...] = o.astype(o_ref.dtype)
    kprev[...] = kn
    vprev[...] = v

    @pl.when(n == pl.num_programs(1) - 1)
    def _():
        ko_ref[0] = kn
        vo_ref[0] = v


def _swa_prompt(qkv, g_q, g_k, sinks, batch):
    rows = qkv.shape[0]
    nb = rows // batch // WINDOW
    kblk = Q_WIDTH // KV_WIDTH
    return pl.pallas_call(
        _swa_prompt_kernel,
        grid=(batch, nb),
        in_specs=[pl.BlockSpec(memory_space=pltpu.SMEM),
                  pl.BlockSpec((WINDOW, Q_WIDTH), lambda b, n: (b * nb + n, 0)),
                  pl.BlockSpec((WINDOW, KV_WIDTH), lambda b, n: (b * nb + n, kblk)),
                  pl.BlockSpec((WINDOW, KV_WIDTH), lambda b, n: (b * nb + n, kblk + 1)),
                  pl.BlockSpec((1, LANES), lambda b, n: (0, 0)),
                  pl.BlockSpec((1, LANES), lambda b, n: (0, 0))],
        out_specs=[pl.BlockSpec((WINDOW, Q_WIDTH), lambda b, n: (b * nb + n, 0)),
                   pl.BlockSpec((1, WINDOW, KV_WIDTH), lambda b, n: (b, 0, 0)),
                   pl.BlockSpec((1, WINDOW, KV_WIDTH), lambda b, n: (b, 0, 0))],
        out_shape=[jax.ShapeDtypeStruct((rows, Q_WIDTH), BF16),
                   jax.ShapeDtypeStruct((batch, WINDOW, KV_WIDTH), F32),
                   jax.ShapeDtypeStruct((batch, WINDOW, KV_WIDTH), F32)],
        scratch_shapes=[pltpu.VMEM((WINDOW, KV_WIDTH), F32), pltpu.VMEM((WINDOW, KV_WIDTH), F32)],
        compiler_params=_cparams(("parallel", "arbitrary")),
        name="swa_prompt",
    )(sinks, qkv, qkv, qkv, g_q, g_k)


DEC_ROWS = 8


def _swa_decode_kernel(sink_ref, qkv_ref, ck_ref, cv_ref, gq_ref, gk_ref, o_ref, kn_ref, *, t_new):
    x = qkv_ref[0]
    qs = _head_norm(x[:, :Q_WIDTH], gq_ref[...]) * ATTN_SCALE
    kn = _head_norm(x[:, Q_WIDTH:Q_WIDTH + KV_WIDTH], gk_ref[...])
    v = x[:, Q_WIDTH + KV_WIDTH:]
    fill = jnp.zeros((WINDOW - DEC_ROWS, KV_WIDTH), F32)
    kcat = jnp.concatenate([ck_ref[0], kn, fill], axis=0)
    vcat = jnp.concatenate([cv_ref[0], v, fill], axis=0)
    ti = lax.broadcasted_iota(jnp.int32, (DEC_ROWS, 2 * WINDOW), 0)
    sj = lax.broadcasted_iota(jnp.int32, (DEC_ROWS, 2 * WINDOW), 1)
    dist = ti + WINDOW - sj
    valid = (dist >= 0) & (dist <= WINDOW) & (sj < WINDOW + t_new)
    o_ref[0] = _attn_heads(qs, kcat, vcat, -dist.astype(F32), valid, sink_ref, True)
    kn_ref[0] = kn


def _swa_decode(qkv, cache_k, cache_v, g_q, g_k, sinks, t_new):
    b = qkv.shape[0]
    return pl.pallas_call(
        functools.partial(_swa_decode_kernel, t_new=t_new),
        grid=(b,),
        in_specs=[pl.BlockSpec(memory_space=pltpu.SMEM),
                  pl.BlockSpec((1, DEC_ROWS, QKV_WIDTH), lambda i: (i, 0, 0)),
                  pl.BlockSpec((1, WINDOW, KV_WIDTH), lambda i: (i, 0, 0)),
                  pl.BlockSpec((1, WINDOW, KV_WIDTH), lambda i: (i, 0, 0)),
                  pl.BlockSpec((1, LANES), lambda i: (0, 0)),
                  pl.BlockSpec((1, LANES), lambda i: (0, 0))],
        out_specs=[pl.BlockSpec((1, DEC_ROWS, Q_WIDTH), lambda i: (i, 0, 0)),
                   pl.BlockSpec((1, DEC_ROWS, KV_WIDTH), lambda i: (i, 0, 0))],
        out_shape=[jax.ShapeDtypeStruct((b, DEC_ROWS, Q_WIDTH), F32),
                   jax.ShapeDtypeStruct((b, DEC_ROWS, KV_WIDTH), F32)],
        compiler_params=_cparams(("parallel",)),
        name="swa_decode",
    )(sinks, qkv, cache_k, cache_v, g_q, g_k)


def _route(logits):
    lane = lax.broadcasted_iota(jnp.int32, logits.shape, 1).astype(F32)
    big = float(ROUTE_LANES)
    lg = jnp.where(lane < N_GROUPS, logits, NEG_INF)
    mg = jnp.max(lg, axis=-1, keepdims=True)
    gsel = jnp.min(jnp.where(lg == mg, lane, big), axis=-1, keepdims=True)
    pg_sel = 1.0 / jnp.sum(jnp.where(lane < N_GROUPS, jnp.exp(logits - mg), 0.0), axis=-1, keepdims=True)
    first = EXPERT_LANE0 + EXPERTS_PER_GROUP * gsel
    le = jnp.where((lane >= first) & (lane < first + EXPERTS_PER_GROUP), logits, NEG_INF)
    m1 = jnp.max(le, axis=-1, keepdims=True)
    i1 = jnp.min(jnp.where(le == m1, lane, big), axis=-1, keepdims=True)
    le2 = jnp.where(lane == i1, NEG_INF, le)
    m2 = jnp.max(le2, axis=-1, keepdims=True)
    i2 = jnp.min(jnp.where(le2 == m2, lane, big), axis=-1, keepdims=True)
    t = jnp.exp(m2 - m1)
    w0 = pg_sel / (1.0 + t)
    w1 = pg_sel * t / (1.0 + t)
    e0 = i1 - EXPERT_LANE0
    e1 = i2 - EXPERT_LANE0
    return jnp.where(lane == 0, e0, jnp.where(lane == 1, e1, jnp.where(lane == 2, w0, jnp.where(lane == 3, w1, 0.0))))


def _outproj_kernel(*refs, hi, n_tail):
    o_ref, w_ref, h_ref, g_ref, wr_ref, br_ref = refs[:6]
    h1_ref, route_ref, h1x_ref, acc_ref = refs[6 + n_tail:]
    i, k = pl.program_id(0), pl.program_id(1)
    n_main = pl.num_programs(0) - (1 if n_tail else 0)
    cdt, prec = _mm_dtype(hi), _mm_prec(hi)

    @pl.when(i < n_main)
    def _():
        @pl.when(k == 0)
        def _():
            acc_ref[...] = h_ref[...]

        acc_ref[...] += jnp.dot(o_ref[...].astype(cdt), w_ref[...], preferred_element_type=F32, precision=prec)

        @pl.when(k == pl.num_programs(1) - 1)
        def _():
            h1 = acc_ref[...]
            h1_ref[...] = h1
            _store_rows(h1x_ref, h1)
            xn = _rms(h1, g_ref[...]).astype(cdt)
            logits = jnp.dot(xn, wr_ref[...], preferred_element_type=F32, precision=prec) + br_ref[...]
            route_ref[...] = _route(logits)

    if n_tail:
        @pl.when((i == n_main) & (k == pl.num_programs(1) - 1))
        def _():
            for src, dst in zip(refs[6:9], (h1_ref, route_ref, h1x_ref)):
                rows = src.shape[0]
                dst[:rows] = src[...]
                dst[rows:] = jnp.zeros((dst.shape[0] - rows, dst.shape[1]), F32)


def _outproj_router(o, w, h, g_ffn, w_r, b_r, *, tm, tk, hi, tail, name):
    m, kdim = o.shape
    d = w.shape[1]
    n_main, n_k = m // tm, kdim // tk
    n_tail = 0 if tail is None else 3
    rows_i = lambda i: jnp.minimum(i, n_main - 1)
    k_i = lambda i, k: jnp.where(i < n_main, k, n_k - 1)
    in_specs = [pl.BlockSpec((tm, tk), lambda i, k: (rows_i(i), k_i(i, k))),
                pl.BlockSpec((tk, d), lambda i, k: (k_i(i, k), 0)),
                pl.BlockSpec((tm, d), lambda i, k: (rows_i(i), 0)),
                pl.BlockSpec((1, d), lambda i, k: (0, 0)),
                pl.BlockSpec((d, ROUTE_LANES), lambda i, k: (0, 0)),
                pl.BlockSpec((1, ROUTE_LANES), lambda i, k: (0, 0))]
    args = [o, w, h, g_ffn, w_r, b_r]
    total_rows = m
    if tail is not None:
        assert tail[0].shape[0] <= tm
        in_specs += [pl.BlockSpec(t.shape, lambda i, k: (0, 0)) for t in tail]
        args += list(tail)
        total_rows += tail[0].shape[0]
    return pl.pallas_call(
        functools.partial(_outproj_kernel, hi=hi, n_tail=n_tail),
        grid=(n_main + (1 if n_tail else 0), n_k),
        in_specs=in_specs,
        out_specs=[pl.BlockSpec((tm, d), lambda i, k: (i, 0)),
                   pl.BlockSpec((tm, ROUTE_LANES), lambda i, k: (i, 0)),
                   pl.BlockSpec((tm * ROW_PITCH, LANES), lambda i, k: (i, 0))],
        out_shape=[jax.ShapeDtypeStruct((total_rows, d), F32),
                   jax.ShapeDtypeStruct((total_rows, ROUTE_LANES), F32),
                   jax.ShapeDtypeStruct((total_rows * ROW_PITCH, LANES), F32)],
        scratch_shapes=[pltpu.VMEM((tm, d), F32)],
        compiler_params=_cparams(("parallel", "arbitrary")),
        name=name,
    )(*args)


def _meta_kernel(eid_ref, pos_ref, tile_ref):
    eid = eid_ref[...]
    r_i = lax.broadcasted_iota(jnp.int32, (LANES, LANES), 0)
    c_i = lax.broadcasted_iota(jnp.int32, (LANES, LANES), 1)
    upper = (r_i <= c_i).astype(BF16)
    ones = jnp.ones((LANES, LANES), BF16)
    rr = lax.broadcasted_iota(jnp.int32, (PAIR_ROWS, PAIR_ROWS), 0)
    cc = lax.broadcasted_iota(jnp.int32, (PAIR_ROWS, PAIR_ROWS), 1)
    below = (cc < rr).astype(BF16)
    tile_start = lax.broadcasted_iota(jnp.int32, (1, LANES), 1).astype(F32) * MOE_TILE

    pos = jnp.zeros((PAIR_ROWS, LANES), F32)
    hit = jnp.zeros((PAIR_ROWS, LANES), F32)
    start = jnp.zeros((1, LANES), F32)
    tile_e = jnp.zeros((1, LANES), F32)
    last_e = jnp.zeros((1, LANES), F32)
    for e in range(N_EXPERTS):
        mf = (eid == e).astype(F32)
        mb = mf.astype(BF16)
        incl = jnp.dot(mb, upper, preferred_element_type=F32)
        row_tot = jnp.dot(mb, ones, preferred_element_type=F32)
        row_off = jnp.dot(below, row_tot.astype(BF16), preferred_element_type=F32)
        rank = incl - mf + row_off
        cnt = row_off[PAIR_ROWS - 1:PAIR_ROWS, :] + row_tot[PAIR_ROWS - 1:PAIR_ROWS, :]
        padded = jnp.floor((cnt + (MOE_TILE - 1)) * (1.0 / MOE_TILE)) * MOE_TILE
        pos = pos + mf * (start + rank)
        hit = hit + mf
        start = start + padded
        tile_e = tile_e + (start <= tile_start).astype(F32)
        last_e = jnp.where(cnt > 0, float(e), last_e)
    pos_ref[...] = jnp.where(hit > 0, pos, -1.0).astype(jnp.int32)
    n_tiles = start * (1.0 / MOE_TILE)
    lane = lax.broadcasted_iota(jnp.int32, (1, LANES), 1)
    tile_e = jnp.minimum(tile_e, last_e)
    rec = jnp.where(lane == MAX_TILES - 1, n_tiles, tile_e)
    tile_ref[...] = jnp.broadcast_to(rec, (8, LANES)).astype(jnp.int32)


def _moe_meta(eid_pairs):
    return pl.pallas_call(
        _meta_kernel,
        out_shape=[jax.ShapeDtypeStruct((PAIR_ROWS, LANES), jnp.int32),
                   jax.ShapeDtypeStruct((8, LANES), jnp.int32)],
        compiler_params=pltpu.CompilerParams(vmem_limit_bytes=VMEM_LIMIT),
        name="moe_meta",
    )(eid_pairs)


INV_CHUNK = 2048


def _invert_kernel(pos_ref, inv_ref, *, n_tokens):
    n_pairs = pos_ref.shape[1]
    hi_acc = jnp.zeros((MAX_TILES, MOE_TILE), F32)
    lo_acc = jnp.zeros((MAX_TILES, MOE_TILE), F32)
    t_iota = lax.broadcasted_iota(jnp.int32, (MAX_TILES, INV_CHUNK), 0)
    r_iota = lax.broadcasted_iota(jnp.int32, (MOE_TILE, INV_CHUNK), 0)
    for c in range(n_pairs // INV_CHUNK):
        pos = pos_ref[:, c * INV_CHUNK:(c + 1) * INV_CHUNK]
        p = lax.broadcasted_iota(jnp.int32, (1, INV_CHUNK), 1) + c * INV_CHUNK
        tok = jnp.where(p >= n_tokens, p - n_tokens, p)
        in_tile = (pos >> (MOE_TILE.bit_length() - 1)) == t_iota
        a_hi = jnp.where(in_tile, (tok >> 7).astype(F32), 0.0).astype(BF16)
        a_lo = jnp.where(in_tile, (tok & 127).astype(F32), 0.0).astype(BF16)
        b = ((pos & (MOE_TILE - 1)) == r_iota).astype(BF16)
        dn = (((1,), (1,)), ((), ()))
        hi_acc = hi_acc + lax.dot_general(a_hi, b, dn, preferred_element_type=F32)
        lo_acc = lo_acc + lax.dot_general(a_lo, b, dn, preferred_element_type=F32)
    inv_ref[...] = (hi_acc * 128.0 + lo_acc).astype(jnp.int32)


def _moe_invert(pos_row, n_tokens):
    return pl.pallas_call(
        functools.partial(_invert_kernel, n_tokens=n_tokens),
        out_shape=jax.ShapeDtypeStruct((MAX_TILES, MOE_TILE), jnp.int32),
        compiler_params=pltpu.CompilerParams(vmem_limit_bytes=VMEM_LIMIT),
        name="moe_invert",
    )(pos_row)


def _experts_kernel(tile_ref, inv_cur, inv_nxt, hx_hbm, g_ref, wg_ref, wu_ref, wd_ref, y_ref,
                    xbuf, sems, wg_bf, wu_bf, wd_bf):
    t = pl.program_id(0)
    n_tiles = tile_ref[MAX_TILES - 1]
    slot = t % 2

    def issue(inv_ref, s):
        def body(r, carry):
            _row_copy(hx_hbm, inv_ref[0, 0, r], xbuf.at[s], r, sems.at[s]).start()
            return carry
        lax.fori_loop(0, MOE_TILE, body, 0, unroll=8)

    @pl.when(t == 0)
    def _():
        issue(inv_cur, 0)

    @pl.when(t + 1 < n_tiles)
    def _():
        issue(inv_nxt, 1 - slot)

    @pl.when(t < n_tiles)
    def _():
        _rows_wait(hx_hbm, xbuf.at[slot], MOE_TILE, sems.at[slot])

        @pl.when((t == 0) | (tile_ref[t] != tile_ref[jnp.maximum(t - 1, 0)]))
        def _():
            wg_bf[...] = wg_ref[...].astype(BF16)
            wu_bf[...] = wu_ref[...].astype(BF16)
            wd_bf[...] = wd_ref[...].astype(BF16)

        xn = _rms(_load_rows(xbuf.at[slot], MOE_TILE), g_ref[...]).astype(BF16)
        hg = jnp.dot(xn, wg_bf[...], preferred_element_type=F32)
        hu = jnp.dot(xn, wu_bf[...], preferred_element_type=F32)
        hid = (hg / (1.0 + jnp.exp(-hg)) * hu).astype(BF16)
        _store_rows(y_ref, jnp.dot(hid, wd_bf[...], preferred_element_type=F32))

    @pl.when(t >= n_tiles)
    def _():
        y_ref[...] = jnp.zeros_like(y_ref)


def _moe_experts(tile_tab, inv, hx_all, g_ffn, w_gate, w_up, w_down, layer, n_steps):
    d = D_MODEL

    def widx(t, tab):
        return (layer, tab[t], 0, 0)

    grid_spec = pltpu.PrefetchScalarGridSpec(
        num_scalar_prefetch=1,
        grid=(n_steps,),
        in_specs=[pl.BlockSpec((1, 1, MOE_TILE), lambda t, tab: (t, 0, 0), memory_space=pltpu.SMEM),
                  pl.BlockSpec((1, 1, MOE_TILE), lambda t, tab: (t + 1, 0, 0), memory_space=pltpu.SMEM),
                  pl.BlockSpec(memory_space=pl.ANY),
                  pl.BlockSpec((1, d), lambda t, tab: (0, 0)),
                  pl.BlockSpec((None, None, d, D_EXPERT), widx),
                  pl.BlockSpec((None, None, d, D_EXPERT), widx),
                  pl.BlockSpec((None, None, D_EXPERT, d), widx)],
        out_specs=pl.BlockSpec((MOE_TILE * ROW_PITCH, LANES), lambda t, tab: (t, 0)),
        scratch_shapes=[pltpu.VMEM((2, MOE_TILE * ROW_PITCH, LANES), F32),
                        pltpu.SemaphoreType.DMA((2,)),
                        pltpu.VMEM((d, D_EXPERT), BF16),
                        pltpu.VMEM((d, D_EXPERT), BF16),
                        pltpu.VMEM((D_EXPERT, d), BF16)],
    )
    return pl.pallas_call(
        _experts_kernel,
        grid_spec=grid_spec,
        out_shape=jax.ShapeDtypeStruct((n_steps * MOE_TILE * ROW_PITCH, LANES), F32),
        compiler_params=_cparams(("arbitrary",)),
        name="moe_experts",
    )(tile_tab, inv, inv, hx_all, g_ffn, w_gate, w_up, w_down)


def _combine_kernel(p0_cur, p0_nxt, p1_cur, p1_nxt, y_hbm, h_ref, route_ref, p_ref, g_ref, wg_ref, wp_ref,
                    o_ref, ybuf, sems, *, tm):
    i = pl.program_id(0)
    slot = i % 2

    def issue(p0, p1, s):
        def body(r, carry):
            _row_copy(y_hbm, p0[0, 0, r], ybuf.at[s, 0], r, sems.at[s]).start()
            _row_copy(y_hbm, p1[0, 0, r], ybuf.at[s, 1], r, sems.at[s]).start()
            return carry
        lax.fori_loop(0, tm, body, 0, unroll=8)

    @pl.when(i == 0)
    def _():
        issue(p0_cur, p1_cur, 0)

    @pl.when(i + 1 < pl.num_programs(0))
    def _():
        issue(p0_nxt, p1_nxt, 1 - slot)

    for c in range(2):
        _rows_wait(y_hbm, ybuf.at[slot, c], tm, sems.at[slot])
    route = route_ref[...]
    y0, y1 = _load_rows(ybuf.at[slot, 0], tm), _load_rows(ybuf.at[slot, 1], tm)
    h2 = h_ref[...] + (route[:, 2:3] * y0 + route[:, 3:4] * y1)
    xn = _rms(h2, g_ref[...]).astype(BF16)
    z = jnp.dot(xn, wg_ref[...], preferred_element_type=F32)
    gate = 1.0 / (1.0 + jnp.exp(-z))
    proj = jnp.dot(p_ref[...].astype(BF16), wp_ref[...], preferred_element_type=F32)
    o_ref[...] = h2 + proj * gate


def _moe_combine_ple(pos0, pos1, y_sorted, h_all, route_all, p, g_ple, w_gate, w_proj, *, layer, tm, rows,
                     row_block0, name):
    d = h_all.shape[1]
    n = rows // tm
    last = n - 1
    smem = functools.partial(pl.BlockSpec, (1, 1, tm), memory_space=pltpu.SMEM)
    return pl.pallas_call(
        functools.partial(_combine_kernel, tm=tm),
        grid=(n,),
        in_specs=[smem(lambda i: (i, 0, 0)),
                  smem(lambda i: (jnp.minimum(i + 1, last), 0, 0)),
                  smem(lambda i: (i, 0, 0)),
                  smem(lambda i: (jnp.minimum(i + 1, last), 0, 0)),
                  pl.BlockSpec(memory_space=pl.ANY),
                  pl.BlockSpec((tm, d), lambda i: (row_block0 + i, 0)),
                  pl.BlockSpec((tm, ROUTE_LANES), lambda i: (row_block0 + i, 0)),
                  pl.BlockSpec((None, tm, PLE_DIM), lambda i: (layer, i, 0)),
                  pl.BlockSpec((1, d), lambda i: (0, 0)),
                  pl.BlockSpec((d, d), lambda i: (0, 0)),
                  pl.BlockSpec((PLE_DIM, d), lambda i: (0, 0))],
        out_specs=pl.BlockSpec((tm, d), lambda i: (i, 0)),
        out_shape=jax.ShapeDtypeStruct((rows, d), F32),
        scratch_shapes=[pltpu.VMEM((2, 2, tm * ROW_PITCH, LANES), F32), pltpu.SemaphoreType.DMA((2,))],
        compiler_params=_cparams(("arbitrary",)),
        name=name,
    )(pos0, pos0, pos1, pos1, y_sorted, h_all, route_all, p, g_ple, w_gate, w_proj)


def _ret_head(q, k, v, gt, state, gn, lg, c_real):
    c = q.shape[0]
    k = k * (RET_DK ** -0.5)
    ri = lax.broadcasted_iota(jnp.int32, (c, c), 0)
    ci = lax.broadcasted_iota(jnp.int32, (c, c), 1)
    diff = (ri - ci).astype(F32)
    decay = jnp.where(diff >= 0, jnp.exp(lg * jnp.maximum(diff, 0.0)), 0.0)
    idx = lax.broadcasted_iota(jnp.int32, (c, 1), 0).astype(F32)
    qk = lax.dot_general(q.astype(BF16), k.astype(BF16), (((1,), (1,)), ((), ())), preferred_element_type=F32)
    inner = qk * decay
    q_dec = q * jnp.exp(lg * (idx + 1.0))
    k_dec = k * jnp.exp(lg * (c_real - 1.0 - idx))
    vb = v.astype(BF16)
    out = (jnp.dot(inner.astype(BF16), vb, preferred_element_type=F32)
           + jnp.dot(q_dec.astype(BF16), state.astype(BF16), preferred_element_type=F32))
    new_state = math.exp(lg * c_real) * state + lax.dot_general(
        k_dec.astype(BF16), vb, (((0,), (0,)), ((), ())), preferred_element_type=F32)
    mu = jnp.mean(out, axis=-1, keepdims=True)
    cen = out - mu
    var = jnp.mean(cen * cen, axis=-1, keepdims=True)
    on = cen * lax.rsqrt(var + GN_EPS) * gn
    return gt / (1.0 + jnp.exp(-gt)) * on, new_state


def _ret_prompt_kernel(q_ref, k_ref, v_ref, g_ref, gn_ref, y_ref, so_ref, st_ref):
    c = pl.program_id(1)

    @pl.when(c == 0)
    def _():
        st_ref[...] = jnp.zeros_like(st_ref)

    for h in range(RET_HEADS):
        ks, vs = slice(h * RET_DK, (h + 1) * RET_DK), slice(h * RET_DV, (h + 1) * RET_DV)
        y, new_state = _ret_head(q_ref[:, ks], k_ref[:, ks], v_ref[:, vs], g_ref[:, vs], st_ref[h],
                                 gn_ref[:, vs], RET_LOG_GAMMA[h], RET_CHUNK)
        st_ref[h] = new_state
        y_ref[:, vs] = y.astype(y_ref.dtype)

    @pl.when(c == pl.num_programs(1) - 1)
    def _():
        so_ref[0] = st_ref[...]


def _ret_prompt(qkvg, g_norm, batch):
    rows = qkvg.shape[0]
    nc = rows // batch // RET_CHUNK
    return pl.pallas_call(
        _ret_prompt_kernel,
        grid=(batch, nc),
        in_specs=[pl.BlockSpec((RET_CHUNK, RET_K_WIDTH), lambda b, c: (b * nc + c, 0)),
                  pl.BlockSpec((RET_CHUNK, RET_K_WIDTH), lambda b, c: (b * nc + c, 1)),
                  pl.BlockSpec((RET_CHUNK, RET_V_WIDTH), lambda b, c: (b * nc + c, 1)),
                  pl.BlockSpec((RET_CHUNK, RET_V_WIDTH), lambda b, c: (b * nc + c, 2)),
                  pl.BlockSpec((1, RET_V_WIDTH), lambda b, c: (0, 0))],
        out_specs=[pl.BlockSpec((RET_CHUNK, RET_V_WIDTH), lambda b, c: (b * nc + c, 0)),
                   pl.BlockSpec((1, RET_HEADS, RET_DK, RET_DV), lambda b, c: (b, 0, 0, 0))],
        out_shape=[jax.ShapeDtypeStruct((rows, RET_V_WIDTH), BF16),
                   jax.ShapeDtypeStruct((batch, RET_HEADS, RET_DK, RET_DV), F32)],
        scratch_shapes=[pltpu.VMEM((RET_HEADS, RET_DK, RET_DV), F32)],
        compiler_params=_cparams(("parallel", "arbitrary")),
        name="ret_prompt",
    )(qkvg, qkvg, qkvg, qkvg, g_norm)


def _ret_decode_kernel(x_ref, s_ref, gn_ref, y_ref, so_ref, *, t_new):
    x = x_ref[0]
    for h in range(RET_HEADS):
        q = x[:, h * RET_DK:(h + 1) * RET_DK]
        k = x[:, RET_K_WIDTH + h * RET_DK:RET_K_WIDTH + (h + 1) * RET_DK]
        v0 = 2 * RET_K_WIDTH + h * RET_DV
        g0 = 2 * RET_K_WIDTH + RET_V_WIDTH + h * RET_DV
        y, new_state = _ret_head(q, k, x[:, v0:v0 + RET_DV], x[:, g0:g0 + RET_DV], s_ref[0, h],
                                 gn_ref[:, h * RET_DV:(h + 1) * RET_DV], RET_LOG_GAMMA[h], t_new)
        so_ref[0, h] = new_state
        y_ref[0, :, h * RET_DV:(h + 1) * RET_DV] = y


def _ret_decode(qkvg, state, g_norm, t_new):
    b = qkvg.shape[0]
    return pl.pallas_call(
        functools.partial(_ret_decode_kernel, t_new=t_new),
        grid=(b,),
        in_specs=[pl.BlockSpec((1, DEC_ROWS, RET_IN_WIDTH), lambda i: (i, 0, 0)),
                  pl.BlockSpec((1, RET_HEADS, RET_DK, RET_DV), lambda i: (i, 0, 0, 0)),
                  pl.BlockSpec((1, RET_V_WIDTH), lambda i: (0, 0))],
        out_specs=[pl.BlockSpec((1, DEC_ROWS, RET_V_WIDTH), lambda i: (i, 0, 0)),
                   pl.BlockSpec((1, RET_HEADS, RET_DK, RET_DV), lambda i: (i, 0, 0, 0))],
        out_shape=[jax.ShapeDtypeStruct((b, DEC_ROWS, RET_V_WIDTH), F32),
                   jax.ShapeDtypeStruct((b, RET_HEADS, RET_DK, RET_DV), F32)],
        compiler_params=_cparams(("parallel",)),
        name="ret_decode",
    )(qkvg, state, g_norm)


def _pad_rows(x, b, t):
    return jnp.pad(x.reshape(b, t, x.shape[-1]), ((0, 0), (0, DEC_ROWS - t), (0, 0)))


def _moe_layer(h_all, route_all, hx_all, p_prompt, p_sample, g_ffn, g_ple, w_gate_e, w_up_e, w_down_e, w_ple_gate,
               w_ple_proj, layer, n_prompt, n_sample):
    n_tok = n_prompt + n_sample
    n_pairs = PAIR_ROWS * LANES
    eid = route_all[:, :2].astype(jnp.int32).T.reshape(-1)
    eid = jnp.pad(eid, (0, n_pairs - 2 * n_tok), constant_values=-1).reshape(PAIR_ROWS, LANES)
    pos, tile_tab = _moe_meta(eid)
    inv = _moe_invert(pos.reshape(1, n_pairs), n_tok).reshape(MAX_TILES, 1, MOE_TILE)
    max_tiles = (2 * n_tok + N_EXPERTS * (MOE_TILE - 1)) // MOE_TILE
    y_sorted = _moe_experts(tile_tab[0], inv, hx_all, g_ffn, w_gate_e, w_up_e, w_down_e, layer, max_tiles)
    pos2 = pos.reshape(-1)[:2 * n_tok].reshape(2, n_tok)
    tm_p, tm_s = 256, n_sample
    outs = []
    for lo, rows, tm, p, nm in ((0, n_prompt, tm_p, p_prompt, "combine_prompt"),
                                (n_prompt, n_sample, tm_s, p_sample, "combine_sample")):
        pos0 = pos2[0, lo:lo + rows].reshape(rows // tm, 1, tm)
        pos1 = pos2[1, lo:lo + rows].reshape(rows // tm, 1, tm)
        outs.append(_moe_combine_ple(pos0, pos1, y_sorted, h_all, route_all, p, g_ple, w_ple_gate, w_ple_proj,
                                     layer=layer, tm=tm, rows=rows, row_block0=lo // tm, name=nm))
    return outs


def kernel(x_prompt, x_sample, cache_k_swa, cache_v_swa, state_ret, p_prompt, p_sample, g_mix, g_ffn, g_ple,
           swa_w_qkv, swa_g_q, swa_g_k, swa_sinks, swa_w_o, ret_w_in, ret_g_norm, ret_w_o, moe_w_group,
           moe_b_group, moe_w_expert, moe_b_expert, moe_w_gate, moe_w_up, moe_w_down, ple_w_proj, ple_w_gate):
    batch, seq, d = x_prompt.shape
    dec_b, dec_t, _ = x_sample.shape
    n_prompt, n_sample = batch * seq, dec_b * dec_t
    n_tok = n_prompt + n_sample
    assert 2 * n_tok <= PAIR_ROWS * LANES and 2 * n_tok + N_EXPERTS * (MOE_TILE - 1) <= (MAX_TILES - 1) * MOE_TILE
    assert dec_t <= DEC_ROWS and n_prompt % 1024 == 0 and n_sample % 8 == 0

    xp = x_prompt.reshape(n_prompt, d)
    xs = x_sample.reshape(n_sample, d)
    pp = p_prompt.reshape(DEPTH, n_prompt, PLE_DIM)
    ps = p_sample.reshape(DEPTH, n_sample, PLE_DIM)
    row = lambda a: a.reshape(1, -1)

    def router_w(i):
        w = jnp.concatenate([moe_w_group[i], moe_w_expert[i]], axis=1)
        b = jnp.concatenate([moe_b_group[i], moe_b_expert[i]])
        pad = ROUTE_LANES - w.shape[1]
        return jnp.pad(w, ((0, 0), (0, pad))), jnp.pad(b, (0, pad)).reshape(1, ROUTE_LANES)

    g_q2, g_k2 = row(jnp.tile(swa_g_q[0], 2)), row(jnp.tile(swa_g_k[0], 2))
    w_r, b_r = router_w(0)
    qkv_p = _norm_proj(xp, row(g_mix[0]), swa_w_qkv[0].astype(BF16), tm=1024, tn=512, hi=False, name="qkv_prompt")
    o_p, k_p, v_p = _swa_prompt(qkv_p, g_q2, g_k2, swa_sinks[0], batch)

    qkv_s = _norm_proj(xs, row(g_mix[0]), swa_w_qkv[0], tm=n_sample, tn=512, hi=True, name="qkv_sample")
    ck = cache_k_swa[0].reshape(dec_b, WINDOW, KV_WIDTH)
    cv = cache_v_swa[0].reshape(dec_b, WINDOW, KV_WIDTH)
    o_s, kn_s = _swa_decode(_pad_rows(qkv_s, dec_b, dec_t), ck, cv, g_q2, g_k2, swa_sinks[0], dec_t)
    o_s = o_s[:, :dec_t].reshape(n_sample, Q_WIDTH)
    tail = _outproj_router(o_s, swa_w_o[0], xs, row(g_ffn[0]), w_r, b_r, tm=n_sample, tk=1024, hi=True,
                           tail=None, name="swa_out_sample")
    h_all, route_all, hx_all = _outproj_router(o_p, swa_w_o[0].astype(BF16), xp, row(g_ffn[0]), w_r.astype(BF16),
                                               b_r, tm=512, tk=1024, hi=False, tail=tail, name="swa_out_prompt")
    k_s = jnp.concatenate([ck[:, dec_t:], kn_s[:, :dec_t]], axis=1)
    v_new = qkv_s[:, Q_WIDTH + KV_WIDTH:].reshape(dec_b, dec_t, KV_WIDTH)
    v_s = jnp.concatenate([cv[:, dec_t:], v_new], axis=1)

    h_p, h_s = _moe_layer(h_all, route_all, hx_all, pp, ps, row(g_ffn[0]), row(g_ple[0]), moe_w_gate, moe_w_up,
                          moe_w_down, ple_w_gate[0].astype(BF16), ple_w_proj[0].astype(BF16), 0, n_prompt, n_sample)

    w_in = ret_w_in[0].astype(BF16)
    w_o = ret_w_o[0].astype(BF16)
    w_r, b_r = router_w(1)
    w_r = w_r.astype(BF16)
    qkvg_p = _norm_proj(h_p, row(g_mix[1]), w_in, tm=1024, tn=512, hi=False, name="ret_in_prompt")
    y_p, s_p = _ret_prompt(qkvg_p, row(ret_g_norm[0]), batch)

    qkvg_s = _norm_proj(h_s, row(g_mix[1]), w_in, tm=n_sample, tn=512, hi=False, name="ret_in_sample")
    y_s, s_s = _ret_decode(_pad_rows(qkvg_s, dec_b, dec_t), state_ret[0], row(ret_g_norm[0]), dec_t)
    y_s = y_s[:, :dec_t].reshape(n_sample, RET_V_WIDTH)
    tail = _outproj_router(y_s, w_o, h_s, row(g_ffn[1]), w_r, b_r, tm=n_sample, tk=1024, hi=False, tail=None,
                           name="ret_out_sample")
    h_all, route_all, hx_all = _outproj_router(y_p, w_o, h_p, row(g_ffn[1]), w_r, b_r, tm=512, tk=1024, hi=False,
                                               tail=tail, name="ret_out_prompt")

    y_prompt, y_sample = _moe_layer(h_all, route_all, hx_all, pp, ps, row(g_ffn[1]), row(g_ple[1]), moe_w_gate,
                                    moe_w_up, moe_w_down, ple_w_gate[1].astype(BF16), ple_w_proj[1].astype(BF16), 1,
                                    n_prompt, n_sample)

    kv_shape = (1, -1, WINDOW, SWA_KV_HEADS, SWA_HEAD_DIM)
    return (y_prompt.reshape(batch, seq, d), y_sample.reshape(dec_b, dec_t, d),
            k_p.reshape(kv_shape), v_p.reshape(kv_shape), s_p[None],
            k_s.reshape(kv_shape), v_s.reshape(kv_shape), s_s[None])
```

```python
import functools
import math

import jax
import jax.numpy as jnp
from jax import lax
from jax.experimental import pallas as pl
from jax.experimental.pallas import tpu as pltpu

D_MODEL = 2048
DEPTH = 2
SWA_HEADS = 32
SWA_KV_HEADS = 8
SWA_HEAD_DIM = 64
Q_WIDTH = SWA_HEADS * SWA_HEAD_DIM
KV_WIDTH = SWA_KV_HEADS * SWA_HEAD_DIM
QKV_WIDTH = Q_WIDTH + 2 * KV_WIDTH
WINDOW = 128
ATTN_SCALE = SWA_HEAD_DIM ** -0.5
RET_HEADS = 8
RET_DK = D_MODEL // RET_HEADS
RET_DV = 2 * D_MODEL // RET_HEADS
RET_K_WIDTH = RET_HEADS * RET_DK
RET_V_WIDTH = RET_HEADS * RET_DV
RET_IN_WIDTH = 2 * RET_K_WIDTH + 2 * RET_V_WIDTH
RET_CHUNK = 128
N_GROUPS = 4
EXPERTS_PER_GROUP = 8
N_EXPERTS = N_GROUPS * EXPERTS_PER_GROUP
D_EXPERT = D_MODEL // 4
PLE_DIM = 256
NORM_EPS = 1e-6
GN_EPS = 1e-5
NEG_INF = -1e30

LANES = 128
ROUTE_LANES = LANES
EXPERT_LANE0 = N_GROUPS
MOE_TILE = 256
PAIR_ROWS = 256
MAX_TILES = LANES
VMEM_LIMIT = 56 * 1024 * 1024
ROW_TILES = D_MODEL // LANES
ROW_PITCH = ROW_TILES + 4

F32 = jnp.float32
BF16 = jnp.bfloat16
HIGHEST = lax.Precision.HIGHEST

ALIBI_SLOPES = tuple(2.0 ** (-8.0 * (h + 1) / SWA_HEADS) for h in range(SWA_HEADS))
RET_LOG_GAMMA = tuple(math.log(1.0 - 2.0 ** (-5.0 - h)) for h in range(RET_HEADS))


def _cparams(sem):
    return pltpu.CompilerParams(dimension_semantics=sem, vmem_limit_bytes=VMEM_LIMIT)


def _mm_dtype(hi):
    return F32 if hi else BF16


def _mm_prec(hi):
    return HIGHEST if hi else None


def _rms(x, g):
    return x * lax.rsqrt(jnp.mean(x * x, axis=-1, keepdims=True) + NORM_EPS) * g


def _store_rows(ref, x):
    n = x.shape[0]
    for s in range(ROW_PITCH):
        piece = x[:, s * LANES:(s + 1) * LANES] if s < ROW_TILES else jnp.zeros((n, LANES), F32)
        ref[pl.ds(s, n, stride=ROW_PITCH), :] = piece


def _load_rows(ref, n):
    return jnp.concatenate([ref[pl.ds(s, n, stride=ROW_PITCH), :] for s in range(ROW_TILES)], axis=1)


def _row_copy(src_hbm, src_row, dst, dst_row, sem):
    return pltpu.make_async_copy(src_hbm.at[pl.ds(src_row * ROW_PITCH, ROW_TILES)],
                                 dst.at[pl.ds(dst_row * ROW_PITCH, ROW_TILES)], sem)


def _rows_wait(src_hbm, dst, n, sem):
    pltpu.make_async_copy(src_hbm.at[pl.ds(0, n * ROW_TILES)], dst.at[pl.ds(0, n * ROW_TILES)], sem).wait()


def _proj_kernel(x_ref, g_ref, w_ref, o_ref, a_ref, *, hi):
    @pl.when(pl.program_id(1) == 0)
    def _():
        a_ref[...] = _rms(x_ref[...], g_ref[...]).astype(a_ref.dtype)

    o_ref[...] = jnp.dot(a_ref[...], w_ref[...], preferred_element_type=F32,
                         precision=_mm_prec(hi)).astype(o_ref.dtype)


def _norm_proj(x, g, w, *, tm, tn, hi, name):
    m, k = x.shape
    n = w.shape[1]
    return pl.pallas_call(
        functools.partial(_proj_kernel, hi=hi),
        grid=(m // tm, n // tn),
        in_specs=[pl.BlockSpec((tm, k), lambda i, j: (i, 0)),
                  pl.BlockSpec((1, k), lambda i, j: (0, 0)),
                  pl.BlockSpec((k, tn), lambda i, j: (0, j))],
        out_specs=pl.BlockSpec((tm, tn), lambda i, j: (i, j)),
        out_shape=jax.ShapeDtypeStruct((m, n), F32),
        scratch_shapes=[pltpu.VMEM((tm, k), _mm_dtype(hi))],
        compiler_params=_cparams(("parallel", "arbitrary")),
        name=name,
    )(x, g, w)


def _head_norm(x, g2):
    lane = lax.broadcasted_iota(jnp.int32, (1, LANES), 1)
    lo = lane < SWA_HEAD_DIM
    outs = []
    for j in range(x.shape[1] // LANES):
        xb = x[:, j * LANES:(j + 1) * LANES]
        x2 = xb * xb
        s_lo = jnp.sum(jnp.where(lo, x2, 0.0), axis=-1, keepdims=True)
        s_hi = jnp.sum(jnp.where(lo, 0.0, x2), axis=-1, keepdims=True)
        r = jnp.where(lo, lax.rsqrt(s_lo * (1.0 / SWA_HEAD_DIM) + NORM_EPS),
                      lax.rsqrt(s_hi * (1.0 / SWA_HEAD_DIM) + NORM_EPS))
        outs.append(xb * r * g2)
    return jnp.concatenate(outs, axis=1)


def _attn_heads(qs, kn, v, negdist, valid, sink_ref, hi):
    nq, nk = qs.shape[0], kn.shape[0]
    cdt, prec = _mm_dtype(hi), _mm_prec(hi)
    lane = lax.broadcasted_iota(jnp.int32, (1, LANES), 1)
    lo = lane < SWA_HEAD_DIM
    group = SWA_HEADS // SWA_KV_HEADS
    outs = []
    for m in range(KV_WIDTH // LANES):
        kb = kn[:, m * LANES:(m + 1) * LANES]
        vb = v[:, m * LANES:(m + 1) * LANES]
        kr = pltpu.roll(kb, SWA_HEAD_DIM, 1)
        vr = pltpu.roll(vb, SWA_HEAD_DIM, 1)
        for half in range(2):
            kv = 2 * m + half
            k_lo, k_hi = (kb, kr) if half == 0 else (kr, kb)
            v_lo, v_hi = (vb, vr) if half == 0 else (vr, vb)
            kbd = jnp.concatenate([jnp.where(lo, k_lo, 0.0), jnp.where(lo, 0.0, k_hi)], axis=0).astype(cdt)
            vbd = jnp.concatenate([jnp.where(lo, v_lo, 0.0), jnp.where(lo, 0.0, v_hi)], axis=0).astype(cdt)
            c0 = kv * group * SWA_HEAD_DIM
            q2 = jnp.concatenate([qs[:, c0:c0 + LANES], qs[:, c0 + LANES:c0 + 2 * LANES]], axis=0).astype(cdt)
            s_all = lax.dot_general(q2, kbd, (((1,), (1,)), ((), ())), preferred_element_type=F32, precision=prec)
            e_rows, inv = [], []
            for r in range(2):
                e_cols = []
                for c in range(2):
                    h = kv * group + 2 * r + c
                    s = s_all[r * nq:(r + 1) * nq, c * nk:(c + 1) * nk]
                    s = jnp.where(valid, s + ALIBI_SLOPES[h] * negdist, NEG_INF)
                    sk = sink_ref[h]
                    mx = jnp.maximum(jnp.max(s, axis=-1, keepdims=True), sk)
                    e = jnp.exp(s - mx)
                    den = jnp.sum(e, axis=-1, keepdims=True) + jnp.exp(sk - mx)
                    e_cols.append(e.astype(cdt))
                    inv.append(1.0 / den)
                e_rows.append(jnp.concatenate(e_cols, axis=1))
            p2 = jnp.concatenate(e_rows, axis=0)
            o2 = jnp.dot(p2, vbd, preferred_element_type=F32, precision=prec)
            for r in range(2):
                outs.append(o2[r * nq:(r + 1) * nq] * jnp.where(lo, inv[2 * r], inv[2 * r + 1]))
    return jnp.concatenate(outs, axis=1)


def _swa_prompt_kernel(sink_ref, q_ref, k_ref, v_ref, gq_ref, gk_ref, o_ref, ko_ref, vo_ref, kprev, vprev):
    n = pl.program_id(1)

    @pl.when(n == 0)
    def _():
        kprev[...] = jnp.zeros_like(kprev)
        vprev[...] = jnp.zeros_like(vprev)

    qs = _head_norm(q_ref[...], gq_ref[...]) * ATTN_SCALE
    kn = _head_norm(k_ref[...], gk_ref[...])
    v = v_ref[...]
    kcat = jnp.concatenate([kprev[...], kn], axis=0)
    vcat = jnp.concatenate([vprev[...], v], axis=0)
    qi = lax.broadcasted_iota(jnp.int32, (WINDOW, 2 * WINDOW), 0)
    kj = lax.broadcasted_iota(jnp.int32, (WINDOW, 2 * WINDOW), 1)
    dist = qi - kj + WINDOW
    valid = (dist >= 0) & (dist <= WINDOW) & ((n > 0) | (kj >= WINDOW))
    o = _attn_heads(qs, kcat, vcat, -dist.astype(F32), valid, sink_ref, False)
    o_ref[...] = o.astype(o_ref.dtype)
    kprev[...] = kn
    vprev[...] = v

    @pl.when(n == pl.num_programs(1) - 1)
    def _():
        ko_ref[0] = kn
        vo_ref[0] = v


def _swa_prompt(qkv, g_q, g_k, sinks, batch):
    rows = qkv.shape[0]
    nb = rows // batch // WINDOW
    kblk = Q_WIDTH // KV_WIDTH
    return pl.pallas_call(
        _swa_prompt_kernel,
        grid=(batch, nb),
        in_specs=[pl.BlockSpec(memory_space=pltpu.SMEM),
                  pl.BlockSpec((WINDOW, Q_WIDTH), lambda b, n: (b * nb + n, 0)),
                  pl.BlockSpec((WINDOW, KV_WIDTH), lambda b, n: (b * nb + n, kblk)),
                  pl.BlockSpec((WINDOW, KV_WIDTH), lambda b, n: (b * nb + n, kblk + 1)),
                  pl.BlockSpec((1, LANES), lambda b, n: (0, 0)),
                  pl.BlockSpec((1, LANES), lambda b, n: (0, 0))],
        out_specs=[pl.BlockSpec((WINDOW, Q_WIDTH), lambda b, n: (b * nb + n, 0)),
                   pl.BlockSpec((1, WINDOW, KV_WIDTH), lambda b, n: (b, 0, 0)),
                   pl.BlockSpec((1, WINDOW, KV_WIDTH), lambda b, n: (b, 0, 0))],
        out_shape=[jax.ShapeDtypeStruct((rows, Q_WIDTH), BF16),
                   jax.ShapeDtypeStruct((batch, WINDOW, KV_WIDTH), F32),
                   jax.ShapeDtypeStruct((batch, WINDOW, KV_WIDTH), F32)],
        scratch_shapes=[pltpu.VMEM((WINDOW, KV_WIDTH), F32), pltpu.VMEM((WINDOW, KV_WIDTH), F32)],
        compiler_params=_cparams(("parallel", "arbitrary")),
        name="swa_prompt",
    )(sinks, qkv, qkv, qkv, g_q, g_k)


DEC_ROWS = 8


def _swa_decode_kernel(sink_ref, qkv_ref, ck_ref, cv_ref, gq_ref, gk_ref, o_ref, kn_ref, *, t_new):
    x = qkv_ref[0]
    qs = _head_norm(x[:, :Q_WIDTH], gq_ref[...]) * ATTN_SCALE
    kn = _head_norm(x[:, Q_WIDTH:Q_WIDTH + KV_WIDTH], gk_ref[...])
    v = x[:, Q_WIDTH + KV_WIDTH:]
    fill = jnp.zeros((WINDOW - DEC_ROWS, KV_WIDTH), F32)
    kcat = jnp.concatenate([ck_ref[0], kn, fill], axis=0)
    vcat = jnp.concatenate([cv_ref[0], v, fill], axis=0)
    ti = lax.broadcasted_iota(jnp.int32, (DEC_ROWS, 2 * WINDOW), 0)
    sj = lax.broadcasted_iota(jnp.int32, (DEC_ROWS, 2 * WINDOW), 1)
    dist = ti + WINDOW - sj
    valid = (dist >= 0) & (dist <= WINDOW) & (sj < WINDOW + t_new)
    o_ref[0] = _attn_heads(qs, kcat, vcat, -dist.astype(F32), valid, sink_ref, True)
    kn_ref[0] = kn


def _swa_decode(qkv, cache_k, cache_v, g_q, g_k, sinks, t_new):
    b = qkv.shape[0]
    return pl.pallas_call(
        functools.partial(_swa_decode_kernel, t_new=t_new),
        grid=(b,),
        in_specs=[pl.BlockSpec(memory_space=pltpu.SMEM),
                  pl.BlockSpec((1, DEC_ROWS, QKV_WIDTH), lambda i: (i, 0, 0)),
                  pl.BlockSpec((1, WINDOW, KV_WIDTH), lambda i: (i, 0, 0)),
                  pl.BlockSpec((1, WINDOW, KV_WIDTH), lambda i: (i, 0, 0)),
                  pl.BlockSpec((1, LANES), lambda i: (0, 0)),
                  pl.BlockSpec((1, LANES), lambda i: (0, 0))],
        out_specs=[pl.BlockSpec((1, DEC_ROWS, Q_WIDTH), lambda i: (i, 0, 0)),
                   pl.BlockSpec((1, DEC_ROWS, KV_WIDTH), lambda i: (i, 0, 0))],
        out_shape=[jax.ShapeDtypeStruct((b, DEC_ROWS, Q_WIDTH), F32),
                   jax.ShapeDtypeStruct((b, DEC_ROWS, KV_WIDTH), F32)],
        compiler_params=_cparams(("parallel",)),
        name="swa_decode",
    )(sinks, qkv, cache_k, cache_v, g_q, g_k)


def _route(logits):
    lane = lax.broadcasted_iota(jnp.int32, logits.shape, 1).astype(F32)
    big = float(ROUTE_LANES)
    lg = jnp.where(lane < N_GROUPS, logits, NEG_INF)
    mg = jnp.max(lg, axis=-1, keepdims=True)
    gsel = jnp.min(jnp.where(lg == mg, lane, big), axis=-1, keepdims=True)
    pg_sel = 1.0 / jnp.sum(jnp.where(lane < N_GROUPS, jnp.exp(logits - mg), 0.0), axis=-1, keepdims=True)
    first = EXPERT_LANE0 + EXPERTS_PER_GROUP * gsel
    le = jnp.where((lane >= first) & (lane < first + EXPERTS_PER_GROUP), logits, NEG_INF)
    m1 = jnp.max(le, axis=-1, keepdims=True)
    i1 = jnp.min(jnp.where(le == m1, lane, big), axis=-1, keepdims=True)
    le2 = jnp.where(lane == i1, NEG_INF, le)
    m2 = jnp.max(le2, axis=-1, keepdims=True)
    i2 = jnp.min(jnp.where(le2 == m2, lane, big), axis=-1, keepdims=True)
    t = jnp.exp(m2 - m1)
    w0 = pg_sel / (1.0 + t)
    w1 = pg_sel * t / (1.0 + t)
    e0 = i1 - EXPERT_LANE0
    e1 = i2 - EXPERT_LANE0
    return jnp.where(lane == 0, e0, jnp.where(lane == 1, e1, jnp.where(lane == 2, w0, jnp.where(lane == 3, w1, 0.0))))


def _outproj_kernel(*refs, hi, n_tail):
    o_ref, w_ref, h_ref, g_ref, wr_ref, br_ref = refs[:6]
    h1_ref, route_ref, h1x_ref, acc_ref = refs[6 + n_tail:]
    i, k = pl.program_id(0), pl.program_id(1)
    n_main = pl.num_programs(0) - (1 if n_tail else 0)
    cdt, prec = _mm_dtype(hi), _mm_prec(hi)

    @pl.when(i < n_main)
    def _():
        @pl.when(k == 0)
        def _():
            acc_ref[...] = h_ref[...]

        acc_ref[...] += jnp.dot(o_ref[...].astype(cdt), w_ref[...], preferred_element_type=F32, precision=prec)

        @pl.when(k == pl.num_programs(1) - 1)
        def _():
            h1 = acc_ref[...]
            h1_ref[...] = h1
            _store_rows(h1x_ref, h1)
            xn = _rms(h1, g_ref[...]).astype(cdt)
            logits = jnp.dot(xn, wr_ref[...], preferred_element_type=F32, precision=prec) + br_ref[...]
            route_ref[...] = _route(logits)

    if n_tail:
        @pl.when((i == n_main) & (k == pl.num_programs(1) - 1))
        def _():
            for src, dst in zip(refs[6:9], (h1_ref, route_ref, h1x_ref)):
                rows = src.shape[0]
                dst[:rows] = src[...]
                dst[rows:] = jnp.zeros((dst.shape[0] - rows, dst.shape[1]), F32)


def _outproj_router(o, w, h, g_ffn, w_r, b_r, *, tm, tk, hi, tail, name):
    m, kdim = o.shape
    d = w.shape[1]
    n_main, n_k = m // tm, kdim // tk
    n_tail = 0 if tail is None else 3
    rows_i = lambda i: jnp.minimum(i, n_main - 1)
    k_i = lambda i, k: jnp.where(i < n_main, k, n_k - 1)
    in_specs = [pl.BlockSpec((tm, tk), lambda i, k: (rows_i(i), k_i(i, k))),
                pl.BlockSpec((tk, d), lambda i, k: (k_i(i, k), 0)),
                pl.BlockSpec((tm, d), lambda i, k: (rows_i(i), 0)),
                pl.BlockSpec((1, d), lambda i, k: (0, 0)),
                pl.BlockSpec((d, ROUTE_LANES), lambda i, k: (0, 0)),
                pl.BlockSpec((1, ROUTE_LANES), lambda i, k: (0, 0))]
    args = [o, w, h, g_ffn, w_r, b_r]
    total_rows = m
    if tail is not None:
        assert tail[0].shape[0] <= tm
        in_specs += [pl.BlockSpec(t.shape, lambda i, k: (0, 0)) for t in tail]
        args += list(tail)
        total_rows += tail[0].shape[0]
    return pl.pallas_call(
        functools.partial(_outproj_kernel, hi=hi, n_tail=n_tail),
        grid=(n_main + (1 if n_tail else 0), n_k),
        in_specs=in_specs,
        out_specs=[pl.BlockSpec((tm, d), lambda i, k: (i, 0)),
                   pl.BlockSpec((tm, ROUTE_LANES), lambda i, k: (i, 0)),
                   pl.BlockSpec((tm * ROW_PITCH, LANES), lambda i, k: (i, 0))],
        out_shape=[jax.ShapeDtypeStruct((total_rows, d), F32),
                   jax.ShapeDtypeStruct((total_rows, ROUTE_LANES), F32),
                   jax.ShapeDtypeStruct((total_rows * ROW_PITCH, LANES), F32)],
        scratch_shapes=[pltpu.VMEM((tm, d), F32)],
        compiler_params=_cparams(("parallel", "arbitrary")),
        name=name,
    )(*args)


def _meta_kernel(eid_ref, pos_ref, tile_ref):
    eid = eid_ref[...]
    r_i = lax.broadcasted_iota(jnp.int32, (LANES, LANES), 0)
    c_i = lax.broadcasted_iota(jnp.int32, (LANES, LANES), 1)
    upper = (r_i <= c_i).astype(BF16)
    ones = jnp.ones((LANES, LANES), BF16)
    rr = lax.broadcasted_iota(jnp.int32, (PAIR_ROWS, PAIR_ROWS), 0)
    cc = lax.broadcasted_iota(jnp.int32, (PAIR_ROWS, PAIR_ROWS), 1)
    below = (cc < rr).astype(BF16)
    lane = lax.broadcasted_iota(jnp.int32, (1, LANES), 1)

    pos = jnp.zeros((PAIR_ROWS, LANES), F32)
    hit = jnp.zeros((PAIR_ROWS, LANES), F32)
    start = jnp.zeros((1, LANES), F32)
    starts = jnp.zeros((1, LANES), F32)
    for e in range(N_EXPERTS):
        mf = (eid == e).astype(F32)
        mb = mf.astype(BF16)
        incl = jnp.dot(mb, upper, preferred_element_type=F32)
        row_tot = jnp.dot(mb, ones, preferred_element_type=F32)
        row_off = jnp.dot(below, row_tot.astype(BF16), preferred_element_type=F32)
        rank = incl - mf + row_off
        cnt = row_off[PAIR_ROWS - 1:PAIR_ROWS, :] + row_tot[PAIR_ROWS - 1:PAIR_ROWS, :]
        padded = jnp.floor((cnt + (MOE_TILE - 1)) * (1.0 / MOE_TILE)) * MOE_TILE
        pos = pos + mf * (start + rank)
        hit = hit + mf
        start = start + padded
        starts = jnp.where(lane > e, start, starts)
    pos_ref[...] = jnp.where(hit > 0, pos, -1.0).astype(jnp.int32)
    tile_ref[...] = jnp.broadcast_to(starts * (1.0 / MOE_TILE), (8, LANES)).astype(jnp.int32)


def _moe_meta(eid_pairs):
    return pl.pallas_call(
        _meta_kernel,
        out_shape=[jax.ShapeDtypeStruct((PAIR_ROWS, LANES), jnp.int32),
                   jax.ShapeDtypeStruct((8, LANES), jnp.int32)],
        compiler_params=pltpu.CompilerParams(vmem_limit_bytes=VMEM_LIMIT),
        name="moe_meta",
    )(eid_pairs)


INV_CHUNK = 2048


def _invert_kernel(pos_ref, inv_ref, *, n_tokens):
    n_pairs = pos_ref.shape[1]
    hi_acc = jnp.zeros((MAX_TILES, MOE_TILE), F32)
    lo_acc = jnp.zeros((MAX_TILES, MOE_TILE), F32)
    hit_acc = jnp.zeros((MAX_TILES, MOE_TILE), F32)
    t_iota = lax.broadcasted_iota(jnp.int32, (MAX_TILES, INV_CHUNK), 0)
    r_iota = lax.broadcasted_iota(jnp.int32, (MOE_TILE, INV_CHUNK), 0)
    for c in range(n_pairs // INV_CHUNK):
        pos = pos_ref[:, c * INV_CHUNK:(c + 1) * INV_CHUNK]
        p = lax.broadcasted_iota(jnp.int32, (1, INV_CHUNK), 1) + c * INV_CHUNK
        tok = jnp.where(p >= n_tokens, p - n_tokens, p)
        in_tile = (pos >> (MOE_TILE.bit_length() - 1)) == t_iota
        a_hi = jnp.where(in_tile, (tok >> 7).astype(F32), 0.0).astype(BF16)
        a_lo = jnp.where(in_tile, (tok & 127).astype(F32), 0.0).astype(BF16)
        b = ((pos & (MOE_TILE - 1)) == r_iota).astype(BF16)
        dn = (((1,), (1,)), ((), ()))
        hi_acc = hi_acc + lax.dot_general(a_hi, b, dn, preferred_element_type=F32)
        lo_acc = lo_acc + lax.dot_general(a_lo, b, dn, preferred_element_type=F32)
        hit_acc = hit_acc + lax.dot_general(in_tile.astype(F32).astype(BF16), b, dn, preferred_element_type=F32)
    slot = (lax.broadcasted_iota(jnp.int32, (MAX_TILES, MOE_TILE), 0) * MOE_TILE
            + lax.broadcasted_iota(jnp.int32, (MAX_TILES, MOE_TILE), 1))
    spread = slot & ((1 << (n_tokens.bit_length() - 1)) - 1)
    inv_ref[...] = jnp.where(hit_acc > 0, (hi_acc * 128.0 + lo_acc).astype(jnp.int32), spread)


def _moe_invert(pos_row, n_tokens):
    return pl.pallas_call(
        functools.partial(_invert_kernel, n_tokens=n_tokens),
        out_shape=jax.ShapeDtypeStruct((MAX_TILES, MOE_TILE), jnp.int32),
        compiler_params=pltpu.CompilerParams(vmem_limit_bytes=VMEM_LIMIT),
        name="moe_invert",
    )(pos_row)


TILE_ROWS = MOE_TILE * ROW_PITCH


def _experts_kernel(starts_ref, inv_hbm, hx_hbm, g_ref, wg_ref, wu_ref, wd_ref, y_hbm,
                    xbuf, ybuf, inv_sm, gsem, ysem, isem, wg_bf, wu_bf, wd_bf, *, n_slots):
    e = pl.program_id(0)
    t0, t1 = starts_ref[e], starts_ref[e + 1]
    n_tiles = starts_ref[N_EXPERTS]

    def inv_copy(t, s):
        return pltpu.make_async_copy(inv_hbm.at[t], inv_sm.at[s], isem.at[s])

    def y_copy(t, s):
        return pltpu.make_async_copy(ybuf.at[s], y_hbm.at[pl.ds(t * TILE_ROWS, TILE_ROWS)], ysem.at[s])

    def issue_gather(s):
        def body(r, carry):
            _row_copy(hx_hbm, inv_sm[s, 0, r], xbuf.at[s], r, gsem.at[s]).start()
            return carry
        lax.fori_loop(0, MOE_TILE, body, 0, unroll=8)

    @pl.when((e == 0) & (n_tiles > 0))
    def _():
        inv_copy(0, 0).start()
        inv_copy(0, 0).wait()
        issue_gather(0)

        @pl.when(n_tiles > 1)
        def _():
            inv_copy(1, 1).start()

    @pl.when(t1 > t0)
    def _():
        wg_bf[...] = wg_ref[...].astype(BF16)
        wu_bf[...] = wu_ref[...].astype(BF16)
        wd_bf[...] = wd_ref[...].astype(BF16)

        def tile(t, carry):
            s = t % 2

            @pl.when(t + 1 < n_tiles)
            def _():
                inv_copy(t + 1, 1 - s).wait()
                issue_gather(1 - s)

            @pl.when(t + 2 < n_tiles)
            def _():
                inv_copy(t + 2, s).start()

            _rows_wait(hx_hbm, xbuf.at[s], MOE_TILE, gsem.at[s])

            @pl.when(t >= 2)
            def _():
                y_copy(t - 2, s).wait()

            xn = _rms(_load_rows(xbuf.at[s], MOE_TILE), g_ref[...]).astype(BF16)
            hg = jnp.dot(xn, wg_bf[...], preferred_element_type=F32)
            hu = jnp.dot(xn, wu_bf[...], preferred_element_type=F32)
            hid = (hg / (1.0 + jnp.exp(-hg)) * hu).astype(BF16)
            _store_rows(ybuf.at[s], jnp.dot(hid, wd_bf[...], preferred_element_type=F32))
            y_copy(t, s).start()
            return carry

        lax.fori_loop(t0, t1, tile, 0)

    @pl.when(e == pl.num_programs(0) - 1)
    def _():
        for back in (1, 2):
            @pl.when(n_tiles >= back)
            def _():
                y_copy(n_tiles - back, (n_tiles - back) % 2).wait()

        ybuf[0] = jnp.zeros((TILE_ROWS, LANES), F32)

        def fill(t, carry):
            y_copy(t, 0).start()
            y_copy(t, 0).wait()
            return carry

        lax.fori_loop(n_tiles, n_slots, fill, 0)


def _moe_experts(tile_tab, inv, hx_all, g_ffn, w_gate, w_up, w_down, layer, n_steps):
    d = D_MODEL

    def widx(e, tab):
        return (layer, e, 0, 0)

    grid_spec = pltpu.PrefetchScalarGridSpec(
        num_scalar_prefetch=1,
        grid=(N_EXPERTS,),
        in_specs=[pl.BlockSpec(memory_space=pl.ANY),
                  pl.BlockSpec(memory_space=pl.ANY),
                  pl.BlockSpec((1, d), lambda e, tab: (0, 0)),
                  pl.BlockSpec((None, None, d, D_EXPERT), widx),
                  pl.BlockSpec((None, None, d, D_EXPERT), widx),
                  pl.BlockSpec((None, None, D_EXPERT, d), widx)],
        out_specs=pl.BlockSpec(memory_space=pl.ANY),
        scratch_shapes=[pltpu.VMEM((2, TILE_ROWS, LANES), F32),
                        pltpu.VMEM((2, TILE_ROWS, LANES), F32),
                        pltpu.SMEM((2, 1, MOE_TILE), jnp.int32),
                        pltpu.SemaphoreType.DMA((2,)),
                        pltpu.SemaphoreType.DMA((2,)),
                        pltpu.SemaphoreType.DMA((2,)),
                        pltpu.VMEM((d, D_EXPERT), BF16),
                        pltpu.VMEM((d, D_EXPERT), BF16),
                        pltpu.VMEM((D_EXPERT, d), BF16)],
    )
    return pl.pallas_call(
        functools.partial(_experts_kernel, n_slots=n_steps),
        grid_spec=grid_spec,
        out_shape=jax.ShapeDtypeStruct((n_steps * TILE_ROWS, LANES), F32),
        compiler_params=_cparams(("arbitrary",)),
        name="moe_experts",
    )(tile_tab, inv, hx_all, g_ffn, w_gate, w_up, w_down)


def _combine_kernel(p0_cur, p0_nxt, p1_cur, p1_nxt, y_hbm, h_ref, route_ref, p_ref, g_ref, wg_ref, wp_ref,
                    o_ref, ybuf, sems, *, tm):
    i = pl.program_id(0)
    slot = i % 2

    def issue(p0, p1, s):
        def body(r, carry):
            _row_copy(y_hbm, p0[0, 0, r], ybuf.at[s, 0], r, sems.at[s]).start()
            _row_copy(y_hbm, p1[0, 0, r], ybuf.at[s, 1], r, sems.at[s]).start()
            return carry
        lax.fori_loop(0, tm, body, 0, unroll=8)

    @pl.when(i == 0)
    def _():
        issue(p0_cur, p1_cur, 0)

    @pl.when(i + 1 < pl.num_programs(0))
    def _():
        issue(p0_nxt, p1_nxt, 1 - slot)

    for c in range(2):
        _rows_wait(y_hbm, ybuf.at[slot, c], tm, sems.at[slot])
    route = route_ref[...]
    y0, y1 = _load_rows(ybuf.at[slot, 0], tm), _load_rows(ybuf.at[slot, 1], tm)
    h2 = h_ref[...] + (route[:, 2:3] * y0 + route[:, 3:4] * y1)
    xn = _rms(h2, g_ref[...]).astype(BF16)
    z = jnp.dot(xn, wg_ref[...], preferred_element_type=F32)
    gate = 1.0 / (1.0 + jnp.exp(-z))
    proj = jnp.dot(p_ref[...].astype(BF16), wp_ref[...], preferred_element_type=F32)
    o_ref[...] = h2 + proj * gate


def _moe_combine_ple(pos0, pos1, y_sorted, h_all, route_all, p, g_ple, w_gate, w_proj, *, layer, tm, rows,
                     row_block0, name):
    d = h_all.shape[1]
    n = rows // tm
    last = n - 1
    smem = functools.partial(pl.BlockSpec, (1, 1, tm), memory_space=pltpu.SMEM)
    return pl.pallas_call(
        functools.partial(_combine_kernel, tm=tm),
        grid=(n,),
        in_specs=[smem(lambda i: (i, 0, 0)),
                  smem(lambda i: (jnp.minimum(i + 1, last), 0, 0)),
                  smem(lambda i: (i, 0, 0)),
                  smem(lambda i: (jnp.minimum(i + 1, last), 0, 0)),
                  pl.BlockSpec(memory_space=pl.ANY),
                  pl.BlockSpec((tm, d), lambda i: (row_block0 + i, 0)),
                  pl.BlockSpec((tm, ROUTE_LANES), lambda i: (row_block0 + i, 0)),
                  pl.BlockSpec((None, tm, PLE_DIM), lambda i: (layer, i, 0)),
                  pl.BlockSpec((1, d), lambda i: (0, 0)),
                  pl.BlockSpec((d, d), lambda i: (0, 0)),
                  pl.BlockSpec((PLE_DIM, d), lambda i: (0, 0))],
        out_specs=pl.BlockSpec((tm, d), lambda i: (i, 0)),
        out_shape=jax.ShapeDtypeStruct((rows, d), F32),
        scratch_shapes=[pltpu.VMEM((2, 2, tm * ROW_PITCH, LANES), F32), pltpu.SemaphoreType.DMA((2,))],
        compiler_params=_cparams(("arbitrary",)),
        name=name,
    )(pos0, pos0, pos1, pos1, y_sorted, h_all, route_all, p, g_ple, w_gate, w_proj)


def _ret_head(q, k, v, gt, state, gn, lg, c_real):
    c = q.shape[0]
    k = k * (RET_DK ** -0.5)
    ri = lax.broadcasted_iota(jnp.int32, (c, c), 0)
    ci = lax.broadcasted_iota(jnp.int32, (c, c), 1)
    diff = (ri - ci).astype(F32)
    decay = jnp.where(diff >= 0, jnp.exp(lg * jnp.maximum(diff, 0.0)), 0.0)
    idx = lax.broadcasted_iota(jnp.int32, (c, 1), 0).astype(F32)
    qk = lax.dot_general(q.astype(BF16), k.astype(BF16), (((1,), (1,)), ((), ())), preferred_element_type=F32)
    inner = qk * decay
    q_dec = q * jnp.exp(lg * (idx + 1.0))
    k_dec = k * jnp.exp(lg * (c_real - 1.0 - idx))
    vb = v.astype(BF16)
    out = (jnp.dot(inner.astype(BF16), vb, preferred_element_type=F32)
           + jnp.dot(q_dec.astype(BF16), state.astype(BF16), preferred_element_type=F32))
    new_state = math.exp(lg * c_real) * state + lax.dot_general(
        k_dec.astype(BF16), vb, (((0,), (0,)), ((), ())), preferred_element_type=F32)
    mu = jnp.mean(out, axis=-1, keepdims=True)
    cen = out - mu
    var = jnp.mean(cen * cen, axis=-1, keepdims=True)
    on = cen * lax.rsqrt(var + GN_EPS) * gn
    return gt / (1.0 + jnp.exp(-gt)) * on, new_state


def _ret_prompt_kernel(q_ref, k_ref, v_ref, g_ref, gn_ref, y_ref, so_ref, st_ref):
    c = pl.program_id(1)

    @pl.when(c == 0)
    def _():
        st_ref[...] = jnp.zeros_like(st_ref)

    for h in range(RET_HEADS):
        ks, vs = slice(h * RET_DK, (h + 1) * RET_DK), slice(h * RET_DV, (h + 1) * RET_DV)
        y, new_state = _ret_head(q_ref[:, ks], k_ref[:, ks], v_ref[:, vs], g_ref[:, vs], st_ref[h],
                                 gn_ref[:, vs], RET_LOG_GAMMA[h], RET_CHUNK)
        st_ref[h] = new_state
        y_ref[:, vs] = y.astype(y_ref.dtype)

    @pl.when(c == pl.num_programs(1) - 1)
    def _():
        so_ref[0] = st_ref[...]


def _ret_prompt(qkvg, g_norm, batch):
    rows = qkvg.shape[0]
    nc = rows // batch // RET_CHUNK
    return pl.pallas_call(
        _ret_prompt_kernel,
        grid=(batch, nc),
        in_specs=[pl.BlockSpec((RET_CHUNK, RET_K_WIDTH), lambda b, c: (b * nc + c, 0)),
                  pl.BlockSpec((RET_CHUNK, RET_K_WIDTH), lambda b, c: (b * nc + c, 1)),
                  pl.BlockSpec((RET_CHUNK, RET_V_WIDTH), lambda b, c: (b * nc + c, 1)),
                  pl.BlockSpec((RET_CHUNK, RET_V_WIDTH), lambda b, c: (b * nc + c, 2)),
                  pl.BlockSpec((1, RET_V_WIDTH), lambda b, c: (0, 0))],
        out_specs=[pl.BlockSpec((RET_CHUNK, RET_V_WIDTH), lambda b, c: (b * nc + c, 0)),
                   pl.BlockSpec((1, RET_HEADS, RET_DK, RET_DV), lambda b, c: (b, 0, 0, 0))],
        out_shape=[jax.ShapeDtypeStruct((rows, RET_V_WIDTH), BF16),
                   jax.ShapeDtypeStruct((batch, RET_HEADS, RET_DK, RET_DV), F32)],
        scratch_shapes=[pltpu.VMEM((RET_HEADS, RET_DK, RET_DV), F32)],
        compiler_params=_cparams(("parallel", "arbitrary")),
        name="ret_prompt",
    )(qkvg, qkvg, qkvg, qkvg, g_norm)


def _ret_decode_kernel(x_ref, s_ref, gn_ref, y_ref, so_ref, *, t_new):
    x = x_ref[0]
    for h in range(RET_HEADS):
        q = x[:, h * RET_DK:(h + 1) * RET_DK]
        k = x[:, RET_K_WIDTH + h * RET_DK:RET_K_WIDTH + (h + 1) * RET_DK]
        v0 = 2 * RET_K_WIDTH + h * RET_DV
        g0 = 2 * RET_K_WIDTH + RET_V_WIDTH + h * RET_DV
        y, new_state = _ret_head(q, k, x[:, v0:v0 + RET_DV], x[:, g0:g0 + RET_DV], s_ref[0, h],
                                 gn_ref[:, h * RET_DV:(h + 1) * RET_DV], RET_LOG_GAMMA[h], t_new)
        so_ref[0, h] = new_state
        y_ref[0, :, h * RET_DV:(h + 1) * RET_DV] = y


def _ret_decode(qkvg, state, g_norm, t_new):
    b = qkvg.shape[0]
    return pl.pallas_call(
        functools.partial(_ret_decode_kernel, t_new=t_new),
        grid=(b,),
        in_specs=[pl.BlockSpec((1, DEC_ROWS, RET_IN_WIDTH), lambda i: (i, 0, 0)),
                  pl.BlockSpec((1, RET_HEADS, RET_DK, RET_DV), lambda i: (i, 0, 0, 0)),
                  pl.BlockSpec((1, RET_V_WIDTH), lambda i: (0, 0))],
        out_specs=[pl.BlockSpec((1, DEC_ROWS, RET_V_WIDTH), lambda i: (i, 0, 0)),
                   pl.BlockSpec((1, RET_HEADS, RET_DK, RET_DV), lambda i: (i, 0, 0, 0))],
        out_shape=[jax.ShapeDtypeStruct((b, DEC_ROWS, RET_V_WIDTH), F32),
                   jax.ShapeDtypeStruct((b, RET_HEADS, RET_DK, RET_DV), F32)],
        compiler_params=_cparams(("parallel",)),
        name="ret_decode",
    )(qkvg, state, g_norm)


def _pad_rows(x, b, t):
    return jnp.pad(x.reshape(b, t, x.shape[-1]), ((0, 0), (0, DEC_ROWS - t), (0, 0)))


def _moe_layer(h_all, route_all, hx_all, p_prompt, p_sample, g_ffn, g_ple, w_gate_e, w_up_e, w_down_e, w_ple_gate,
               w_ple_proj, layer, n_prompt, n_sample):
    n_tok = n_prompt + n_sample
    n_pairs = PAIR_ROWS * LANES
    eid = route_all[:, :2].astype(jnp.int32).T.reshape(-1)
    eid = jnp.pad(eid, (0, n_pairs - 2 * n_tok), constant_values=-1).reshape(PAIR_ROWS, LANES)
    pos, tile_tab = _moe_meta(eid)
    inv = _moe_invert(pos.reshape(1, n_pairs), n_tok).reshape(MAX_TILES, 1, MOE_TILE)
    max_tiles = (2 * n_tok + N_EXPERTS * (MOE_TILE - 1)) // MOE_TILE
    y_sorted = _moe_experts(tile_tab[0], inv, hx_all, g_ffn, w_gate_e, w_up_e, w_down_e, layer, max_tiles)
    pos2 = pos.reshape(-1)[:2 * n_tok].reshape(2, n_tok)
    tm_p, tm_s = 256, n_sample
    outs = []
    for lo, rows, tm, p, nm in ((0, n_prompt, tm_p, p_prompt, "combine_prompt"),
                                (n_prompt, n_sample, tm_s, p_sample, "combine_sample")):
        pos0 = pos2[0, lo:lo + rows].reshape(rows // tm, 1, tm)
        pos1 = pos2[1, lo:lo + rows].reshape(rows // tm, 1, tm)
        outs.append(_moe_combine_ple(pos0, pos1, y_sorted, h_all, route_all, p, g_ple, w_ple_gate, w_ple_proj,
                                     layer=layer, tm=tm, rows=rows, row_block0=lo // tm, name=nm))
    return outs


def kernel(x_prompt, x_sample, cache_k_swa, cache_v_swa, state_ret, p_prompt, p_sample, g_mix, g_ffn, g_ple,
           swa_w_qkv, swa_g_q, swa_g_k, swa_sinks, swa_w_o, ret_w_in, ret_g_norm, ret_w_o, moe_w_group,
           moe_b_group, moe_w_expert, moe_b_expert, moe_w_gate, moe_w_up, moe_w_down, ple_w_proj, ple_w_gate):
    batch, seq, d = x_prompt.shape
    dec_b, dec_t, _ = x_sample.shape
    n_prompt, n_sample = batch * seq, dec_b * dec_t
    n_tok = n_prompt + n_sample
    assert 2 * n_tok <= PAIR_ROWS * LANES and 2 * n_tok + N_EXPERTS * (MOE_TILE - 1) <= (MAX_TILES - 1) * MOE_TILE
    assert dec_t <= DEC_ROWS and n_prompt % 1024 == 0 and n_sample % 8 == 0

    xp = x_prompt.reshape(n_prompt, d)
    xs = x_sample.reshape(n_sample, d)
    pp = p_prompt.reshape(DEPTH, n_prompt, PLE_DIM)
    ps = p_sample.reshape(DEPTH, n_sample, PLE_DIM)
    row = lambda a: a.reshape(1, -1)

    def router_w(i):
        w = jnp.concatenate([moe_w_group[i], moe_w_expert[i]], axis=1)
        b = jnp.concatenate([moe_b_group[i], moe_b_expert[i]])
        pad = ROUTE_LANES - w.shape[1]
        return jnp.pad(w, ((0, 0), (0, pad))), jnp.pad(b, (0, pad)).reshape(1, ROUTE_LANES)

    g_q2, g_k2 = row(jnp.tile(swa_g_q[0], 2)), row(jnp.tile(swa_g_k[0], 2))
    w_r, b_r = router_w(0)
    qkv_p = _norm_proj(xp, row(g_mix[0]), swa_w_qkv[0].astype(BF16), tm=1024, tn=512, hi=False, name="qkv_prompt")
    o_p, k_p, v_p = _swa_prompt(qkv_p, g_q2, g_k2, swa_sinks[0], batch)

    qkv_s = _norm_proj(xs, row(g_mix[0]), swa_w_qkv[0], tm=n_sample, tn=512, hi=True, name="qkv_sample")
    ck = cache_k_swa[0].reshape(dec_b, WINDOW, KV_WIDTH)
    cv = cache_v_swa[0].reshape(dec_b, WINDOW, KV_WIDTH)
    o_s, kn_s = _swa_decode(_pad_rows(qkv_s, dec_b, dec_t), ck, cv, g_q2, g_k2, swa_sinks[0], dec_t)
    o_s = o_s[:, :dec_t].reshape(n_sample, Q_WIDTH)
    tail = _outproj_router(o_s, swa_w_o[0], xs, row(g_ffn[0]), w_r, b_r, tm=n_sample, tk=1024, hi=True,
                           tail=None, name="swa_out_sample")
    h_all, route_all, hx_all = _outproj_router(o_p, swa_w_o[0].astype(BF16), xp, row(g_ffn[0]), w_r.astype(BF16),
                                               b_r, tm=512, tk=1024, hi=False, tail=tail, name="swa_out_prompt")
    k_s = jnp.concatenate([ck[:, dec_t:], kn_s[:, :dec_t]], axis=1)
    v_new = qkv_s[:, Q_WIDTH + KV_WIDTH:].reshape(dec_b, dec_t, KV_WIDTH)
    v_s = jnp.concatenate([cv[:, dec_t:], v_new], axis=1)

    h_p, h_s = _moe_layer(h_all, route_all, hx_all, pp, ps, row(g_ffn[0]), row(g_ple[0]), moe_w_gate, moe_w_up,
                          moe_w_down, ple_w_gate[0].astype(BF16), ple_w_proj[0].astype(BF16), 0, n_prompt, n_sample)

    w_in = ret_w_in[0].astype(BF16)
    w_o = ret_w_o[0].astype(BF16)
    w_r, b_r = router_w(1)
    w_r = w_r.astype(BF16)
    qkvg_p = _norm_proj(h_p, row(g_mix[1]), w_in, tm=1024, tn=512, hi=False, name="ret_in_prompt")
    y_p, s_p = _ret_prompt(qkvg_p, row(ret_g_norm[0]), batch)

    qkvg_s = _norm_proj(h_s, row(g_mix[1]), w_in, tm=n_sample, tn=512, hi=False, name="ret_in_sample")
    y_s, s_s = _ret_decode(_pad_rows(qkvg_s, dec_b, dec_t), state_ret[0], row(ret_g_norm[0]), dec_t)
    y_s = y_s[:, :dec_t].reshape(n_sample, RET_V_WIDTH)
    tail = _outproj_router(y_s, w_o, h_s, row(g_ffn[1]), w_r, b_r, tm=n_sample, tk=1024, hi=False, tail=None,
                           name="ret_out_sample")
    h_all, route_all, hx_all = _outproj_router(y_p, w_o, h_p, row(g_ffn[1]), w_r, b_r, tm=512, tk=1024, hi=False,
                                               tail=tail, name="ret_out_prompt")

    y_prompt, y_sample = _moe_layer(h_all, route_all, hx_all, pp, ps, row(g_ffn[1]), row(g_ple[1]), moe_w_gate,
                                    moe_w_up, moe_w_down, ple_w_gate[1].astype(BF16), ple_w_proj[1].astype(BF16), 1,
                                    n_prompt, n_sample)

    kv_shape = (1, -1, WINDOW, SWA_KV_HEADS, SWA_HEAD_DIM)
    return (y_prompt.reshape(batch, seq, d), y_sample.reshape(dec_b, dec_t, d),
            k_p.reshape(kv_shape), v_p.reshape(kv_shape), s_p[None],
            k_s.reshape(kv_shape), v_s.reshape(kv_shape), s_s[None])
```

```python
import functools
import math

import jax
import jax.numpy as jnp
from jax import lax
from jax.experimental import pallas as pl
from jax.experimental.pallas import tpu as pltpu

D_MODEL = 2048
DEPTH = 2
SWA_HEADS = 32
SWA_KV_HEADS = 8
SWA_HEAD_DIM = 64
Q_WIDTH = SWA_HEADS * SWA_HEAD_DIM
KV_WIDTH = SWA_KV_HEADS * SWA_HEAD_DIM
QKV_WIDTH = Q_WIDTH + 2 * KV_WIDTH
WINDOW = 128
ATTN_SCALE = SWA_HEAD_DIM ** -0.5
RET_HEADS = 8
RET_DK = D_MODEL // RET_HEADS
RET_DV = 2 * D_MODEL // RET_HEADS
RET_K_WIDTH = RET_HEADS * RET_DK
RET_V_WIDTH = RET_HEADS * RET_DV
RET_IN_WIDTH = 2 * RET_K_WIDTH + 2 * RET_V_WIDTH
RET_CHUNK = 128
N_GROUPS = 4
EXPERTS_PER_GROUP = 8
N_EXPERTS = N_GROUPS * EXPERTS_PER_GROUP
D_EXPERT = D_MODEL // 4
PLE_DIM = 256
NORM_EPS = 1e-6
GN_EPS = 1e-5
NEG_INF = -1e30

LANES = 128
ROUTE_LANES = LANES
EXPERT_LANE0 = N_GROUPS
MOE_TILE = 256
PAIR_ROWS = 256
MAX_TILES = LANES
VMEM_LIMIT = 56 * 1024 * 1024
ROW_TILES = D_MODEL // LANES
ROW_PITCH = ROW_TILES + 4

F32 = jnp.float32
BF16 = jnp.bfloat16
HIGHEST = lax.Precision.HIGHEST

ALIBI_SLOPES = tuple(2.0 ** (-8.0 * (h + 1) / SWA_HEADS) for h in range(SWA_HEADS))
RET_LOG_GAMMA = tuple(math.log(1.0 - 2.0 ** (-5.0 - h)) for h in range(RET_HEADS))


def _cparams(sem):
    return pltpu.CompilerParams(dimension_semantics=sem, vmem_limit_bytes=VMEM_LIMIT)


def _mm_dtype(hi):
    return F32 if hi else BF16


def _mm_prec(hi):
    return HIGHEST if hi else None


def _rms(x, g):
    return x * lax.rsqrt(jnp.mean(x * x, axis=-1, keepdims=True) + NORM_EPS) * g


def _store_rows(ref, x):
    n = x.shape[0]
    for s in range(ROW_PITCH):
        piece = x[:, s * LANES:(s + 1) * LANES] if s < ROW_TILES else jnp.zeros((n, LANES), F32)
        ref[pl.ds(s, n, stride=ROW_PITCH), :] = piece


def _load_rows(ref, n):
    return jnp.concatenate([ref[pl.ds(s, n, stride=ROW_PITCH), :] for s in range(ROW_TILES)], axis=1)


def _row_copy(src_hbm, src_row, dst, dst_row, sem):
    return pltpu.make_async_copy(src_hbm.at[pl.ds(src_row * ROW_PITCH, ROW_TILES)],
                                 dst.at[pl.ds(dst_row * ROW_PITCH, ROW_TILES)], sem)


def _rows_wait(src_hbm, dst, n, sem):
    pltpu.make_async_copy(src_hbm.at[pl.ds(0, n * ROW_TILES)], dst.at[pl.ds(0, n * ROW_TILES)], sem).wait()


def _proj_kernel(x_ref, g_ref, w_ref, o_ref, a_ref, *, hi):
    @pl.when(pl.program_id(1) == 0)
    def _():
        a_ref[...] = _rms(x_ref[...], g_ref[...]).astype(a_ref.dtype)

    o_ref[...] = jnp.dot(a_ref[...], w_ref[...], preferred_element_type=F32,
                         precision=_mm_prec(hi)).astype(o_ref.dtype)


def _norm_proj(x, g, w, *, tm, tn, hi, name):
    m, k = x.shape
    n = w.shape[1]
    return pl.pallas_call(
        functools.partial(_proj_kernel, hi=hi),
        grid=(m // tm, n // tn),
        in_specs=[pl.BlockSpec((tm, k), lambda i, j: (i, 0)),
                  pl.BlockSpec((1, k), lambda i, j: (0, 0)),
                  pl.BlockSpec((k, tn), lambda i, j: (0, j))],
        out_specs=pl.BlockSpec((tm, tn), lambda i, j: (i, j)),
        out_shape=jax.ShapeDtypeStruct((m, n), F32),
        scratch_shapes=[pltpu.VMEM((tm, k), _mm_dtype(hi))],
        compiler_params=_cparams(("parallel", "arbitrary")),
        name=name,
    )(x, g, w)


def _head_norm(x, g2):
    lane = lax.broadcasted_iota(jnp.int32, (1, LANES), 1)
    lo = lane < SWA_HEAD_DIM
    outs = []
    for j in range(x.shape[1] // LANES):
        xb = x[:, j * LANES:(j + 1) * LANES]
        x2 = xb * xb
        s_lo = jnp.sum(jnp.where(lo, x2, 0.0), axis=-1, keepdims=True)
        s_hi = jnp.sum(jnp.where(lo, 0.0, x2), axis=-1, keepdims=True)
        r = jnp.where(lo, lax.rsqrt(s_lo * (1.0 / SWA_HEAD_DIM) + NORM_EPS),
                      lax.rsqrt(s_hi * (1.0 / SWA_HEAD_DIM) + NORM_EPS))
        outs.append(xb * r * g2)
    return jnp.concatenate(outs, axis=1)


def _attn_heads(qs, kn, v, negdist, valid, sink_ref, hi):
    nq, nk = qs.shape[0], kn.shape[0]
    cdt, prec = _mm_dtype(hi), _mm_prec(hi)
    lane = lax.broadcasted_iota(jnp.int32, (1, LANES), 1)
    lo = lane < SWA_HEAD_DIM
    group = SWA_HEADS // SWA_KV_HEADS
    outs = []
    for m in range(KV_WIDTH // LANES):
        kb = kn[:, m * LANES:(m + 1) * LANES]
        vb = v[:, m * LANES:(m + 1) * LANES]
        kr = pltpu.roll(kb, SWA_HEAD_DIM, 1)
        vr = pltpu.roll(vb, SWA_HEAD_DIM, 1)
        for half in range(2):
            kv = 2 * m + half
            k_lo, k_hi = (kb, kr) if half == 0 else (kr, kb)
            v_lo, v_hi = (vb, vr) if half == 0 else (vr, vb)
            kbd = jnp.concatenate([jnp.where(lo, k_lo, 0.0), jnp.where(lo, 0.0, k_hi)], axis=0).astype(cdt)
            vbd = jnp.concatenate([jnp.where(lo, v_lo, 0.0), jnp.where(lo, 0.0, v_hi)], axis=0).astype(cdt)
            c0 = kv * group * SWA_HEAD_DIM
            q2 = jnp.concatenate([qs[:, c0:c0 + LANES], qs[:, c0 + LANES:c0 + 2 * LANES]], axis=0).astype(cdt)
            s_all = lax.dot_general(q2, kbd, (((1,), (1,)), ((), ())), preferred_element_type=F32, precision=prec)
            e_rows, inv = [], []
            for r in range(2):
                e_cols = []
                for c in range(2):
                    h = kv * group + 2 * r + c
                    s = s_all[r * nq:(r + 1) * nq, c * nk:(c + 1) * nk]
                    s = jnp.where(valid, s + ALIBI_SLOPES[h] * negdist, NEG_INF)
                    sk = sink_ref[h]
                    mx = jnp.maximum(jnp.max(s, axis=-1, keepdims=True), sk)
                    e = jnp.exp(s - mx)
                    den = jnp.sum(e, axis=-1, keepdims=True) + jnp.exp(sk - mx)
                    e_cols.append(e.astype(cdt))
                    inv.append(1.0 / den)
                e_rows.append(jnp.concatenate(e_cols, axis=1))
            p2 = jnp.concatenate(e_rows, axis=0)
            o2 = jnp.dot(p2, vbd, preferred_element_type=F32, precision=prec)
            for r in range(2):
                outs.append(o2[r * nq:(r + 1) * nq] * jnp.where(lo, inv[2 * r], inv[2 * r + 1]))
    return jnp.concatenate(outs, axis=1)


def _swa_prompt_kernel(sink_ref, q_ref, k_ref, v_ref, gq_ref, gk_ref, o_ref, ko_ref, vo_ref, kprev, vprev):
    n = pl.program_id(1)

    @pl.when(n == 0)
    def _():
        kprev[...] = jnp.zeros_like(kprev)
        vprev[...] = jnp.zeros_like(vprev)

    qs = _head_norm(q_ref[...], gq_ref[...]) * ATTN_SCALE
    kn = _head_norm(k_ref[...], gk_ref[...])
    v = v_ref[...]
    kcat = jnp.concatenate([kprev[...], kn], axis=0)
    vcat = jnp.concatenate([vprev[...], v], axis=0)
    qi = lax.broadcasted_iota(jnp.int32, (WINDOW, 2 * WINDOW), 0)
    kj = lax.broadcasted_iota(jnp.int32, (WINDOW, 2 * WINDOW), 1)
    dist = qi - kj + WINDOW
    valid = (dist >= 0) & (dist <= WINDOW) & ((n > 0) | (kj >= WINDOW))
    o = _attn_heads(qs, kcat, vcat, -dist.astype(F32), valid, sink_ref, False)
    o_ref[...] = o.astype(o_ref.dtype)
    kprev[...] = kn
    vprev[...] = v

    @pl.when(n == pl.num_programs(1) - 1)
    def _():
        ko_ref[0] = kn
        vo_ref[0] = v


def _swa_prompt(qkv, g_q, g_k, sinks, batch):
    rows = qkv.shape[0]
    nb = rows // batch // WINDOW
    kblk = Q_WIDTH // KV_WIDTH
    return pl.pallas_call(
        _swa_prompt_kernel,
        grid=(batch, nb),
        in_specs=[pl.BlockSpec(memory_space=pltpu.SMEM),
                  pl.BlockSpec((WINDOW, Q_WIDTH), lambda b, n: (b * nb + n, 0)),
                  pl.BlockSpec((WINDOW, KV_WIDTH), lambda b, n: (b * nb + n, kblk)),
                  pl.BlockSpec((WINDOW, KV_WIDTH), lambda b, n: (b * nb + n, kblk + 1)),
                  pl.BlockSpec((1, LANES), lambda b, n: (0, 0)),
                  pl.BlockSpec((1, LANES), lambda b, n: (0, 0))],
        out_specs=[pl.BlockSpec((WINDOW, Q_WIDTH), lambda b, n: (b * nb + n, 0)),
                   pl.BlockSpec((1, WINDOW, KV_WIDTH), lambda b, n: (b, 0, 0)),
                   pl.BlockSpec((1, WINDOW, KV_WIDTH), lambda b, n: (b, 0, 0))],
        out_shape=[jax.ShapeDtypeStruct((rows, Q_WIDTH), BF16),
                   jax.ShapeDtypeStruct((batch, WINDOW, KV_WIDTH), F32),
                   jax.ShapeDtypeStruct((batch, WINDOW, KV_WIDTH), F32)],
        scratch_shapes=[pltpu.VMEM((WINDOW, KV_WIDTH), F32), pltpu.VMEM((WINDOW, KV_WIDTH), F32)],
        compiler_params=_cparams(("parallel", "arbitrary")),
        name="swa_prompt",
    )(sinks, qkv, qkv, qkv, g_q, g_k)


DEC_ROWS = 8
DEC_GROUP = 4


def _swa_decode_kernel(sink_ref, qkv_ref, ck_ref, cv_ref, gq_ref, gk_ref, o_ref, kn_ref, *, t_new):
    ti = lax.broadcasted_iota(jnp.int32, (DEC_ROWS, 2 * WINDOW), 0)
    sj = lax.broadcasted_iota(jnp.int32, (DEC_ROWS, 2 * WINDOW), 1)
    dist = ti + WINDOW - sj
    valid = (dist >= 0) & (dist <= WINDOW) & (sj < WINDOW + t_new)
    negdist = -dist.astype(F32)
    fill = jnp.zeros((WINDOW - DEC_ROWS, KV_WIDTH), F32)
    for b in range(qkv_ref.shape[0]):
        x = qkv_ref[b]
        qs = _head_norm(x[:, :Q_WIDTH], gq_ref[...]) * ATTN_SCALE
        kn = _head_norm(x[:, Q_WIDTH:Q_WIDTH + KV_WIDTH], gk_ref[...])
        v = x[:, Q_WIDTH + KV_WIDTH:]
        kcat = jnp.concatenate([ck_ref[b], kn, fill], axis=0)
        vcat = jnp.concatenate([cv_ref[b], v, fill], axis=0)
        o_ref[b] = _attn_heads(qs, kcat, vcat, negdist, valid, sink_ref, True)
        kn_ref[b] = kn


def _swa_decode(qkv, cache_k, cache_v, g_q, g_k, sinks, t_new):
    b = qkv.shape[0]
    g = DEC_GROUP if b % DEC_GROUP == 0 else 1
    return pl.pallas_call(
        functools.partial(_swa_decode_kernel, t_new=t_new),
        grid=(b // g,),
        in_specs=[pl.BlockSpec(memory_space=pltpu.SMEM),
                  pl.BlockSpec((g, DEC_ROWS, QKV_WIDTH), lambda i: (i, 0, 0)),
                  pl.BlockSpec((g, WINDOW, KV_WIDTH), lambda i: (i, 0, 0)),
                  pl.BlockSpec((g, WINDOW, KV_WIDTH), lambda i: (i, 0, 0)),
                  pl.BlockSpec((1, LANES), lambda i: (0, 0)),
                  pl.BlockSpec((1, LANES), lambda i: (0, 0))],
        out_specs=[pl.BlockSpec((g, DEC_ROWS, Q_WIDTH), lambda i: (i, 0, 0)),
                   pl.BlockSpec((g, DEC_ROWS, KV_WIDTH), lambda i: (i, 0, 0))],
        out_shape=[jax.ShapeDtypeStruct((b, DEC_ROWS, Q_WIDTH), F32),
                   jax.ShapeDtypeStruct((b, DEC_ROWS, KV_WIDTH), F32)],
        compiler_params=_cparams(("parallel",)),
        name="swa_decode",
    )(sinks, qkv, cache_k, cache_v, g_q, g_k)


def _route(logits):
    lane = lax.broadcasted_iota(jnp.int32, logits.shape, 1).astype(F32)
    big = float(ROUTE_LANES)
    lg = jnp.where(lane < N_GROUPS, logits, NEG_INF)
    mg = jnp.max(lg, axis=-1, keepdims=True)
    gsel = jnp.min(jnp.where(lg == mg, lane, big), axis=-1, keepdims=True)
    pg_sel = 1.0 / jnp.sum(jnp.where(lane < N_GROUPS, jnp.exp(logits - mg), 0.0), axis=-1, keepdims=True)
    first = EXPERT_LANE0 + EXPERTS_PER_GROUP * gsel
    le = jnp.where((lane >= first) & (lane < first + EXPERTS_PER_GROUP), logits, NEG_INF)
    m1 = jnp.max(le, axis=-1, keepdims=True)
    i1 = jnp.min(jnp.where(le == m1, lane, big), axis=-1, keepdims=True)
    le2 = jnp.where(lane == i1, NEG_INF, le)
    m2 = jnp.max(le2, axis=-1, keepdims=True)
    i2 = jnp.min(jnp.where(le2 == m2, lane, big), axis=-1, keepdims=True)
    t = jnp.exp(m2 - m1)
    w0 = pg_sel / (1.0 + t)
    w1 = pg_sel * t / (1.0 + t)
    e0 = i1 - EXPERT_LANE0
    e1 = i2 - EXPERT_LANE0
    return jnp.where(lane == 0, e0, jnp.where(lane == 1, e1, jnp.where(lane == 2, w0, jnp.where(lane == 3, w1, 0.0))))


def _outproj_kernel(*refs, hi, n_tail):
    o_ref, w_ref, h_ref, g_ref, wr_ref, br_ref = refs[:6]
    h1_ref, route_ref, h1x_ref, acc_ref = refs[6 + n_tail:]
    i, k = pl.program_id(0), pl.program_id(1)
    n_main = pl.num_programs(0) - (1 if n_tail else 0)
    cdt, prec = _mm_dtype(hi), _mm_prec(hi)

    @pl.when(i < n_main)
    def _():
        @pl.when(k == 0)
        def _():
            acc_ref[...] = h_ref[...]

        acc_ref[...] += jnp.dot(o_ref[...].astype(cdt), w_ref[...], preferred_element_type=F32, precision=prec)

        @pl.when(k == pl.num_programs(1) - 1)
        def _():
            h1 = acc_ref[...]
            h1_ref[...] = h1
            _store_rows(h1x_ref, h1)
            xn = _rms(h1, g_ref[...]).astype(cdt)
            logits = jnp.dot(xn, wr_ref[...], preferred_element_type=F32, precision=prec) + br_ref[...]
            route_ref[...] = _route(logits)

    if n_tail:
        @pl.when((i == n_main) & (k == pl.num_programs(1) - 1))
        def _():
            for src, dst in zip(refs[6:9], (h1_ref, route_ref, h1x_ref)):
                rows = src.shape[0]
                dst[:rows] = src[...]
                dst[rows:] = jnp.zeros((dst.shape[0] - rows, dst.shape[1]), F32)


def _outproj_router(o, w, h, g_ffn, w_r, b_r, *, tm, tk, hi, tail, name):
    m, kdim = o.shape
    d = w.shape[1]
    n_main, n_k = m // tm, kdim // tk
    n_tail = 0 if tail is None else 3
    rows_i = lambda i: jnp.minimum(i, n_main - 1)
    k_i = lambda i, k: jnp.where(i < n_main, k, n_k - 1)
    in_specs = [pl.BlockSpec((tm, tk), lambda i, k: (rows_i(i), k_i(i, k))),
                pl.BlockSpec((tk, d), lambda i, k: (k_i(i, k), 0)),
                pl.BlockSpec((tm, d), lambda i, k: (rows_i(i), 0)),
                pl.BlockSpec((1, d), lambda i, k: (0, 0)),
                pl.BlockSpec((d, ROUTE_LANES), lambda i, k: (0, 0)),
                pl.BlockSpec((1, ROUTE_LANES), lambda i, k: (0, 0))]
    args = [o, w, h, g_ffn, w_r, b_r]
    total_rows = m
    if tail is not None:
        assert tail[0].shape[0] <= tm
        in_specs += [pl.BlockSpec(t.shape, lambda i, k: (0, 0)) for t in tail]
        args += list(tail)
        total_rows += tail[0].shape[0]
    return pl.pallas_call(
        functools.partial(_outproj_kernel, hi=hi, n_tail=n_tail),
        grid=(n_main + (1 if n_tail else 0), n_k),
        in_specs=in_specs,
        out_specs=[pl.BlockSpec((tm, d), lambda i, k: (i, 0)),
                   pl.BlockSpec((tm, ROUTE_LANES), lambda i, k: (i, 0)),
                   pl.BlockSpec((tm * ROW_PITCH, LANES), lambda i, k: (i, 0))],
        out_shape=[jax.ShapeDtypeStruct((total_rows, d), F32),
                   jax.ShapeDtypeStruct((total_rows, ROUTE_LANES), F32),
                   jax.ShapeDtypeStruct((total_rows * ROW_PITCH, LANES), F32)],
        scratch_shapes=[pltpu.VMEM((tm, d), F32)],
        compiler_params=_cparams(("parallel", "arbitrary")),
        name=name,
    )(*args)


def _meta_kernel(eid_ref, pos_ref, tile_ref):
    eid = eid_ref[...]
    r_i = lax.broadcasted_iota(jnp.int32, (LANES, LANES), 0)
    c_i = lax.broadcasted_iota(jnp.int32, (LANES, LANES), 1)
    upper = (r_i <= c_i).astype(BF16)
    ones = jnp.ones((LANES, LANES), BF16)
    rr = lax.broadcasted_iota(jnp.int32, (PAIR_ROWS, PAIR_ROWS), 0)
    cc = lax.broadcasted_iota(jnp.int32, (PAIR_ROWS, PAIR_ROWS), 1)
    below = (cc < rr).astype(BF16)
    lane = lax.broadcasted_iota(jnp.int32, (1, LANES), 1)

    pos = jnp.zeros((PAIR_ROWS, LANES), F32)
    hit = jnp.zeros((PAIR_ROWS, LANES), F32)
    start = jnp.zeros((1, LANES), F32)
    starts = jnp.zeros((1, LANES), F32)
    for e in range(N_EXPERTS):
        mf = (eid == e).astype(F32)
        mb = mf.astype(BF16)
        incl = jnp.dot(mb, upper, preferred_element_type=F32)
        row_tot = jnp.dot(mb, ones, preferred_element_type=F32)
        row_off = jnp.dot(below, row_tot.astype(BF16), preferred_element_type=F32)
        rank = incl - mf + row_off
        cnt = row_off[PAIR_ROWS - 1:PAIR_ROWS, :] + row_tot[PAIR_ROWS - 1:PAIR_ROWS, :]
        padded = jnp.floor((cnt + (MOE_TILE - 1)) * (1.0 / MOE_TILE)) * MOE_TILE
        pos = pos + mf * (start + rank)
        hit = hit + mf
        start = start + padded
        starts = jnp.where(lane > e, start, starts)
    pos_ref[...] = jnp.where(hit > 0, pos, -1.0).astype(jnp.int32)
    tile_ref[...] = jnp.broadcast_to(starts * (1.0 / MOE_TILE), (8, LANES)).astype(jnp.int32)


def _moe_meta(eid_pairs):
    return pl.pallas_call(
        _meta_kernel,
        out_shape=[jax.ShapeDtypeStruct((PAIR_ROWS, LANES), jnp.int32),
                   jax.ShapeDtypeStruct((8, LANES), jnp.int32)],
        compiler_params=pltpu.CompilerParams(vmem_limit_bytes=VMEM_LIMIT),
        name="moe_meta",
    )(eid_pairs)


INV_CHUNK = 2048


def _invert_kernel(pos_ref, inv_ref, *, n_tokens):
    n_pairs = pos_ref.shape[1]
    hi_acc = jnp.zeros((MAX_TILES, MOE_TILE), F32)
    lo_acc = jnp.zeros((MAX_TILES, MOE_TILE), F32)
    hit_acc = jnp.zeros((MAX_TILES, MOE_TILE), F32)
    t_iota = lax.broadcasted_iota(jnp.int32, (MAX_TILES, INV_CHUNK), 0)
    r_iota = lax.broadcasted_iota(jnp.int32, (MOE_TILE, INV_CHUNK), 0)
    for c in range(n_pairs // INV_CHUNK):
        pos = pos_ref[:, c * INV_CHUNK:(c + 1) * INV_CHUNK]
        p = lax.broadcasted_iota(jnp.int32, (1, INV_CHUNK), 1) + c * INV_CHUNK
        tok = jnp.where(p >= n_tokens, p - n_tokens, p)
        in_tile = (pos >> (MOE_TILE.bit_length() - 1)) == t_iota
        a_hi = jnp.where(in_tile, (tok >> 7).astype(F32), 0.0).astype(BF16)
        a_lo = jnp.where(in_tile, (tok & 127).astype(F32), 0.0).astype(BF16)
        b = ((pos & (MOE_TILE - 1)) == r_iota).astype(BF16)
        dn = (((1,), (1,)), ((), ()))
        hi_acc = hi_acc + lax.dot_general(a_hi, b, dn, preferred_element_type=F32)
        lo_acc = lo_acc + lax.dot_general(a_lo, b, dn, preferred_element_type=F32)
        hit_acc = hit_acc + lax.dot_general(in_tile.astype(F32).astype(BF16), b, dn, preferred_element_type=F32)
    slot = (lax.broadcasted_iota(jnp.int32, (MAX_TILES, MOE_TILE), 0) * MOE_TILE
            + lax.broadcasted_iota(jnp.int32, (MAX_TILES, MOE_TILE), 1))
    spread = slot & ((1 << (n_tokens.bit_length() - 1)) - 1)
    inv_ref[...] = jnp.where(hit_acc > 0, (hi_acc * 128.0 + lo_acc).astype(jnp.int32), spread)


def _moe_invert(pos_row, n_tokens):
    return pl.pallas_call(
        functools.partial(_invert_kernel, n_tokens=n_tokens),
        out_shape=jax.ShapeDtypeStruct((MAX_TILES, MOE_TILE), jnp.int32),
        compiler_params=pltpu.CompilerParams(vmem_limit_bytes=VMEM_LIMIT),
        name="moe_invert",
    )(pos_row)


TILE_ROWS = MOE_TILE * ROW_PITCH
GATHER_DEPTH = 3


def _experts_kernel(starts_ref, inv_hbm, hx_hbm, g_ref, wg_ref, wu_ref, wd_ref, y_hbm,
                    xbuf, ybuf, inv_sm, gsem, ysem, isem, wg_bf, wu_bf, wd_bf, *, n_slots):
    e = pl.program_id(0)
    t0, t1 = starts_ref[e], starts_ref[e + 1]
    n_tiles = starts_ref[N_EXPERTS]
    last = n_tiles - 1

    def inv_copy(t, s):
        return pltpu.make_async_copy(inv_hbm.at[jnp.minimum(t, last)], inv_sm.at[s], isem.at[s])

    def y_copy(t, s):
        return pltpu.make_async_copy(ybuf.at[s], y_hbm.at[pl.ds(t * TILE_ROWS, TILE_ROWS)], ysem.at[s])

    def gather_start(s, r):
        _row_copy(hx_hbm, inv_sm[s, 0, r], xbuf.at[s], r, gsem.at[s]).start()

    @pl.when((e == 0) & (n_tiles > 0))
    def _():
        for s in range(GATHER_DEPTH):
            inv_copy(s, s).start()
        for s in range(GATHER_DEPTH - 1):
            inv_copy(s, s).wait()

            def body(r, carry, s=s):
                gather_start(s, r)
                return carry
            lax.fori_loop(0, MOE_TILE, body, 0, unroll=8)

    @pl.when(t1 > t0)
    def _():
        wg_bf[...] = wg_ref[...].astype(BF16)
        wu_bf[...] = wu_ref[...].astype(BF16)
        wd_bf[...] = wd_ref[...].astype(BF16)

        def tile(t, carry):
            s, s_req, s_y = t % GATHER_DEPTH, (t + GATHER_DEPTH - 1) % GATHER_DEPTH, t % 2
            inv_copy(t + GATHER_DEPTH - 1, s_req).wait()
            _rows_wait(hx_hbm, xbuf.at[s], MOE_TILE, gsem.at[s])

            @pl.when(t >= 2)
            def _():
                y_copy(t - 2, s_y).wait()

            x = _load_rows(xbuf.at[s], MOE_TILE)
            for r in range(MOE_TILE):
                gather_start(s_req, r)
            xn = _rms(x, g_ref[...]).astype(BF16)
            hg = jnp.dot(xn, wg_bf[...], preferred_element_type=F32)
            hu = jnp.dot(xn, wu_bf[...], preferred_element_type=F32)
            hid = (hg / (1.0 + jnp.exp(-hg)) * hu).astype(BF16)
            _store_rows(ybuf.at[s_y], jnp.dot(hid, wd_bf[...], preferred_element_type=F32))
            y_copy(t, s_y).start()
            inv_copy(t + GATHER_DEPTH, s).start()
            return carry

        lax.fori_loop(t0, t1, tile, 0)

    @pl.when(e == pl.num_programs(0) - 1)
    def _():
        @pl.when(n_tiles > 0)
        def _():
            for ahead in range(GATHER_DEPTH - 1):
                s = (n_tiles + ahead) % GATHER_DEPTH
                _rows_wait(hx_hbm, xbuf.at[s], MOE_TILE, gsem.at[s])
            inv_copy(last, last % GATHER_DEPTH).wait()
            y_copy(last, last % 2).wait()

        @pl.when(n_tiles > 1)
        def _():
            y_copy(last - 1, (last - 1) % 2).wait()

        ybuf[0] = jnp.zeros((TILE_ROWS, LANES), F32)

        def fill(t, carry):
            y_copy(t, 0).start()
            y_copy(t, 0).wait()
            return carry

        lax.fori_loop(n_tiles, n_slots, fill, 0)


def _moe_experts(tile_tab, inv, hx_all, g_ffn, w_gate, w_up, w_down, layer, n_steps):
    d = D_MODEL

    def widx(e, tab):
        return (layer, e, 0, 0)

    grid_spec = pltpu.PrefetchScalarGridSpec(
        num_scalar_prefetch=1,
        grid=(N_EXPERTS,),
        in_specs=[pl.BlockSpec(memory_space=pl.ANY),
                  pl.BlockSpec(memory_space=pl.ANY),
                  pl.BlockSpec((1, d), lambda e, tab: (0, 0)),
                  pl.BlockSpec((None, None, d, D_EXPERT), widx),
                  pl.BlockSpec((None, None, d, D_EXPERT), widx),
                  pl.BlockSpec((None, None, D_EXPERT, d), widx)],
        out_specs=pl.BlockSpec(memory_space=pl.ANY),
        scratch_shapes=[pltpu.VMEM((GATHER_DEPTH, TILE_ROWS, LANES), F32),
                        pltpu.VMEM((2, TILE_ROWS, LANES), F32),
                        pltpu.SMEM((GATHER_DEPTH, 1, MOE_TILE), jnp.int32),
                        pltpu.SemaphoreType.DMA((GATHER_DEPTH,)),
                        pltpu.SemaphoreType.DMA((2,)),
                        pltpu.SemaphoreType.DMA((GATHER_DEPTH,)),
                        pltpu.VMEM((d, D_EXPERT), BF16),
                        pltpu.VMEM((d, D_EXPERT), BF16),
                        pltpu.VMEM((D_EXPERT, d), BF16)],
    )
    return pl.pallas_call(
        functools.partial(_experts_kernel, n_slots=n_steps),
        grid_spec=grid_spec,
        out_shape=jax.ShapeDtypeStruct((n_steps * TILE_ROWS, LANES), F32),
        compiler_params=_cparams(("arbitrary",)),
        name="moe_experts",
    )(tile_tab, inv, hx_all, g_ffn, w_gate, w_up, w_down)


def _combine_kernel(*refs, tm, n_steps):
    lists = [(refs[k], refs[GATHER_DEPTH + k]) for k in range(GATHER_DEPTH)]
    y_hbm, h_ref, route_ref, p_ref, g_ref, wg_ref, wp_ref, o_ref, ybuf, sems = refs[2 * GATHER_DEPTH:]
    i = pl.program_id(0)
    s, s_req = i % GATHER_DEPTH, (i + GATHER_DEPTH - 1) % GATHER_DEPTH

    def request(p0, p1, sl, r):
        _row_copy(y_hbm, p0[0, 0, r], ybuf.at[sl, 0], r, sems.at[sl]).start()
        _row_copy(y_hbm, p1[0, 0, r], ybuf.at[sl, 1], r, sems.at[sl]).start()

    def wait_tile(sl):
        for c in range(2):
            _rows_wait(y_hbm, ybuf.at[sl, c], tm, sems.at[sl])

    @pl.when(i == 0)
    def _():
        for k in range(GATHER_DEPTH - 1):
            def body(r, carry, k=k):
                request(lists[k][0], lists[k][1], k, r)
                return carry
            lax.fori_loop(0, tm, body, 0, unroll=8)

    wait_tile(s)
    route = route_ref[...]
    y0, y1 = _load_rows(ybuf.at[s, 0], tm), _load_rows(ybuf.at[s, 1], tm)
    for r in range(tm):
        request(lists[-1][0], lists[-1][1], s_req, r)
    h2 = h_ref[...] + (route[:, 2:3] * y0 + route[:, 3:4] * y1)
    xn = _rms(h2, g_ref[...]).astype(BF16)
    z = jnp.dot(xn, wg_ref[...], preferred_element_type=F32)
    gate = 1.0 / (1.0 + jnp.exp(-z))
    proj = jnp.dot(p_ref[...].astype(BF16), wp_ref[...], preferred_element_type=F32)
    o_ref[...] = h2 + proj * gate

    @pl.when(i == n_steps - 1)
    def _():
        for ahead in range(GATHER_DEPTH - 1):
            wait_tile((n_steps + ahead) % GATHER_DEPTH)


def _moe_combine_ple(pos0, pos1, y_sorted, h_all, route_all, p, g_ple, w_gate, w_proj, *, layer, tm, rows,
                     row_block0, name):
    d = h_all.shape[1]
    n = rows // tm
    last = n - 1
    ahead = [pl.BlockSpec((1, 1, tm), lambda i, k=k: (jnp.minimum(i + k, last), 0, 0), memory_space=pltpu.SMEM)
             for k in range(GATHER_DEPTH)]
    return pl.pallas_call(
        functools.partial(_combine_kernel, tm=tm, n_steps=n),
        grid=(n,),
        in_specs=ahead + ahead + [
                  pl.BlockSpec(memory_space=pl.ANY),
                  pl.BlockSpec((tm, d), lambda i: (row_block0 + i, 0)),
                  pl.BlockSpec((tm, ROUTE_LANES), lambda i: (row_block0 + i, 0)),
                  pl.BlockSpec((None, tm, PLE_DIM), lambda i: (layer, i, 0)),
                  pl.BlockSpec((1, d), lambda i: (0, 0)),
                  pl.BlockSpec((d, d), lambda i: (0, 0)),
                  pl.BlockSpec((PLE_DIM, d), lambda i: (0, 0))],
        out_specs=pl.BlockSpec((tm, d), lambda i: (i, 0)),
        out_shape=jax.ShapeDtypeStruct((rows, d), F32),
        scratch_shapes=[pltpu.VMEM((GATHER_DEPTH, 2, tm * ROW_PITCH, LANES), F32),
                        pltpu.SemaphoreType.DMA((GATHER_DEPTH,))],
        compiler_params=_cparams(("arbitrary",)),
        name=name,
    )(*([pos0] * GATHER_DEPTH + [pos1] * GATHER_DEPTH), y_sorted, h_all, route_all, p, g_ple, w_gate, w_proj)


def _ret_head(q, k, v, gt, state, gn, lg, c_real):
    c = q.shape[0]
    k = k * (RET_DK ** -0.5)
    ri = lax.broadcasted_iota(jnp.int32, (c, c), 0)
    ci = lax.broadcasted_iota(jnp.int32, (c, c), 1)
    diff = (ri - ci).astype(F32)
    decay = jnp.where(diff >= 0, jnp.exp(lg * jnp.maximum(diff, 0.0)), 0.0)
    idx = lax.broadcasted_iota(jnp.int32, (c, 1), 0).astype(F32)
    qk = lax.dot_general(q.astype(BF16), k.astype(BF16), (((1,), (1,)), ((), ())), preferred_element_type=F32)
    inner = qk * decay
    q_dec = q * jnp.exp(lg * (idx + 1.0))
    k_dec = k * jnp.exp(lg * (c_real - 1.0 - idx))
    vb = v.astype(BF16)
    out = (jnp.dot(inner.astype(BF16), vb, preferred_element_type=F32)
           + jnp.dot(q_dec.astype(BF16), state.astype(BF16), preferred_element_type=F32))
    new_state = math.exp(lg * c_real) * state + lax.dot_general(
        k_dec.astype(BF16), vb, (((0,), (0,)), ((), ())), preferred_element_type=F32)
    mu = jnp.mean(out, axis=-1, keepdims=True)
    cen = out - mu
    var = jnp.mean(cen * cen, axis=-1, keepdims=True)
    on = cen * lax.rsqrt(var + GN_EPS) * gn
    return gt / (1.0 + jnp.exp(-gt)) * on, new_state


def _ret_prompt_kernel(q_ref, k_ref, v_ref, g_ref, gn_ref, y_ref, so_ref, st_ref):
    c = pl.program_id(1)

    @pl.when(c == 0)
    def _():
        st_ref[...] = jnp.zeros_like(st_ref)

    for h in range(RET_HEADS):
        ks, vs = slice(h * RET_DK, (h + 1) * RET_DK), slice(h * RET_DV, (h + 1) * RET_DV)
        y, new_state = _ret_head(q_ref[:, ks], k_ref[:, ks], v_ref[:, vs], g_ref[:, vs], st_ref[h],
                                 gn_ref[:, vs], RET_LOG_GAMMA[h], RET_CHUNK)
        st_ref[h] = new_state
        y_ref[:, vs] = y.astype(y_ref.dtype)

    @pl.when(c == pl.num_programs(1) - 1)
    def _():
        so_ref[0] = st_ref[...]


def _ret_prompt(qkvg, g_norm, batch):
    rows = qkvg.shape[0]
    nc = rows // batch // RET_CHUNK
    return pl.pallas_call(
        _ret_prompt_kernel,
        grid=(batch, nc),
        in_specs=[pl.BlockSpec((RET_CHUNK, RET_K_WIDTH), lambda b, c: (b * nc + c, 0)),
                  pl.BlockSpec((RET_CHUNK, RET_K_WIDTH), lambda b, c: (b * nc + c, 1)),
                  pl.BlockSpec((RET_CHUNK, RET_V_WIDTH), lambda b, c: (b * nc + c, 1)),
                  pl.BlockSpec((RET_CHUNK, RET_V_WIDTH), lambda b, c: (b * nc + c, 2)),
                  pl.BlockSpec((1, RET_V_WIDTH), lambda b, c: (0, 0))],
        out_specs=[pl.BlockSpec((RET_CHUNK, RET_V_WIDTH), lambda b, c: (b * nc + c, 0)),
                   pl.BlockSpec((1, RET_HEADS, RET_DK, RET_DV), lambda b, c: (b, 0, 0, 0))],
        out_shape=[jax.ShapeDtypeStruct((rows, RET_V_WIDTH), BF16),
                   jax.ShapeDtypeStruct((batch, RET_HEADS, RET_DK, RET_DV), F32)],
        scratch_shapes=[pltpu.VMEM((RET_HEADS, RET_DK, RET_DV), F32)],
        compiler_params=_cparams(("parallel", "arbitrary")),
        name="ret_prompt",
    )(qkvg, qkvg, qkvg, qkvg, g_norm)


def _ret_decode_kernel(x_ref, s_ref, gn_ref, y_ref, so_ref, *, t_new):
    x = x_ref[0]
    for h in range(RET_HEADS):
        q = x[:, h * RET_DK:(h + 1) * RET_DK]
        k = x[:, RET_K_WIDTH + h * RET_DK:RET_K_WIDTH + (h + 1) * RET_DK]
        v0 = 2 * RET_K_WIDTH + h * RET_DV
        g0 = 2 * RET_K_WIDTH + RET_V_WIDTH + h * RET_DV
        y, new_state = _ret_head(q, k, x[:, v0:v0 + RET_DV], x[:, g0:g0 + RET_DV], s_ref[0, h],
                                 gn_ref[:, h * RET_DV:(h + 1) * RET_DV], RET_LOG_GAMMA[h], t_new)
        so_ref[0, h] = new_state
        y_ref[0, :, h * RET_DV:(h + 1) * RET_DV] = y


def _ret_decode(qkvg, state, g_norm, t_new):
    b = qkvg.shape[0]
    return pl.pallas_call(
        functools.partial(_ret_decode_kernel, t_new=t_new),
        grid=(b,),
        in_specs=[pl.BlockSpec((1, DEC_ROWS, RET_IN_WIDTH), lambda i: (i, 0, 0)),
                  pl.BlockSpec((1, RET_HEADS, RET_DK, RET_DV), lambda i: (i, 0, 0, 0)),
                  pl.BlockSpec((1, RET_V_WIDTH), lambda i: (0, 0))],
        out_specs=[pl.BlockSpec((1, DEC_ROWS, RET_V_WIDTH), lambda i: (i, 0, 0)),
                   pl.BlockSpec((1, RET_HEADS, RET_DK, RET_DV), lambda i: (i, 0, 0, 0))],
        out_shape=[jax.ShapeDtypeStruct((b, DEC_ROWS, RET_V_WIDTH), F32),
                   jax.ShapeDtypeStruct((b, RET_HEADS, RET_DK, RET_DV), F32)],
        compiler_params=_cparams(("parallel",)),
        name="ret_decode",
    )(qkvg, state, g_norm)


def _pad_rows(x, b, t):
    return jnp.pad(x.reshape(b, t, x.shape[-1]), ((0, 0), (0, DEC_ROWS - t), (0, 0)))


def _moe_layer(h_all, route_all, hx_all, p_prompt, p_sample, g_ffn, g_ple, w_gate_e, w_up_e, w_down_e, w_ple_gate,
               w_ple_proj, layer, n_prompt, n_sample):
    n_tok = n_prompt + n_sample
    n_pairs = PAIR_ROWS * LANES
    eid = route_all[:, :2].astype(jnp.int32).T.reshape(-1)
    eid = jnp.pad(eid, (0, n_pairs - 2 * n_tok), constant_values=-1).reshape(PAIR_ROWS, LANES)
    pos, tile_tab = _moe_meta(eid)
    inv = _moe_invert(pos.reshape(1, n_pairs), n_tok).reshape(MAX_TILES, 1, MOE_TILE)
    max_tiles = (2 * n_tok + N_EXPERTS * (MOE_TILE - 1)) // MOE_TILE
    y_sorted = _moe_experts(tile_tab[0], inv, hx_all, g_ffn, w_gate_e, w_up_e, w_down_e, layer, max_tiles)
    pos2 = pos.reshape(-1)[:2 * n_tok].reshape(2, n_tok)
    tm_p, tm_s = 256, n_sample
    outs = []
    for lo, rows, tm, p, nm in ((0, n_prompt, tm_p, p_prompt, "combine_prompt"),
                                (n_prompt, n_sample, tm_s, p_sample, "combine_sample")):
        pos0 = pos2[0, lo:lo + rows].reshape(rows // tm, 1, tm)
        pos1 = pos2[1, lo:lo + rows].reshape(rows // tm, 1, tm)
        outs.append(_moe_combine_ple(pos0, pos1, y_sorted, h_all, route_all, p, g_ple, w_ple_gate, w_ple_proj,
                                     layer=layer, tm=tm, rows=rows, row_block0=lo // tm, name=nm))
    return outs


def kernel(x_prompt, x_sample, cache_k_swa, cache_v_swa, state_ret, p_prompt, p_sample, g_mix, g_ffn, g_ple,
           swa_w_qkv, swa_g_q, swa_g_k, swa_sinks, swa_w_o, ret_w_in, ret_g_norm, ret_w_o, moe_w_group,
           moe_b_group, moe_w_expert, moe_b_expert, moe_w_gate, moe_w_up, moe_w_down, ple_w_proj, ple_w_gate):
    batch, seq, d = x_prompt.shape
    dec_b, dec_t, _ = x_sample.shape
    n_prompt, n_sample = batch * seq, dec_b * dec_t
    n_tok = n_prompt + n_sample
    assert 2 * n_tok <= PAIR_ROWS * LANES and 2 * n_tok + N_EXPERTS * (MOE_TILE - 1) <= (MAX_TILES - 1) * MOE_TILE
    assert dec_t <= DEC_ROWS and n_prompt % 1024 == 0 and n_sample % 8 == 0

    xp = x_prompt.reshape(n_prompt, d)
    xs = x_sample.reshape(n_sample, d)
    pp = p_prompt.reshape(DEPTH, n_prompt, PLE_DIM)
    ps = p_sample.reshape(DEPTH, n_sample, PLE_DIM)
    row = lambda a: a.reshape(1, -1)

    def router_w(i):
        w = jnp.concatenate([moe_w_group[i], moe_w_expert[i]], axis=1)
        b = jnp.concatenate([moe_b_group[i], moe_b_expert[i]])
        pad = ROUTE_LANES - w.shape[1]
        return jnp.pad(w, ((0, 0), (0, pad))), jnp.pad(b, (0, pad)).reshape(1, ROUTE_LANES)

    g_q2, g_k2 = row(jnp.tile(swa_g_q[0], 2)), row(jnp.tile(swa_g_k[0], 2))
    w_r, b_r = router_w(0)
    qkv_p = _norm_proj(xp, row(g_mix[0]), swa_w_qkv[0].astype(BF16), tm=1024, tn=512, hi=False, name="qkv_prompt")
    o_p, k_p, v_p = _swa_prompt(qkv_p, g_q2, g_k2, swa_sinks[0], batch)

    qkv_s = _norm_proj(xs, row(g_mix[0]), swa_w_qkv[0], tm=n_sample, tn=512, hi=True, name="qkv_sample")
    ck = cache_k_swa[0].reshape(dec_b, WINDOW, KV_WIDTH)
    cv = cache_v_swa[0].reshape(dec_b, WINDOW, KV_WIDTH)
    o_s, kn_s = _swa_decode(_pad_rows(qkv_s, dec_b, dec_t), ck, cv, g_q2, g_k2, swa_sinks[0], dec_t)
    o_s = o_s[:, :dec_t].reshape(n_sample, Q_WIDTH)
    tail = _outproj_router(o_s, swa_w_o[0], xs, row(g_ffn[0]), w_r, b_r, tm=n_sample, tk=2048, hi=True,
                           tail=None, name="swa_out_sample")
    h_all, route_all, hx_all = _outproj_router(o_p, swa_w_o[0].astype(BF16), xp, row(g_ffn[0]), w_r.astype(BF16),
                                               b_r, tm=512, tk=2048, hi=False, tail=tail, name="swa_out_prompt")
    k_s = jnp.concatenate([ck[:, dec_t:], kn_s[:, :dec_t]], axis=1)
    v_new = qkv_s[:, Q_WIDTH + KV_WIDTH:].reshape(dec_b, dec_t, KV_WIDTH)
    v_s = jnp.concatenate([cv[:, dec_t:], v_new], axis=1)

    h_p, h_s = _moe_layer(h_all, route_all, hx_all, pp, ps, row(g_ffn[0]), row(g_ple[0]), moe_w_gate, moe_w_up,
                          moe_w_down, ple_w_gate[0].astype(BF16), ple_w_proj[0].astype(BF16), 0, n_prompt, n_sample)

    w_in = ret_w_in[0].astype(BF16)
    w_o = ret_w_o[0].astype(BF16)
    w_r, b_r = router_w(1)
    w_r = w_r.astype(BF16)
    qkvg_p = _norm_proj(h_p, row(g_mix[1]), w_in, tm=1024, tn=512, hi=False, name="ret_in_prompt")
    y_p, s_p = _ret_prompt(qkvg_p, row(ret_g_norm[0]), batch)

    qkvg_s = _norm_proj(h_s, row(g_mix[1]), w_in, tm=n_sample, tn=512, hi=False, name="ret_in_sample")
    y_s, s_s = _ret_decode(_pad_rows(qkvg_s, dec_b, dec_t), state_ret[0], row(ret_g_norm[0]), dec_t)
    y_s = y_s[:, :dec_t].reshape(n_sample, RET_V_WIDTH)
    tail = _outproj_router(y_s, w_o, h_s, row(g_ffn[1]), w_r, b_r, tm=n_sample, tk=2048, hi=False, tail=None,
                           name="ret_out_sample")
    h_all, route_all, hx_all = _outproj_router(y_p, w_o, h_p, row(g_ffn[1]), w_r, b_r, tm=512, tk=2048, hi=False,
                                               tail=tail, name="ret_out_prompt")

    y_prompt, y_sample = _moe_layer(h_all, route_all, hx_all, pp, ps, row(g_ffn[1]), row(g_ple[1]), moe_w_gate,
                                    moe_w_up, moe_w_down, ple_w_gate[1].astype(BF16), ple_w_proj[1].astype(BF16), 1,
                                    n_prompt, n_sample)

    kv_shape = (1, -1, WINDOW, SWA_KV_HEADS, SWA_HEAD_DIM)
    return (y_prompt.reshape(batch, seq, d), y_sample.reshape(dec_b, dec_t, d),
            k_p.reshape(kv_shape), v_p.reshape(kv_shape), s_p[None],
            k_s.reshape(kv_shape), v_s.reshape(kv_shape), s_s[None])
```

```python
import functools
import math

import jax
import jax.numpy as jnp
from jax import lax
from jax.experimental import pallas as pl
from jax.experimental.pallas import tpu as pltpu

D_MODEL = 2048
DEPTH = 2
SWA_HEADS = 32
SWA_KV_HEADS = 8
SWA_HEAD_DIM = 64
Q_WIDTH = SWA_HEADS * SWA_HEAD_DIM
KV_WIDTH = SWA_KV_HEADS * SWA_HEAD_DIM
QKV_WIDTH = Q_WIDTH + 2 * KV_WIDTH
WINDOW = 128
ATTN_SCALE = SWA_HEAD_DIM ** -0.5
LOG2E = math.log2(math.e)
RET_HEADS = 8
RET_DK = D_MODEL // RET_HEADS
RET_DV = 2 * D_MODEL // RET_HEADS
RET_K_WIDTH = RET_HEADS * RET_DK
RET_V_WIDTH = RET_HEADS * RET_DV
RET_IN_WIDTH = 2 * RET_K_WIDTH + 2 * RET_V_WIDTH
RET_CHUNK = 128
N_GROUPS = 4
EXPERTS_PER_GROUP = 8
N_EXPERTS = N_GROUPS * EXPERTS_PER_GROUP
D_EXPERT = D_MODEL // 4
PLE_DIM = 256
NORM_EPS = 1e-6
GN_EPS = 1e-5
NEG_INF = -1e30

LANES = 128
ROUTE_LANES = LANES
EXPERT_LANE0 = N_GROUPS
MOE_TILE = 256
PAIR_ROWS = 256
MAX_TILES = LANES
VMEM_LIMIT = 56 * 1024 * 1024
ROW_TILES = D_MODEL // LANES
ROW_PITCH = ROW_TILES + 4

F32 = jnp.float32
BF16 = jnp.bfloat16
HIGHEST = lax.Precision.HIGHEST

ALIBI_SLOPES = tuple(2.0 ** (-8.0 * (h + 1) / SWA_HEADS) for h in range(SWA_HEADS))
RET_LOG_GAMMA = tuple(math.log(1.0 - 2.0 ** (-5.0 - h)) for h in range(RET_HEADS))


def _cparams(sem):
    return pltpu.CompilerParams(dimension_semantics=sem, vmem_limit_bytes=VMEM_LIMIT)


def _mm_dtype(hi):
    return F32 if hi else BF16


def _mm_prec(hi):
    return HIGHEST if hi else None


def _rms(x, g):
    return x * lax.rsqrt(jnp.mean(x * x, axis=-1, keepdims=True) + NORM_EPS) * g


def _store_rows(ref, x):
    n = x.shape[0]
    for s in range(ROW_PITCH):
        piece = x[:, s * LANES:(s + 1) * LANES] if s < ROW_TILES else jnp.zeros((n, LANES), F32)
        ref[pl.ds(s, n, stride=ROW_PITCH), :] = piece


def _load_rows(ref, n):
    return jnp.concatenate([ref[pl.ds(s, n, stride=ROW_PITCH), :] for s in range(ROW_TILES)], axis=1)


def _row_copy(src_hbm, src_row, dst, dst_row, sem):
    return pltpu.make_async_copy(src_hbm.at[pl.ds(src_row * ROW_PITCH, ROW_TILES)],
                                 dst.at[pl.ds(dst_row * ROW_PITCH, ROW_TILES)], sem)


def _rows_wait(src_hbm, dst, n, sem):
    pltpu.make_async_copy(src_hbm.at[pl.ds(0, n * ROW_TILES)], dst.at[pl.ds(0, n * ROW_TILES)], sem).wait()


def _proj_kernel(x_ref, g_ref, w_ref, o_ref, a_ref, *, hi):
    @pl.when(pl.program_id(1) == 0)
    def _():
        a_ref[...] = _rms(x_ref[...], g_ref[...]).astype(a_ref.dtype)

    o_ref[...] = jnp.dot(a_ref[...], w_ref[...], preferred_element_type=F32,
                         precision=_mm_prec(hi)).astype(o_ref.dtype)


def _norm_proj(x, g, w, *, tm, tn, hi, name):
    m, k = x.shape
    n = w.shape[1]
    return pl.pallas_call(
        functools.partial(_proj_kernel, hi=hi),
        grid=(m // tm, n // tn),
        in_specs=[pl.BlockSpec((tm, k), lambda i, j: (i, 0)),
                  pl.BlockSpec((1, k), lambda i, j: (0, 0)),
                  pl.BlockSpec((k, tn), lambda i, j: (0, j))],
        out_specs=pl.BlockSpec((tm, tn), lambda i, j: (i, j)),
        out_shape=jax.ShapeDtypeStruct((m, n), F32),
        scratch_shapes=[pltpu.VMEM((tm, k), _mm_dtype(hi))],
        compiler_params=_cparams(("parallel", "arbitrary")),
        name=name,
    )(x, g, w)


def _head_norm(x, g2):
    lane = lax.broadcasted_iota(jnp.int32, (1, LANES), 1)
    lo = lane < SWA_HEAD_DIM
    outs = []
    for j in range(x.shape[1] // LANES):
        xb = x[:, j * LANES:(j + 1) * LANES]
        x2 = xb * xb
        s_lo = jnp.sum(jnp.where(lo, x2, 0.0), axis=-1, keepdims=True)
        s_hi = jnp.sum(jnp.where(lo, 0.0, x2), axis=-1, keepdims=True)
        r = jnp.where(lo, lax.rsqrt(s_lo * (1.0 / SWA_HEAD_DIM) + NORM_EPS),
                      lax.rsqrt(s_hi * (1.0 / SWA_HEAD_DIM) + NORM_EPS))
        outs.append(xb * r * g2)
    return jnp.concatenate(outs, axis=1)


def _attn_heads(q, g_q2, kn, v, bias_ref, sink_ref, hi, emit):
    nq, nk = q.shape[0], kn.shape[0]
    cdt, prec = _mm_dtype(hi), _mm_prec(hi)
    lane = lax.broadcasted_iota(jnp.int32, (1, LANES), 1)
    lo = lane < SWA_HEAD_DIM
    group = SWA_HEADS // SWA_KV_HEADS
    ones_bd = (lo == (lax.broadcasted_iota(jnp.int32, (2 * nk, LANES), 0) < nk)).astype(cdt)
    for m in range(KV_WIDTH // LANES):
        kb = kn[:, m * LANES:(m + 1) * LANES]
        vb = v[:, m * LANES:(m + 1) * LANES]
        kr = pltpu.roll(kb, SWA_HEAD_DIM, 1)
        vr = pltpu.roll(vb, SWA_HEAD_DIM, 1)
        for half in range(2):
            kv = 2 * m + half
            k_lo, k_hi = (kb, kr) if half == 0 else (kr, kb)
            v_lo, v_hi = (vb, vr) if half == 0 else (vr, vb)
            kbd = jnp.concatenate([jnp.where(lo, k_lo, 0.0), jnp.where(lo, 0.0, k_hi)], axis=0).astype(cdt)
            vbd = jnp.concatenate([jnp.where(lo, v_lo, 0.0), jnp.where(lo, 0.0, v_hi)], axis=0).astype(cdt)
            c0 = kv * group * SWA_HEAD_DIM
            qs = _head_norm(q[:, c0:c0 + 2 * LANES], g_q2) * (ATTN_SCALE * LOG2E)
            q2 = jnp.concatenate([qs[:, :LANES], qs[:, LANES:]], axis=0).astype(cdt)
            s_all = lax.dot_general(q2, kbd, (((1,), (1,)), ((), ())), preferred_element_type=F32, precision=prec)
            e_rows, sink_e = [], []
            for r in range(2):
                e_cols = []
                for c in range(2):
                    h = kv * group + 2 * r + c
                    s = s_all[r * nq:(r + 1) * nq, c * nk:(c + 1) * nk] + bias_ref[h]
                    sk = sink_ref[h]
                    mx = jnp.maximum(jnp.max(s, axis=-1, keepdims=True), sk)
                    e_cols.append(jnp.exp2(s - mx).astype(cdt))
                    sink_e.append(jnp.exp2(sk - mx))
                e_rows.append(jnp.concatenate(e_cols, axis=1))
            p2 = jnp.concatenate(e_rows, axis=0)
            o2 = jnp.dot(p2, vbd, preferred_element_type=F32, precision=prec)
            den2 = jnp.dot(p2, ones_bd, preferred_element_type=F32, precision=prec)
            for r in range(2):
                den = den2[r * nq:(r + 1) * nq] + jnp.where(lo, sink_e[2 * r], sink_e[2 * r + 1])
                emit(2 * kv + r, o2[r * nq:(r + 1) * nq] / den)


def _swa_bias(nq, q_pos0, n_keys, first_key):
    t = jnp.arange(nq)[:, None] + q_pos0
    s = jnp.arange(2 * WINDOW)[None, :]
    dist = t - s
    valid = (dist >= 0) & (dist <= WINDOW) & (s >= first_key) & (s < n_keys)
    slopes = jnp.asarray(ALIBI_SLOPES, F32)[:, None, None]
    return jnp.where(valid[None], -slopes * dist.astype(F32)[None] * LOG2E, NEG_INF)


def _swa_prompt_kernel(sink_ref, q_ref, k_ref, v_ref, gq_ref, gk_ref, bias_ref, o_ref, ko_ref, vo_ref, kprev, vprev):
    n = pl.program_id(1)

    @pl.when(n == 0)
    def _():
        kprev[...] = jnp.zeros_like(kprev)
        vprev[...] = jnp.zeros_like(vprev)

    kn = _head_norm(k_ref[...], gk_ref[...])
    v = v_ref[...]
    kcat = jnp.concatenate([kprev[...], kn], axis=0)
    vcat = jnp.concatenate([vprev[...], v], axis=0)

    def emit(j, block):
        o_ref[:, j * LANES:(j + 1) * LANES] = block.astype(o_ref.dtype)

    _attn_heads(q_ref[...], gq_ref[...], kcat, vcat, bias_ref, sink_ref, False, emit)
    kprev[...] = kn
    vprev[...] = v

    @pl.when(n == pl.num_programs(1) - 1)
    def _():
        ko_ref[0] = kn
        vo_ref[0] = v


def _swa_prompt(qkv, g_q, g_k, sinks, batch):
    rows = qkv.shape[0]
    nb = rows // batch // WINDOW
    kblk = Q_WIDTH // KV_WIDTH
    bias = jnp.stack([_swa_bias(WINDOW, WINDOW, 2 * WINDOW, WINDOW), _swa_bias(WINDOW, WINDOW, 2 * WINDOW, 0)])
    return pl.pallas_call(
        _swa_prompt_kernel,
        grid=(batch, nb),
        in_specs=[pl.BlockSpec(memory_space=pltpu.SMEM),
                  pl.BlockSpec((WINDOW, Q_WIDTH), lambda b, n: (b * nb + n, 0)),
                  pl.BlockSpec((WINDOW, KV_WIDTH), lambda b, n: (b * nb + n, kblk)),
                  pl.BlockSpec((WINDOW, KV_WIDTH), lambda b, n: (b * nb + n, kblk + 1)),
                  pl.BlockSpec((1, LANES), lambda b, n: (0, 0)),
                  pl.BlockSpec((1, LANES), lambda b, n: (0, 0)),
                  pl.BlockSpec((None, SWA_HEADS, WINDOW, 2 * WINDOW), lambda b, n: (jnp.minimum(n, 1), 0, 0, 0))],
        out_specs=[pl.BlockSpec((WINDOW, Q_WIDTH), lambda b, n: (b * nb + n, 0)),
                   pl.BlockSpec((1, WINDOW, KV_WIDTH), lambda b, n: (b, 0, 0)),
                   pl.BlockSpec((1, WINDOW, KV_WIDTH), lambda b, n: (b, 0, 0))],
        out_shape=[jax.ShapeDtypeStruct((rows, Q_WIDTH), BF16),
                   jax.ShapeDtypeStruct((batch, WINDOW, KV_WIDTH), F32),
                   jax.ShapeDtypeStruct((batch, WINDOW, KV_WIDTH), F32)],
        scratch_shapes=[pltpu.VMEM((WINDOW, KV_WIDTH), F32), pltpu.VMEM((WINDOW, KV_WIDTH), F32)],
        compiler_params=_cparams(("parallel", "arbitrary")),
        name="swa_prompt",
    )(sinks, qkv, qkv, qkv, g_q, g_k, bias)


DEC_ROWS = 8
DEC_GROUP = 4


def _swa_decode_kernel(sink_ref, qkv_ref, ck_ref, cv_ref, gq_ref, gk_ref, bias_ref, o_ref, kn_ref):
    fill = jnp.zeros((WINDOW - DEC_ROWS, KV_WIDTH), F32)
    for b in range(qkv_ref.shape[0]):
        x = qkv_ref[b]
        kn = _head_norm(x[:, Q_WIDTH:Q_WIDTH + KV_WIDTH], gk_ref[...])
        v = x[:, Q_WIDTH + KV_WIDTH:]
        kcat = jnp.concatenate([ck_ref[b], kn, fill], axis=0)
        vcat = jnp.concatenate([cv_ref[b], v, fill], axis=0)

        def emit(j, block, b=b):
            o_ref[b, :, j * LANES:(j + 1) * LANES] = block

        _attn_heads(x[:, :Q_WIDTH], gq_ref[...], kcat, vcat, bias_ref, sink_ref, True, emit)
        kn_ref[b] = kn


def _swa_decode(qkv, cache_k, cache_v, g_q, g_k, sinks, t_new):
    b = qkv.shape[0]
    g = DEC_GROUP if b % DEC_GROUP == 0 else 1
    bias = _swa_bias(DEC_ROWS, WINDOW, WINDOW + t_new, 0)
    return pl.pallas_call(
        _swa_decode_kernel,
        grid=(b // g,),
        in_specs=[pl.BlockSpec(memory_space=pltpu.SMEM),
                  pl.BlockSpec((g, DEC_ROWS, QKV_WIDTH), lambda i: (i, 0, 0)),
                  pl.BlockSpec((g, WINDOW, KV_WIDTH), lambda i: (i, 0, 0)),
                  pl.BlockSpec((g, WINDOW, KV_WIDTH), lambda i: (i, 0, 0)),
                  pl.BlockSpec((1, LANES), lambda i: (0, 0)),
                  pl.BlockSpec((1, LANES), lambda i: (0, 0)),
                  pl.BlockSpec((SWA_HEADS, DEC_ROWS, 2 * WINDOW), lambda i: (0, 0, 0))],
        out_specs=[pl.BlockSpec((g, DEC_ROWS, Q_WIDTH), lambda i: (i, 0, 0)),
                   pl.BlockSpec((g, DEC_ROWS, KV_WIDTH), lambda i: (i, 0, 0))],
        out_shape=[jax.ShapeDtypeStruct((b, DEC_ROWS, Q_WIDTH), F32),
                   jax.ShapeDtypeStruct((b, DEC_ROWS, KV_WIDTH), F32)],
        compiler_params=_cparams(("parallel",)),
        name="swa_decode",
    )(sinks, qkv, cache_k, cache_v, g_q, g_k, bias)


def _route(logits):
    lane = lax.broadcasted_iota(jnp.int32, logits.shape, 1).astype(F32)
    big = float(ROUTE_LANES)
    lg = jnp.where(lane < N_GROUPS, logits, NEG_INF)
    mg = jnp.max(lg, axis=-1, keepdims=True)
    gsel = jnp.min(jnp.where(lg == mg, lane, big), axis=-1, keepdims=True)
    pg_sel = 1.0 / jnp.sum(jnp.where(lane < N_GROUPS, jnp.exp(logits - mg), 0.0), axis=-1, keepdims=True)
    first = EXPERT_LANE0 + EXPERTS_PER_GROUP * gsel
    le = jnp.where((lane >= first) & (lane < first + EXPERTS_PER_GROUP), logits, NEG_INF)
    m1 = jnp.max(le, axis=-1, keepdims=True)
    i1 = jnp.min(jnp.where(le == m1, lane, big), axis=-1, keepdims=True)
    le2 = jnp.where(lane == i1, NEG_INF, le)
    m2 = jnp.max(le2, axis=-1, keepdims=True)
    i2 = jnp.min(jnp.where(le2 == m2, lane, big), axis=-1, keepdims=True)
    t = jnp.exp(m2 - m1)
    w0 = pg_sel / (1.0 + t)
    w1 = pg_sel * t / (1.0 + t)
    e0 = i1 - EXPERT_LANE0
    e1 = i2 - EXPERT_LANE0
    return jnp.where(lane == 0, e0, jnp.where(lane == 1, e1, jnp.where(lane == 2, w0, jnp.where(lane == 3, w1, 0.0))))


def _outproj_kernel(*refs, hi, n_tail):
    o_ref, w_ref, h_ref, g_ref, wr_ref, br_ref = refs[:6]
    h1_ref, route_ref, h1x_ref, acc_ref = refs[6 + n_tail:]
    i, k = pl.program_id(0), pl.program_id(1)
    n_main = pl.num_programs(0) - (1 if n_tail else 0)
    cdt, prec = _mm_dtype(hi), _mm_prec(hi)

    @pl.when(i < n_main)
    def _():
        @pl.when(k == 0)
        def _():
            acc_ref[...] = h_ref[...]

        acc_ref[...] += jnp.dot(o_ref[...].astype(cdt), w_ref[...], preferred_element_type=F32, precision=prec)

        @pl.when(k == pl.num_programs(1) - 1)
        def _():
            h1 = acc_ref[...]
            h1_ref[...] = h1
            _store_rows(h1x_ref, h1)
            xn = _rms(h1, g_ref[...]).astype(cdt)
            logits = jnp.dot(xn, wr_ref[...], preferred_element_type=F32, precision=prec) + br_ref[...]
            route_ref[...] = _route(logits)

    if n_tail:
        @pl.when((i == n_main) & (k == pl.num_programs(1) - 1))
        def _():
            for src, dst in zip(refs[6:9], (h1_ref, route_ref, h1x_ref)):
                rows = src.shape[0]
                dst[:rows] = src[...]
                dst[rows:] = jnp.zeros((dst.shape[0] - rows, dst.shape[1]), F32)


def _outproj_router(o, w, h, g_ffn, w_r, b_r, *, tm, tk, hi, tail, name):
    m, kdim = o.shape
    d = w.shape[1]
    n_main, n_k = m // tm, kdim // tk
    n_tail = 0 if tail is None else 3
    rows_i = lambda i: jnp.minimum(i, n_main - 1)
    k_i = lambda i, k: jnp.where(i < n_main, k, n_k - 1)
    in_specs = [pl.BlockSpec((tm, tk), lambda i, k: (rows_i(i), k_i(i, k))),
                pl.BlockSpec((tk, d), lambda i, k: (k_i(i, k), 0)),
                pl.BlockSpec((tm, d), lambda i, k: (rows_i(i), 0)),
                pl.BlockSpec((1, d), lambda i, k: (0, 0)),
                pl.BlockSpec((d, ROUTE_LANES), lambda i, k: (0, 0)),
                pl.BlockSpec((1, ROUTE_LANES), lambda i, k: (0, 0))]
    args = [o, w, h, g_ffn, w_r, b_r]
    total_rows = m
    if tail is not None:
        assert tail[0].shape[0] <= tm
        in_specs += [pl.BlockSpec(t.shape, lambda i, k: (0, 0)) for t in tail]
        args += list(tail)
        total_rows += tail[0].shape[0]
    return pl.pallas_call(
        functools.partial(_outproj_kernel, hi=hi, n_tail=n_tail),
        grid=(n_main + (1 if n_tail else 0), n_k),
        in_specs=in_specs,
        out_specs=[pl.BlockSpec((tm, d), lambda i, k: (i, 0)),
                   pl.BlockSpec((tm, ROUTE_LANES), lambda i, k: (i, 0)),
                   pl.BlockSpec((tm * ROW_PITCH, LANES), lambda i, k: (i, 0))],
        out_shape=[jax.ShapeDtypeStruct((total_rows, d), F32),
                   jax.ShapeDtypeStruct((total_rows, ROUTE_LANES), F32),
                   jax.ShapeDtypeStruct((total_rows * ROW_PITCH, LANES), F32)],
        scratch_shapes=[pltpu.VMEM((tm, d), F32)],
        compiler_params=_cparams(("parallel", "arbitrary")),
        name=name,
    )(*args)


def _meta_kernel(eid_ref, pos_ref, tile_ref):
    eid = eid_ref[...]
    r_i = lax.broadcasted_iota(jnp.int32, (LANES, LANES), 0)
    c_i = lax.broadcasted_iota(jnp.int32, (LANES, LANES), 1)
    upper = (r_i <= c_i).astype(BF16)
    ones = jnp.ones((LANES, LANES), BF16)
    rr = lax.broadcasted_iota(jnp.int32, (PAIR_ROWS, PAIR_ROWS), 0)
    cc = lax.broadcasted_iota(jnp.int32, (PAIR_ROWS, PAIR_ROWS), 1)
    below = (cc < rr).astype(BF16)
    lane = lax.broadcasted_iota(jnp.int32, (1, LANES), 1)

    pos = jnp.zeros((PAIR_ROWS, LANES), F32)
    hit = jnp.zeros((PAIR_ROWS, LANES), F32)
    start = jnp.zeros((1, LANES), F32)
    starts = jnp.zeros((1, LANES), F32)
    for e in range(N_EXPERTS):
        mf = (eid == e).astype(F32)
        mb = mf.astype(BF16)
        incl = jnp.dot(mb, upper, preferred_element_type=F32)
        row_tot = jnp.dot(mb, ones, preferred_element_type=F32)
        row_off = jnp.dot(below, row_tot.astype(BF16), preferred_element_type=F32)
        rank = incl - mf + row_off
        cnt = row_off[PAIR_ROWS - 1:PAIR_ROWS, :] + row_tot[PAIR_ROWS - 1:PAIR_ROWS, :]
        padded = jnp.floor((cnt + (MOE_TILE - 1)) * (1.0 / MOE_TILE)) * MOE_TILE
        pos = pos + mf * (start + rank)
        hit = hit + mf
        start = start + padded
        starts = jnp.where(lane > e, start, starts)
    pos_ref[...] = jnp.where(hit > 0, pos, -1.0).astype(jnp.int32)
    tile_ref[...] = jnp.broadcast_to(starts * (1.0 / MOE_TILE), (8, LANES)).astype(jnp.int32)


def _moe_meta(eid_pairs):
    return pl.pallas_call(
        _meta_kernel,
        out_shape=[jax.ShapeDtypeStruct((PAIR_ROWS, LANES), jnp.int32),
                   jax.ShapeDtypeStruct((8, LANES), jnp.int32)],
        compiler_params=pltpu.CompilerParams(vmem_limit_bytes=VMEM_LIMIT),
        name="moe_meta",
    )(eid_pairs)


INV_CHUNK = 2048


def _invert_kernel(pos_ref, inv_ref, *, n_tokens):
    n_pairs = pos_ref.shape[1]
    hi_acc = jnp.zeros((MAX_TILES, MOE_TILE), F32)
    lo_acc = jnp.zeros((MAX_TILES, MOE_TILE), F32)
    hit_acc = jnp.zeros((MAX_TILES, MOE_TILE), F32)
    t_iota = lax.broadcasted_iota(jnp.int32, (MAX_TILES, INV_CHUNK), 0)
    r_iota = lax.broadcasted_iota(jnp.int32, (MOE_TILE, INV_CHUNK), 0)
    for c in range(n_pairs // INV_CHUNK):
        pos = pos_ref[:, c * INV_CHUNK:(c + 1) * INV_CHUNK]
        p = lax.broadcasted_iota(jnp.int32, (1, INV_CHUNK), 1) + c * INV_CHUNK
        tok = jnp.where(p >= n_tokens, p - n_tokens, p)
        in_tile = (pos >> (MOE_TILE.bit_length() - 1)) == t_iota
        a_hi = jnp.where(in_tile, (tok >> 7).astype(F32), 0.0).astype(BF16)
        a_lo = jnp.where(in_tile, (tok & 127).astype(F32), 0.0).astype(BF16)
        b = ((pos & (MOE_TILE - 1)) == r_iota).astype(BF16)
        dn = (((1,), (1,)), ((), ()))
        hi_acc = hi_acc + lax.dot_general(a_hi, b, dn, preferred_element_type=F32)
        lo_acc = lo_acc + lax.dot_general(a_lo, b, dn, preferred_element_type=F32)
        hit_acc = hit_acc + lax.dot_general(in_tile.astype(F32).astype(BF16), b, dn, preferred_element_type=F32)
    slot = (lax.broadcasted_iota(jnp.int32, (MAX_TILES, MOE_TILE), 0) * MOE_TILE
            + lax.broadcasted_iota(jnp.int32, (MAX_TILES, MOE_TILE), 1))
    spread = slot & ((1 << (n_tokens.bit_length() - 1)) - 1)
    inv_ref[...] = jnp.where(hit_acc > 0, (hi_acc * 128.0 + lo_acc).astype(jnp.int32), spread)


def _moe_invert(pos_row, n_tokens):
    return pl.pallas_call(
        functools.partial(_invert_kernel, n_tokens=n_tokens),
        out_shape=jax.ShapeDtypeStruct((MAX_TILES, MOE_TILE), jnp.int32),
        compiler_params=pltpu.CompilerParams(vmem_limit_bytes=VMEM_LIMIT),
        name="moe_invert",
    )(pos_row)


TILE_ROWS = MOE_TILE * ROW_PITCH
GATHER_DEPTH = 3


def _experts_kernel(starts_ref, inv_hbm, hx_hbm, g_ref, wg_ref, wu_ref, wd_ref, y_hbm,
                    xbuf, ybuf, inv_sm, gsem, ysem, isem, wg_bf, wu_bf, wd_bf, *, n_slots):
    e = pl.program_id(0)
    t0, t1 = starts_ref[e], starts_ref[e + 1]
    n_tiles = starts_ref[N_EXPERTS]
    last = n_tiles - 1

    def inv_copy(t, s):
        return pltpu.make_async_copy(inv_hbm.at[jnp.minimum(t, last)], inv_sm.at[s], isem.at[s])

    def y_copy(t, s):
        return pltpu.make_async_copy(ybuf.at[s], y_hbm.at[pl.ds(t * TILE_ROWS, TILE_ROWS)], ysem.at[s])

    def gather_start(s, r):
        _row_copy(hx_hbm, inv_sm[s, 0, r], xbuf.at[s], r, gsem.at[s]).start()

    @pl.when((e == 0) & (n_tiles > 0))
    def _():
        for s in range(GATHER_DEPTH):
            inv_copy(s, s).start()
        for s in range(GATHER_DEPTH - 1):
            inv_copy(s, s).wait()

            def body(r, carry, s=s):
                gather_start(s, r)
                return carry
            lax.fori_loop(0, MOE_TILE, body, 0, unroll=8)

    @pl.when(t1 > t0)
    def _():
        wg_bf[...] = wg_ref[...].astype(BF16)
        wu_bf[...] = wu_ref[...].astype(BF16)
        wd_bf[...] = wd_ref[...].astype(BF16)

        def tile(t, carry):
            s, s_req, s_y = t % GATHER_DEPTH, (t + GATHER_DEPTH - 1) % GATHER_DEPTH, t % 2
            inv_copy(t + GATHER_DEPTH, s).start()
            inv_copy(t + GATHER_DEPTH - 1, s_req).wait()
            _rows_wait(hx_hbm, xbuf.at[s], MOE_TILE, gsem.at[s])

            @pl.when(t >= 2)
            def _():
                y_copy(t - 2, s_y).wait()

            x = _load_rows(xbuf.at[s], MOE_TILE)
            for r in range(MOE_TILE):
                gather_start(s_req, r)
            xn = _rms(x, g_ref[...]).astype(BF16)
            hg = jnp.dot(xn, wg_bf[...], preferred_element_type=F32)
            hu = jnp.dot(xn, wu_bf[...], preferred_element_type=F32)
            hid = (hg / (1.0 + jnp.exp(-hg)) * hu).astype(BF16)
            _store_rows(ybuf.at[s_y], jnp.dot(hid, wd_bf[...], preferred_element_type=F32))
            y_copy(t, s_y).start()
            return carry

        lax.fori_loop(t0, t1, tile, 0)

    @pl.when(e == pl.num_programs(0) - 1)
    def _():
        @pl.when(n_tiles > 0)
        def _():
            for ahead in range(GATHER_DEPTH - 1):
                s = (n_tiles + ahead) % GATHER_DEPTH
                _rows_wait(hx_hbm, xbuf.at[s], MOE_TILE, gsem.at[s])
            inv_copy(last, last % GATHER_DEPTH).wait()
            y_copy(last, last % 2).wait()

        @pl.when(n_tiles > 1)
        def _():
            y_copy(last - 1, (last - 1) % 2).wait()

        ybuf[0] = jnp.zeros((TILE_ROWS, LANES), F32)

        def fill(t, carry):
            y_copy(t, 0).start()
            y_copy(t, 0).wait()
            return carry

        lax.fori_loop(n_tiles, n_slots, fill, 0)


def _moe_experts(tile_tab, inv, hx_all, g_ffn, w_gate, w_up, w_down, layer, n_steps):
    d = D_MODEL

    def widx(e, tab):
        return (layer, e, 0, 0)

    grid_spec = pltpu.PrefetchScalarGridSpec(
        num_scalar_prefetch=1,
        grid=(N_EXPERTS,),
        in_specs=[pl.BlockSpec(memory_space=pl.ANY),
                  pl.BlockSpec(memory_space=pl.ANY),
                  pl.BlockSpec((1, d), lambda e, tab: (0, 0)),
                  pl.BlockSpec((None, None, d, D_EXPERT), widx),
                  pl.BlockSpec((None, None, d, D_EXPERT), widx),
                  pl.BlockSpec((None, None, D_EXPERT, d), widx)],
        out_specs=pl.BlockSpec(memory_space=pl.ANY),
        scratch_shapes=[pltpu.VMEM((GATHER_DEPTH, TILE_ROWS, LANES), F32),
                        pltpu.VMEM((2, TILE_ROWS, LANES), F32),
                        pltpu.SMEM((GATHER_DEPTH, 1, MOE_TILE), jnp.int32),
                        pltpu.SemaphoreType.DMA((GATHER_DEPTH,)),
                        pltpu.SemaphoreType.DMA((2,)),
                        pltpu.SemaphoreType.DMA((GATHER_DEPTH,)),
                        pltpu.VMEM((d, D_EXPERT), BF16),
                        pltpu.VMEM((d, D_EXPERT), BF16),
                        pltpu.VMEM((D_EXPERT, d), BF16)],
    )
    return pl.pallas_call(
        functools.partial(_experts_kernel, n_slots=n_steps),
        grid_spec=grid_spec,
        out_shape=jax.ShapeDtypeStruct((n_steps * TILE_ROWS, LANES), F32),
        compiler_params=_cparams(("arbitrary",)),
        name="moe_experts",
    )(tile_tab, inv, hx_all, g_ffn, w_gate, w_up, w_down)


PLE_COLS = 512


def _combine_kernel(*refs, tm, n_steps):
    lists = [(refs[k], refs[GATHER_DEPTH + k]) for k in range(GATHER_DEPTH)]
    y_hbm, h_ref, route_ref, p_ref, g_ref, wg_ref, wp_ref, o_ref, ybuf, sems = refs[2 * GATHER_DEPTH:]
    i = pl.program_id(0)
    s, s_req = i % GATHER_DEPTH, (i + GATHER_DEPTH - 1) % GATHER_DEPTH

    def request(p0, p1, sl, r):
        _row_copy(y_hbm, p0[0, 0, r], ybuf.at[sl, 0], r, sems.at[sl]).start()
        _row_copy(y_hbm, p1[0, 0, r], ybuf.at[sl, 1], r, sems.at[sl]).start()

    def wait_tile(sl):
        for c in range(2):
            _rows_wait(y_hbm, ybuf.at[sl, c], tm, sems.at[sl])

    @pl.when(i == 0)
    def _():
        for k in range(GATHER_DEPTH - 1):
            def body(r, carry, k=k):
                request(lists[k][0], lists[k][1], k, r)
                return carry
            lax.fori_loop(0, tm, body, 0, unroll=8)

    wait_tile(s)
    route = route_ref[...]
    y0, y1 = _load_rows(ybuf.at[s, 0], tm), _load_rows(ybuf.at[s, 1], tm)
    for r in range(tm):
        request(lists[-1][0], lists[-1][1], s_req, r)
    h2 = h_ref[...] + (route[:, 2:3] * y0 + route[:, 3:4] * y1)
    xn = _rms(h2, g_ref[...]).astype(BF16)
    pb = p_ref[...].astype(BF16)
    for j in range(h2.shape[1] // PLE_COLS):
        cols = slice(j * PLE_COLS, (j + 1) * PLE_COLS)
        z = jnp.dot(xn, wg_ref[:, cols], preferred_element_type=F32)
        proj = jnp.dot(pb, wp_ref[:, cols], preferred_element_type=F32)
        o_ref[:, cols] = h2[:, cols] + proj * (1.0 / (1.0 + jnp.exp(-z)))

    @pl.when(i == n_steps - 1)
    def _():
        for ahead in range(GATHER_DEPTH - 1):
            wait_tile((n_steps + ahead) % GATHER_DEPTH)


def _moe_combine_ple(pos0, pos1, y_sorted, h_all, route_all, p, g_ple, w_gate, w_proj, *, layer, tm, rows,
                     row_block0, name):
    d = h_all.shape[1]
    n = rows // tm
    last = n - 1
    ahead = [pl.BlockSpec((1, 1, tm), lambda i, k=k: (jnp.minimum(i + k, last), 0, 0), memory_space=pltpu.SMEM)
             for k in range(GATHER_DEPTH)]
    return pl.pallas_call(
        functools.partial(_combine_kernel, tm=tm, n_steps=n),
        grid=(n,),
        in_specs=ahead + ahead + [
                  pl.BlockSpec(memory_space=pl.ANY),
                  pl.BlockSpec((tm, d), lambda i: (row_block0 + i, 0)),
                  pl.BlockSpec((tm, ROUTE_LANES), lambda i: (row_block0 + i, 0)),
                  pl.BlockSpec((None, tm, PLE_DIM), lambda i: (layer, i, 0)),
                  pl.BlockSpec((1, d), lambda i: (0, 0)),
                  pl.BlockSpec((d, d), lambda i: (0, 0)),
                  pl.BlockSpec((PLE_DIM, d), lambda i: (0, 0))],
        out_specs=pl.BlockSpec((tm, d), lambda i: (i, 0)),
        out_shape=jax.ShapeDtypeStruct((rows, d), F32),
        scratch_shapes=[pltpu.VMEM((GATHER_DEPTH, 2, tm * ROW_PITCH, LANES), F32),
                        pltpu.SemaphoreType.DMA((GATHER_DEPTH,))],
        compiler_params=_cparams(("arbitrary",)),
        name=name,
    )(*([pos0] * GATHER_DEPTH + [pos1] * GATHER_DEPTH), y_sorted, h_all, route_all, p, g_ple, w_gate, w_proj)


def _ret_head(q, k, v, gt, state, gn, lg, c_real):
    c = q.shape[0]
    k = k * (RET_DK ** -0.5)
    ri = lax.broadcasted_iota(jnp.int32, (c, c), 0)
    ci = lax.broadcasted_iota(jnp.int32, (c, c), 1)
    diff = (ri - ci).astype(F32)
    decay = jnp.where(diff >= 0, jnp.exp(lg * jnp.maximum(diff, 0.0)), 0.0)
    idx = lax.broadcasted_iota(jnp.int32, (c, 1), 0).astype(F32)
    qk = lax.dot_general(q.astype(BF16), k.astype(BF16), (((1,), (1,)), ((), ())), preferred_element_type=F32)
    inner = qk * decay
    q_dec = q * jnp.exp(lg * (idx + 1.0))
    k_dec = k * jnp.exp(lg * (c_real - 1.0 - idx))
    vb = v.astype(BF16)
    out = (jnp.dot(inner.astype(BF16), vb, preferred_element_type=F32)
           + jnp.dot(q_dec.astype(BF16), state.astype(BF16), preferred_element_type=F32))
    new_state = math.exp(lg * c_real) * state + lax.dot_general(
        k_dec.astype(BF16), vb, (((0,), (0,)), ((), ())), preferred_element_type=F32)
    mu = jnp.mean(out, axis=-1, keepdims=True)
    cen = out - mu
    var = jnp.mean(cen * cen, axis=-1, keepdims=True)
    on = cen * lax.rsqrt(var + GN_EPS) * gn
    return gt / (1.0 + jnp.exp(-gt)) * on, new_state


def _ret_prompt_kernel(q_ref, k_ref, v_ref, g_ref, gn_ref, y_ref, so_ref, st_ref):
    c = pl.program_id(1)

    @pl.when(c == 0)
    def _():
        st_ref[...] = jnp.zeros_like(st_ref)

    for h in range(RET_HEADS):
        ks, vs = slice(h * RET_DK, (h + 1) * RET_DK), slice(h * RET_DV, (h + 1) * RET_DV)
        y, new_state = _ret_head(q_ref[:, ks], k_ref[:, ks], v_ref[:, vs], g_ref[:, vs], st_ref[h],
                                 gn_ref[:, vs], RET_LOG_GAMMA[h], RET_CHUNK)
        st_ref[h] = new_state
        y_ref[:, vs] = y.astype(y_ref.dtype)

    @pl.when(c == pl.num_programs(1) - 1)
    def _():
        so_ref[0] = st_ref[...]


def _ret_prompt(qkvg, g_norm, batch):
    rows = qkvg.shape[0]
    nc = rows // batch // RET_CHUNK
    return pl.pallas_call(
        _ret_prompt_kernel,
        grid=(batch, nc),
        in_specs=[pl.BlockSpec((RET_CHUNK, RET_K_WIDTH), lambda b, c: (b * nc + c, 0)),
                  pl.BlockSpec((RET_CHUNK, RET_K_WIDTH), lambda b, c: (b * nc + c, 1)),
                  pl.BlockSpec((RET_CHUNK, RET_V_WIDTH), lambda b, c: (b * nc + c, 1)),
                  pl.BlockSpec((RET_CHUNK, RET_V_WIDTH), lambda b, c: (b * nc + c, 2)),
                  pl.BlockSpec((1, RET_V_WIDTH), lambda b, c: (0, 0))],
        out_specs=[pl.BlockSpec((RET_CHUNK, RET_V_WIDTH), lambda b, c: (b * nc + c, 0)),
                   pl.BlockSpec((1, RET_HEADS, RET_DK, RET_DV), lambda b, c: (b, 0, 0, 0))],
        out_shape=[jax.ShapeDtypeStruct((rows, RET_V_WIDTH), BF16),
                   jax.ShapeDtypeStruct((batch, RET_HEADS, RET_DK, RET_DV), F32)],
        scratch_shapes=[pltpu.VMEM((RET_HEADS, RET_DK, RET_DV), F32)],
        compiler_params=_cparams(("parallel", "arbitrary")),
        name="ret_prompt",
    )(qkvg, qkvg, qkvg, qkvg, g_norm)


def _ret_decode_kernel(x_ref, s_ref, gn_ref, y_ref, so_ref, *, t_new):
    x = x_ref[0]
    for h in range(RET_HEADS):
        q = x[:, h * RET_DK:(h + 1) * RET_DK]
        k = x[:, RET_K_WIDTH + h * RET_DK:RET_K_WIDTH + (h + 1) * RET_DK]
        v0 = 2 * RET_K_WIDTH + h * RET_DV
        g0 = 2 * RET_K_WIDTH + RET_V_WIDTH + h * RET_DV
        y, new_state = _ret_head(q, k, x[:, v0:v0 + RET_DV], x[:, g0:g0 + RET_DV], s_ref[0, h],
                                 gn_ref[:, h * RET_DV:(h + 1) * RET_DV], RET_LOG_GAMMA[h], t_new)
        so_ref[0, h] = new_state
        y_ref[0, :, h * RET_DV:(h + 1) * RET_DV] = y


def _ret_decode(qkvg, state, g_norm, t_new):
    b = qkvg.shape[0]
    return pl.pallas_call(
        functools.partial(_ret_decode_kernel, t_new=t_new),
        grid=(b,),
        in_specs=[pl.BlockSpec((1, DEC_ROWS, RET_IN_WIDTH), lambda i: (i, 0, 0)),
                  pl.BlockSpec((1, RET_HEADS, RET_DK, RET_DV), lambda i: (i, 0, 0, 0)),
                  pl.BlockSpec((1, RET_V_WIDTH), lambda i: (0, 0))],
        out_specs=[pl.BlockSpec((1, DEC_ROWS, RET_V_WIDTH), lambda i: (i, 0, 0)),
                   pl.BlockSpec((1, RET_HEADS, RET_DK, RET_DV), lambda i: (i, 0, 0, 0))],
        out_shape=[jax.ShapeDtypeStruct((b, DEC_ROWS, RET_V_WIDTH), F32),
                   jax.ShapeDtypeStruct((b, RET_HEADS, RET_DK, RET_DV), F32)],
        compiler_params=_cparams(("parallel",)),
        name="ret_decode",
    )(qkvg, state, g_norm)


def _pad_rows(x, b, t):
    return jnp.pad(x.reshape(b, t, x.shape[-1]), ((0, 0), (0, DEC_ROWS - t), (0, 0)))


def _moe_layer(h_all, route_all, hx_all, p_prompt, p_sample, g_ffn, g_ple, w_gate_e, w_up_e, w_down_e, w_ple_gate,
               w_ple_proj, layer, n_prompt, n_sample):
    n_tok = n_prompt + n_sample
    n_pairs = PAIR_ROWS * LANES
    eid = route_all[:, :2].astype(jnp.int32).T.reshape(-1)
    eid = jnp.pad(eid, (0, n_pairs - 2 * n_tok), constant_values=-1).reshape(PAIR_ROWS, LANES)
    pos, tile_tab = _moe_meta(eid)
    inv = _moe_invert(pos.reshape(1, n_pairs), n_tok).reshape(MAX_TILES, 1, MOE_TILE)
    max_tiles = (2 * n_tok + N_EXPERTS * (MOE_TILE - 1)) // MOE_TILE
    y_sorted = _moe_experts(tile_tab[0], inv, hx_all, g_ffn, w_gate_e, w_up_e, w_down_e, layer, max_tiles)
    pos2 = pos.reshape(-1)[:2 * n_tok].reshape(2, n_tok)
    tm_p, tm_s = 256, n_sample
    outs = []
    for lo, rows, tm, p, nm in ((0, n_prompt, tm_p, p_prompt, "combine_prompt"),
                                (n_prompt, n_sample, tm_s, p_sample, "combine_sample")):
        pos0 = pos2[0, lo:lo + rows].reshape(rows // tm, 1, tm)
        pos1 = pos2[1, lo:lo + rows].reshape(rows // tm, 1, tm)
        outs.append(_moe_combine_ple(pos0, pos1, y_sorted, h_all, route_all, p, g_ple, w_ple_gate, w_ple_proj,
                                     layer=layer, tm=tm, rows=rows, row_block0=lo // tm, name=nm))
    return outs


def kernel(x_prompt, x_sample, cache_k_swa, cache_v_swa, state_ret, p_prompt, p_sample, g_mix, g_ffn, g_ple,
           swa_w_qkv, swa_g_q, swa_g_k, swa_sinks, swa_w_o, ret_w_in, ret_g_norm, ret_w_o, moe_w_group,
           moe_b_group, moe_w_expert, moe_b_expert, moe_w_gate, moe_w_up, moe_w_down, ple_w_proj, ple_w_gate):
    batch, seq, d = x_prompt.shape
    dec_b, dec_t, _ = x_sample.shape
    n_prompt, n_sample = batch * seq, dec_b * dec_t
    n_tok = n_prompt + n_sample
    assert 2 * n_tok <= PAIR_ROWS * LANES and 2 * n_tok + N_EXPERTS * (MOE_TILE - 1) <= (MAX_TILES - 1) * MOE_TILE
    assert dec_t <= DEC_ROWS and n_prompt % 1024 == 0 and n_sample % 8 == 0

    xp = x_prompt.reshape(n_prompt, d)
    xs = x_sample.reshape(n_sample, d)
    pp = p_prompt.reshape(DEPTH, n_prompt, PLE_DIM)
    ps = p_sample.reshape(DEPTH, n_sample, PLE_DIM)
    row = lambda a: a.reshape(1, -1)

    def router_w(i):
        w = jnp.concatenate([moe_w_group[i], moe_w_expert[i]], axis=1)
        b = jnp.concatenate([moe_b_group[i], moe_b_expert[i]])
        pad = ROUTE_LANES - w.shape[1]
        return jnp.pad(w, ((0, 0), (0, pad))), jnp.pad(b, (0, pad)).reshape(1, ROUTE_LANES)

    g_q2, g_k2 = row(jnp.tile(swa_g_q[0], 2)), row(jnp.tile(swa_g_k[0], 2))
    w_r, b_r = router_w(0)
    qkv_p = _norm_proj(xp, row(g_mix[0]), swa_w_qkv[0].astype(BF16), tm=1024, tn=512, hi=False, name="qkv_prompt")
    sinks2 = swa_sinks[0] * LOG2E
    o_p, k_p, v_p = _swa_prompt(qkv_p, g_q2, g_k2, sinks2, batch)

    qkv_s = _norm_proj(xs, row(g_mix[0]), swa_w_qkv[0], tm=n_sample, tn=512, hi=True, name="qkv_sample")
    ck = cache_k_swa[0].reshape(dec_b, WINDOW, KV_WIDTH)
    cv = cache_v_swa[0].reshape(dec_b, WINDOW, KV_WIDTH)
    o_s, kn_s = _swa_decode(_pad_rows(qkv_s, dec_b, dec_t), ck, cv, g_q2, g_k2, sinks2, dec_t)
    o_s = o_s[:, :dec_t].reshape(n_sample, Q_WIDTH)
    tail = _outproj_router(o_s, swa_w_o[0], xs, row(g_ffn[0]), w_r, b_r, tm=n_sample, tk=2048, hi=True,
                           tail=None, name="swa_out_sample")
    h_all, route_all, hx_all = _outproj_router(o_p, swa_w_o[0].astype(BF16), xp, row(g_ffn[0]), w_r.astype(BF16),
                                               b_r, tm=512, tk=2048, hi=False, tail=tail, name="swa_out_prompt")
    k_s = jnp.concatenate([ck[:, dec_t:], kn_s[:, :dec_t]], axis=1)
    v_new = qkv_s[:, Q_WIDTH + KV_WIDTH:].reshape(dec_b, dec_t, KV_WIDTH)
    v_s = jnp.concatenate([cv[:, dec_t:], v_new], axis=1)

    h_p, h_s = _moe_layer(h_all, route_all, hx_all, pp, ps, row(g_ffn[0]), row(g_ple[0]), moe_w_gate, moe_w_up,
                          moe_w_down, ple_w_gate[0].astype(BF16), ple_w_proj[0].astype(BF16), 0, n_prompt, n_sample)

    w_in = ret_w_in[0].astype(BF16)
    w_o = ret_w_o[0].astype(BF16)
    w_r, b_r = router_w(1)
    w_r = w_r.astype(BF16)
    qkvg_p = _norm_proj(h_p, row(g_mix[1]), w_in, tm=1024, tn=512, hi=False, name="ret_in_prompt")
    y_p, s_p = _ret_prompt(qkvg_p, row(ret_g_norm[0]), batch)

    qkvg_s = _norm_proj(h_s, row(g_mix[1]), w_in, tm=n_sample, tn=512, hi=False, name="ret_in_sample")
    y_s, s_s = _ret_decode(_pad_rows(qkvg_s, dec_b, dec_t), state_ret[0], row(ret_g_norm[0]), dec_t)
    y_s = y_s[:, :dec_t].reshape(n_sample, RET_V_WIDTH)
    tail = _outproj_router(y_s, w_o, h_s, row(g_ffn[1]), w_r, b_r, tm=n_sample, tk=2048, hi=False, tail=None,
                           name="ret_out_sample")
    h_all, route_all, hx_all = _outproj_router(y_p, w_o, h_p, row(g_ffn[1]), w_r, b_r, tm=512, tk=2048, hi=False,
                                               tail=tail, name="ret_out_prompt")

    y_prompt, y_sample = _moe_layer(h_all, route_all, hx_all, pp, ps, row(g_ffn[1]), row(g_ple[1]), moe_w_gate,
                                    moe_w_up, moe_w_down, ple_w_gate[1].astype(BF16), ple_w_proj[1].astype(BF16), 1,
                                    n_prompt, n_sample)

    kv_shape = (1, -1, WINDOW, SWA_KV_HEADS, SWA_HEAD_DIM)
    return (y_prompt.reshape(batch, seq, d), y_sample.reshape(dec_b, dec_t, d),
            k_p.reshape(kv_shape), v_p.reshape(kv_shape), s_p[None],
            k_s.reshape(kv_shape), v_s.reshape(kv_shape), s_s[None])
```

```python
import functools
import math

import jax
import jax.numpy as jnp
from jax import lax
from jax.experimental import pallas as pl
from jax.experimental.pallas import tpu as pltpu

D_MODEL = 2048
DEPTH = 2
SWA_HEADS = 32
SWA_KV_HEADS = 8
SWA_HEAD_DIM = 64
Q_WIDTH = SWA_HEADS * SWA_HEAD_DIM
KV_WIDTH = SWA_KV_HEADS * SWA_HEAD_DIM
QKV_WIDTH = Q_WIDTH + 2 * KV_WIDTH
WINDOW = 128
ATTN_SCALE = SWA_HEAD_DIM ** -0.5
LOG2E = math.log2(math.e)
RET_HEADS = 8
RET_DK = D_MODEL // RET_HEADS
RET_DV = 2 * D_MODEL // RET_HEADS
RET_K_WIDTH = RET_HEADS * RET_DK
RET_V_WIDTH = RET_HEADS * RET_DV
RET_IN_WIDTH = 2 * RET_K_WIDTH + 2 * RET_V_WIDTH
RET_CHUNK = 128
N_GROUPS = 4
EXPERTS_PER_GROUP = 8
N_EXPERTS = N_GROUPS * EXPERTS_PER_GROUP
D_EXPERT = D_MODEL // 4
PLE_DIM = 256
NORM_EPS = 1e-6
GN_EPS = 1e-5
NEG_INF = -1e30

LANES = 128
ROUTE_LANES = LANES
EXPERT_LANE0 = N_GROUPS
MOE_TILE = 256
PAIR_ROWS = 256
MAX_TILES = LANES
VMEM_LIMIT = 56 * 1024 * 1024
ROW_TILES = D_MODEL // LANES
ROW_PITCH = ROW_TILES + 4

F32 = jnp.float32
BF16 = jnp.bfloat16
HIGHEST = lax.Precision.HIGHEST

ALIBI_SLOPES = tuple(2.0 ** (-8.0 * (h + 1) / SWA_HEADS) for h in range(SWA_HEADS))
RET_LOG_GAMMA = tuple(math.log(1.0 - 2.0 ** (-5.0 - h)) for h in range(RET_HEADS))


def _cparams(sem):
    return pltpu.CompilerParams(dimension_semantics=sem, vmem_limit_bytes=VMEM_LIMIT)


def _mm_dtype(hi):
    return F32 if hi else BF16


def _mm_prec(hi):
    return HIGHEST if hi else None


def _rms(x, g):
    return x * lax.rsqrt(jnp.mean(x * x, axis=-1, keepdims=True) + NORM_EPS) * g


def _store_rows(ref, x):
    n = x.shape[0]
    for s in range(ROW_PITCH):
        piece = x[:, s * LANES:(s + 1) * LANES] if s < ROW_TILES else jnp.zeros((n, LANES), F32)
        ref[pl.ds(s, n, stride=ROW_PITCH), :] = piece


def _load_rows(ref, n):
    return jnp.concatenate([ref[pl.ds(s, n, stride=ROW_PITCH), :] for s in range(ROW_TILES)], axis=1)


def _row_copy(src_hbm, src_row, dst, dst_row, sem):
    return pltpu.make_async_copy(src_hbm.at[pl.ds(src_row * ROW_PITCH, ROW_TILES)],
                                 dst.at[pl.ds(dst_row * ROW_PITCH, ROW_TILES)], sem)


def _rows_wait(src_hbm, dst, n, sem):
    pltpu.make_async_copy(src_hbm.at[pl.ds(0, n * ROW_TILES)], dst.at[pl.ds(0, n * ROW_TILES)], sem).wait()


def _proj_kernel(x_ref, g_ref, w_ref, o_ref, a_ref, *, hi):
    @pl.when(pl.program_id(1) == 0)
    def _():
        a_ref[...] = _rms(x_ref[...], g_ref[...]).astype(a_ref.dtype)

    o_ref[...] = jnp.dot(a_ref[...], w_ref[...].astype(a_ref.dtype), preferred_element_type=F32,
                         precision=_mm_prec(hi)).astype(o_ref.dtype)


def _norm_proj(x, g, w, *, tm, tn, hi, name):
    m, k = x.shape
    n = w.shape[1]
    return pl.pallas_call(
        functools.partial(_proj_kernel, hi=hi),
        grid=(m // tm, n // tn),
        in_specs=[pl.BlockSpec((tm, k), lambda i, j: (i, 0)),
                  pl.BlockSpec((1, k), lambda i, j: (0, 0)),
                  pl.BlockSpec((k, tn), lambda i, j: (0, j))],
        out_specs=pl.BlockSpec((tm, tn), lambda i, j: (i, j)),
        out_shape=jax.ShapeDtypeStruct((m, n), F32),
        scratch_shapes=[pltpu.VMEM((tm, k), _mm_dtype(hi))],
        compiler_params=_cparams(("parallel", "arbitrary")),
        name=name,
    )(x, g, w)


def _head_norm(x, g2):
    lane = lax.broadcasted_iota(jnp.int32, (1, LANES), 1)
    lo = lane < SWA_HEAD_DIM
    outs = []
    for j in range(x.shape[1] // LANES):
        xb = x[:, j * LANES:(j + 1) * LANES]
        x2 = xb * xb
        s_lo = jnp.sum(jnp.where(lo, x2, 0.0), axis=-1, keepdims=True)
        s_hi = jnp.sum(jnp.where(lo, 0.0, x2), axis=-1, keepdims=True)
        r = jnp.where(lo, lax.rsqrt(s_lo * (1.0 / SWA_HEAD_DIM) + NORM_EPS),
                      lax.rsqrt(s_hi * (1.0 / SWA_HEAD_DIM) + NORM_EPS))
        outs.append(xb * r * g2)
    return jnp.concatenate(outs, axis=1)


def _mm3(a, b, dims):
    def dot(x, y):
        return lax.dot_general(x, y, (dims, ((), ())), preferred_element_type=F32)
    a_hi, b_hi = a.astype(BF16), b.astype(BF16)
    a_lo = (a - a_hi.astype(F32)).astype(BF16)
    b_lo = (b - b_hi.astype(F32)).astype(BF16)
    return dot(a_hi, b_hi) + (dot(a_hi, b_lo) + dot(a_lo, b_hi))


def _attn_heads(q, g_q2, kn, v, bias_ref, sink_ref, hi, emit):
    nq, nk = q.shape[0], kn.shape[0]
    cdt = _mm_dtype(hi)
    lane = lax.broadcasted_iota(jnp.int32, (1, LANES), 1)
    lo = lane < SWA_HEAD_DIM
    group = SWA_HEADS // SWA_KV_HEADS
    ones_bd = (lo == (lax.broadcasted_iota(jnp.int32, (2 * nk, LANES), 0) < nk)).astype(cdt)
    for m in range(KV_WIDTH // LANES):
        kb = kn[:, m * LANES:(m + 1) * LANES]
        vb = v[:, m * LANES:(m + 1) * LANES]
        kr = pltpu.roll(kb, SWA_HEAD_DIM, 1)
        vr = pltpu.roll(vb, SWA_HEAD_DIM, 1)
        for half in range(2):
            kv = 2 * m + half
            k_lo, k_hi = (kb, kr) if half == 0 else (kr, kb)
            v_lo, v_hi = (vb, vr) if half == 0 else (vr, vb)
            kbd = jnp.concatenate([jnp.where(lo, k_lo, 0.0), jnp.where(lo, 0.0, k_hi)], axis=0).astype(cdt)
            vbd = jnp.concatenate([jnp.where(lo, v_lo, 0.0), jnp.where(lo, 0.0, v_hi)], axis=0).astype(cdt)
            c0 = kv * group * SWA_HEAD_DIM
            qs = _head_norm(q[:, c0:c0 + 2 * LANES], g_q2) * (ATTN_SCALE * LOG2E)
            q2 = jnp.concatenate([qs[:, :LANES], qs[:, LANES:]], axis=0).astype(cdt)
            if hi:
                s_all = _mm3(q2, kbd, ((1,), (1,)))
            else:
                s_all = lax.dot_general(q2, kbd, (((1,), (1,)), ((), ())), preferred_element_type=F32)
            e_rows, sink_e = [], []
            for r in range(2):
                e_cols = []
                for c in range(2):
                    h = kv * group + 2 * r + c
                    s = s_all[r * nq:(r + 1) * nq, c * nk:(c + 1) * nk] + bias_ref[h]
                    sk = sink_ref[h]
                    mx = jnp.maximum(jnp.max(s, axis=-1, keepdims=True), sk)
                    e_cols.append(jnp.exp2(s - mx).astype(cdt))
                    sink_e.append(jnp.exp2(sk - mx))
                e_rows.append(jnp.concatenate(e_cols, axis=1))
            p2 = jnp.concatenate(e_rows, axis=0)
            o2 = _mm3(p2, vbd, ((1,), (0,))) if hi else jnp.dot(p2, vbd, preferred_element_type=F32)
            if hi:
                sums = [jnp.sum(p2[r * nq:(r + 1) * nq, c * nk:(c + 1) * nk], axis=-1, keepdims=True)
                        for r in range(2) for c in range(2)]
            else:
                den2 = jnp.dot(p2, ones_bd, preferred_element_type=F32)
            for r in range(2):
                sums_r = jnp.where(lo, sums[2 * r], sums[2 * r + 1]) if hi else den2[r * nq:(r + 1) * nq]
                den = sums_r + jnp.where(lo, sink_e[2 * r], sink_e[2 * r + 1])
                emit(2 * kv + r, o2[r * nq:(r + 1) * nq] / den)


def _swa_bias(nq, q_pos0, n_keys, first_key):
    t = jnp.arange(nq)[:, None] + q_pos0
    s = jnp.arange(2 * WINDOW)[None, :]
    dist = t - s
    valid = (dist >= 0) & (dist <= WINDOW) & (s >= first_key) & (s < n_keys)
    slopes = jnp.asarray(ALIBI_SLOPES, F32)[:, None, None]
    return jnp.where(valid[None], -slopes * dist.astype(F32)[None] * LOG2E, NEG_INF)


def _swa_prompt_kernel(sink_ref, q_ref, k_ref, v_ref, gq_ref, gk_ref, bias_ref, o_ref, ko_ref, vo_ref, kprev, vprev):
    n = pl.program_id(1)

    @pl.when(n == 0)
    def _():
        kprev[...] = jnp.zeros_like(kprev)
        vprev[...] = jnp.zeros_like(vprev)

    kn = _head_norm(k_ref[...], gk_ref[...])
    v = v_ref[...]
    kcat = jnp.concatenate([kprev[...], kn], axis=0)
    vcat = jnp.concatenate([vprev[...], v], axis=0)

    def emit(j, block):
        o_ref[:, j * LANES:(j + 1) * LANES] = block.astype(o_ref.dtype)

    _attn_heads(q_ref[...], gq_ref[...], kcat, vcat, bias_ref, sink_ref, False, emit)
    kprev[...] = kn
    vprev[...] = v

    @pl.when(n == pl.num_programs(1) - 1)
    def _():
        ko_ref[0] = kn
        vo_ref[0] = v


def _swa_prompt(qkv, g_q, g_k, sinks, batch):
    rows = qkv.shape[0]
    nb = rows // batch // WINDOW
    kblk = Q_WIDTH // KV_WIDTH
    bias = jnp.stack([_swa_bias(WINDOW, WINDOW, 2 * WINDOW, WINDOW), _swa_bias(WINDOW, WINDOW, 2 * WINDOW, 0)])
    return pl.pallas_call(
        _swa_prompt_kernel,
        grid=(batch, nb),
        in_specs=[pl.BlockSpec(memory_space=pltpu.SMEM),
                  pl.BlockSpec((WINDOW, Q_WIDTH), lambda b, n: (b * nb + n, 0)),
                  pl.BlockSpec((WINDOW, KV_WIDTH), lambda b, n: (b * nb + n, kblk)),
                  pl.BlockSpec((WINDOW, KV_WIDTH), lambda b, n: (b * nb + n, kblk + 1)),
                  pl.BlockSpec((1, LANES), lambda b, n: (0, 0)),
                  pl.BlockSpec((1, LANES), lambda b, n: (0, 0)),
                  pl.BlockSpec((None, SWA_HEADS, WINDOW, 2 * WINDOW), lambda b, n: (jnp.minimum(n, 1), 0, 0, 0))],
        out_specs=[pl.BlockSpec((WINDOW, Q_WIDTH), lambda b, n: (b * nb + n, 0)),
                   pl.BlockSpec((1, WINDOW, KV_WIDTH), lambda b, n: (b, 0, 0)),
                   pl.BlockSpec((1, WINDOW, KV_WIDTH), lambda b, n: (b, 0, 0))],
        out_shape=[jax.ShapeDtypeStruct((rows, Q_WIDTH), BF16),
                   jax.ShapeDtypeStruct((batch, WINDOW, KV_WIDTH), F32),
                   jax.ShapeDtypeStruct((batch, WINDOW, KV_WIDTH), F32)],
        scratch_shapes=[pltpu.VMEM((WINDOW, KV_WIDTH), F32), pltpu.VMEM((WINDOW, KV_WIDTH), F32)],
        compiler_params=_cparams(("parallel", "arbitrary")),
        name="swa_prompt",
    )(sinks, qkv, qkv, qkv, g_q, g_k, bias)


DEC_ROWS = 8
DEC_GROUP = 4


def _swa_decode_kernel(sink_ref, qkv_ref, ck_ref, cv_ref, gq_ref, gk_ref, bias_ref, o_ref, ko_ref, vo_ref, *, t_new):
    fill = jnp.zeros((WINDOW - DEC_ROWS, KV_WIDTH), F32)
    is_new = lax.broadcasted_iota(jnp.int32, (WINDOW, 1), 0) >= WINDOW - t_new

    def slide(cache, new):
        new_at_end = pltpu.roll(jnp.concatenate([new, fill], axis=0), WINDOW - t_new, 0)
        return jnp.where(is_new, new_at_end, pltpu.roll(cache, WINDOW - t_new, 0))

    for b in range(qkv_ref.shape[0]):
        x = qkv_ref[b]
        kn = _head_norm(x[:, Q_WIDTH:Q_WIDTH + KV_WIDTH], gk_ref[...])
        v = x[:, Q_WIDTH + KV_WIDTH:]
        ck, cv = ck_ref[b], cv_ref[b]
        kcat = jnp.concatenate([ck, kn, fill], axis=0)
        vcat = jnp.concatenate([cv, v, fill], axis=0)

        def emit(j, block, b=b):
            o_ref[b, :, j * LANES:(j + 1) * LANES] = block

        _attn_heads(x[:, :Q_WIDTH], gq_ref[...], kcat, vcat, bias_ref, sink_ref, True, emit)
        ko_ref[b] = slide(ck, kn)
        vo_ref[b] = slide(cv, v)


def _swa_decode(qkv, cache_k, cache_v, g_q, g_k, sinks, t_new):
    b = qkv.shape[0]
    g = DEC_GROUP if b % DEC_GROUP == 0 else 1
    bias = _swa_bias(DEC_ROWS, WINDOW, WINDOW + t_new, 0)
    return pl.pallas_call(
        functools.partial(_swa_decode_kernel, t_new=t_new),
        grid=(b // g,),
        in_specs=[pl.BlockSpec(memory_space=pltpu.SMEM),
                  pl.BlockSpec((g, DEC_ROWS, QKV_WIDTH), lambda i: (i, 0, 0)),
                  pl.BlockSpec((g, WINDOW, KV_WIDTH), lambda i: (i, 0, 0)),
                  pl.BlockSpec((g, WINDOW, KV_WIDTH), lambda i: (i, 0, 0)),
                  pl.BlockSpec((1, LANES), lambda i: (0, 0)),
                  pl.BlockSpec((1, LANES), lambda i: (0, 0)),
                  pl.BlockSpec((SWA_HEADS, DEC_ROWS, 2 * WINDOW), lambda i: (0, 0, 0))],
        out_specs=[pl.BlockSpec((g, DEC_ROWS, Q_WIDTH), lambda i: (i, 0, 0)),
                   pl.BlockSpec((g, WINDOW, KV_WIDTH), lambda i: (i, 0, 0)),
                   pl.BlockSpec((g, WINDOW, KV_WIDTH), lambda i: (i, 0, 0))],
        out_shape=[jax.ShapeDtypeStruct((b, DEC_ROWS, Q_WIDTH), F32),
                   jax.ShapeDtypeStruct((b, WINDOW, KV_WIDTH), F32),
                   jax.ShapeDtypeStruct((b, WINDOW, KV_WIDTH), F32)],
        compiler_params=_cparams(("parallel",)),
        name="swa_decode",
    )(sinks, qkv, cache_k, cache_v, g_q, g_k, bias)


def _route(logits):
    lane = lax.broadcasted_iota(jnp.int32, logits.shape, 1).astype(F32)
    big = float(ROUTE_LANES)
    lg = jnp.where(lane < N_GROUPS, logits, NEG_INF)
    mg = jnp.max(lg, axis=-1, keepdims=True)
    gsel = jnp.min(jnp.where(lg == mg, lane, big), axis=-1, keepdims=True)
    pg_sel = 1.0 / jnp.sum(jnp.where(lane < N_GROUPS, jnp.exp(logits - mg), 0.0), axis=-1, keepdims=True)
    first = EXPERT_LANE0 + EXPERTS_PER_GROUP * gsel
    le = jnp.where((lane >= first) & (lane < first + EXPERTS_PER_GROUP), logits, NEG_INF)
    m1 = jnp.max(le, axis=-1, keepdims=True)
    i1 = jnp.min(jnp.where(le == m1, lane, big), axis=-1, keepdims=True)
    le2 = jnp.where(lane == i1, NEG_INF, le)
    m2 = jnp.max(le2, axis=-1, keepdims=True)
    i2 = jnp.min(jnp.where(le2 == m2, lane, big), axis=-1, keepdims=True)
    t = jnp.exp(m2 - m1)
    w0 = pg_sel / (1.0 + t)
    w1 = pg_sel * t / (1.0 + t)
    e0 = i1 - EXPERT_LANE0
    e1 = i2 - EXPERT_LANE0
    return jnp.where(lane == 0, e0, jnp.where(lane == 1, e1, jnp.where(lane == 2, w0, jnp.where(lane == 3, w1, 0.0))))


def _outproj_kernel(*refs, hi, n_tail):
    o_ref, w_ref, h_ref, g_ref, wr_ref, br_ref = refs[:6]
    h1_ref, route_ref, h1x_ref, acc_ref = refs[6 + n_tail:]
    i, k = pl.program_id(0), pl.program_id(1)
    n_main = pl.num_programs(0) - (1 if n_tail else 0)
    cdt, prec = _mm_dtype(hi), _mm_prec(hi)

    @pl.when(i < n_main)
    def _():
        @pl.when(k == 0)
        def _():
            acc_ref[...] = h_ref[...]

        acc_ref[...] += jnp.dot(o_ref[...].astype(cdt), w_ref[...], preferred_element_type=F32, precision=prec)

        @pl.when(k == pl.num_programs(1) - 1)
        def _():
            h1 = acc_ref[...]
            h1_ref[...] = h1
            _store_rows(h1x_ref, h1)
            xn = _rms(h1, g_ref[...]).astype(cdt)
            logits = jnp.dot(xn, wr_ref[...], preferred_element_type=F32, precision=prec) + br_ref[...]
            route_ref[...] = _route(logits)

    if n_tail:
        @pl.when((i == n_main) & (k == pl.num_programs(1) - 1))
        def _():
            for src, dst in zip(refs[6:9], (h1_ref, route_ref, h1x_ref)):
                rows = src.shape[0]
                dst[:rows] = src[...]
                dst[rows:] = jnp.zeros((dst.shape[0] - rows, dst.shape[1]), F32)


def _outproj_router(o, w, h, g_ffn, w_r, b_r, *, tm, tk, hi, tail, name):
    m, kdim = o.shape
    d = w.shape[1]
    n_main, n_k = m // tm, kdim // tk
    n_tail = 0 if tail is None else 3
    rows_i = lambda i: jnp.minimum(i, n_main - 1)
    k_i = lambda i, k: jnp.where(i < n_main, k, n_k - 1)
    in_specs = [pl.BlockSpec((tm, tk), lambda i, k: (rows_i(i), k_i(i, k))),
                pl.BlockSpec((tk, d), lambda i, k: (k_i(i, k), 0)),
                pl.BlockSpec((tm, d), lambda i, k: (rows_i(i), 0)),
                pl.BlockSpec((1, d), lambda i, k: (0, 0)),
                pl.BlockSpec((d, ROUTE_LANES), lambda i, k: (0, 0)),
                pl.BlockSpec((1, ROUTE_LANES), lambda i, k: (0, 0))]
    args = [o, w, h, g_ffn, w_r, b_r]
    total_rows = m
    if tail is not None:
        assert tail[0].shape[0] <= tm
        in_specs += [pl.BlockSpec(t.shape, lambda i, k: (0, 0)) for t in tail]
        args += list(tail)
        total_rows += tail[0].shape[0]
    return pl.pallas_call(
        functools.partial(_outproj_kernel, hi=hi, n_tail=n_tail),
        grid=(n_main + (1 if n_tail else 0), n_k),
        in_specs=in_specs,
        out_specs=[pl.BlockSpec((tm, d), lambda i, k: (i, 0)),
                   pl.BlockSpec((tm, ROUTE_LANES), lambda i, k: (i, 0)),
                   pl.BlockSpec((tm * ROW_PITCH, LANES), lambda i, k: (i, 0))],
        out_shape=[jax.ShapeDtypeStruct((total_rows, d), F32),
                   jax.ShapeDtypeStruct((total_rows, ROUTE_LANES), F32),
                   jax.ShapeDtypeStruct((total_rows * ROW_PITCH, LANES), F32)],
        scratch_shapes=[pltpu.VMEM((tm, d), F32)],
        compiler_params=_cparams(("parallel", "arbitrary")),
        name=name,
    )(*args)


def _meta_kernel(eid_ref, pos_ref, tile_ref):
    eid = eid_ref[...]
    r_i = lax.broadcasted_iota(jnp.int32, (LANES, LANES), 0)
    c_i = lax.broadcasted_iota(jnp.int32, (LANES, LANES), 1)
    upper = (r_i <= c_i).astype(BF16)
    ones = jnp.ones((LANES, LANES), BF16)
    rr = lax.broadcasted_iota(jnp.int32, (PAIR_ROWS, PAIR_ROWS), 0)
    cc = lax.broadcasted_iota(jnp.int32, (PAIR_ROWS, PAIR_ROWS), 1)
    below = (cc < rr).astype(BF16)
    lane = lax.broadcasted_iota(jnp.int32, (1, LANES), 1)

    pos = jnp.zeros((PAIR_ROWS, LANES), F32)
    hit = jnp.zeros((PAIR_ROWS, LANES), F32)
    start = jnp.zeros((1, LANES), F32)
    starts = jnp.zeros((1, LANES), F32)
    for e in range(N_EXPERTS):
        mf = (eid == e).astype(F32)
        mb = mf.astype(BF16)
        incl = jnp.dot(mb, upper, preferred_element_type=F32)
        row_tot = jnp.dot(mb, ones, preferred_element_type=F32)
        row_off = jnp.dot(below, row_tot.astype(BF16), preferred_element_type=F32)
        rank = incl - mf + row_off
        cnt = row_off[PAIR_ROWS - 1:PAIR_ROWS, :] + row_tot[PAIR_ROWS - 1:PAIR_ROWS, :]
        padded = jnp.floor((cnt + (MOE_TILE - 1)) * (1.0 / MOE_TILE)) * MOE_TILE
        pos = pos + mf * (start + rank)
        hit = hit + mf
        start = start + padded
        starts = jnp.where(lane > e, start, starts)
    pos_ref[...] = jnp.where(hit > 0, pos, -1.0).astype(jnp.int32)
    tile_ref[...] = jnp.broadcast_to(starts * (1.0 / MOE_TILE), (8, LANES)).astype(jnp.int32)


def _moe_meta(eid_pairs):
    return pl.pallas_call(
        _meta_kernel,
        out_shape=[jax.ShapeDtypeStruct((PAIR_ROWS, LANES), jnp.int32),
                   jax.ShapeDtypeStruct((8, LANES), jnp.int32)],
        compiler_params=pltpu.CompilerParams(vmem_limit_bytes=VMEM_LIMIT),
        name="moe_meta",
    )(eid_pairs)


INV_CHUNK = 2048


def _invert_kernel(pos_ref, inv_ref, *, n_tokens):
    n_pairs = pos_ref.shape[1]
    hi_acc = jnp.zeros((MAX_TILES, MOE_TILE), F32)
    lo_acc = jnp.zeros((MAX_TILES, MOE_TILE), F32)
    hit_acc = jnp.zeros((MAX_TILES, MOE_TILE), F32)
    t_iota = lax.broadcasted_iota(jnp.int32, (MAX_TILES, INV_CHUNK), 0)
    r_iota = lax.broadcasted_iota(jnp.int32, (MOE_TILE, INV_CHUNK), 0)
    for c in range(n_pairs // INV_CHUNK):
        pos = pos_ref[:, c * INV_CHUNK:(c + 1) * INV_CHUNK]
        p = lax.broadcasted_iota(jnp.int32, (1, INV_CHUNK), 1) + c * INV_CHUNK
        tok = jnp.where(p >= n_tokens, p - n_tokens, p)
        in_tile = (pos >> (MOE_TILE.bit_length() - 1)) == t_iota
        a_hi = jnp.where(in_tile, (tok >> 7).astype(F32), 0.0).astype(BF16)
        a_lo = jnp.where(in_tile, (tok & 127).astype(F32), 0.0).astype(BF16)
        b = ((pos & (MOE_TILE - 1)) == r_iota).astype(BF16)
        dn = (((1,), (1,)), ((), ()))
        hi_acc = hi_acc + lax.dot_general(a_hi, b, dn, preferred_element_type=F32)
        lo_acc = lo_acc + lax.dot_general(a_lo, b, dn, preferred_element_type=F32)
        hit_acc = hit_acc + lax.dot_general(in_tile.astype(F32).astype(BF16), b, dn, preferred_element_type=F32)
    slot = (lax.broadcasted_iota(jnp.int32, (MAX_TILES, MOE_TILE), 0) * MOE_TILE
            + lax.broadcasted_iota(jnp.int32, (MAX_TILES, MOE_TILE), 1))
    spread = slot & ((1 << (n_tokens.bit_length() - 1)) - 1)
    inv_ref[...] = jnp.where(hit_acc > 0, (hi_acc * 128.0 + lo_acc).astype(jnp.int32), spread)


def _moe_invert(pos_row, n_tokens):
    return pl.pallas_call(
        functools.partial(_invert_kernel, n_tokens=n_tokens),
        out_shape=jax.ShapeDtypeStruct((MAX_TILES, MOE_TILE), jnp.int32),
        compiler_params=pltpu.CompilerParams(vmem_limit_bytes=VMEM_LIMIT),
        name="moe_invert",
    )(pos_row)


TILE_ROWS = MOE_TILE * ROW_PITCH
GATHER_DEPTH = 3


def _experts_kernel(starts_ref, inv_hbm, hx_hbm, g_ref, wg_ref, wu_ref, wd_ref, y_hbm,
                    xbuf, ybuf, inv_sm, gsem, ysem, isem, wg_bf, wu_bf, wd_bf, *, n_slots):
    e = pl.program_id(0)
    t0, t1 = starts_ref[e], starts_ref[e + 1]
    n_tiles = starts_ref[N_EXPERTS]
    last = n_tiles - 1

    def inv_copy(t, s):
        return pltpu.make_async_copy(inv_hbm.at[jnp.minimum(t, last)], inv_sm.at[s], isem.at[s])

    def y_copy(t, s):
        return pltpu.make_async_copy(ybuf.at[s], y_hbm.at[pl.ds(t * TILE_ROWS, TILE_ROWS)], ysem.at[s])

    def gather_start(s, r):
        _row_copy(hx_hbm, inv_sm[s, 0, r], xbuf.at[s], r, gsem.at[s]).start()

    @pl.when((e == 0) & (n_tiles > 0))
    def _():
        for s in range(GATHER_DEPTH):
            inv_copy(s, s).start()
        for s in range(GATHER_DEPTH - 1):
            inv_copy(s, s).wait()

            def body(r, carry, s=s):
                gather_start(s, r)
                return carry
            lax.fori_loop(0, MOE_TILE, body, 0, unroll=8)

    @pl.when(t1 > t0)
    def _():
        wg_bf[...] = wg_ref[...].astype(BF16)
        wu_bf[...] = wu_ref[...].astype(BF16)
        wd_bf[...] = wd_ref[...].astype(BF16)

        def tile(t, carry):
            s, s_req, s_y = t % GATHER_DEPTH, (t + GATHER_DEPTH - 1) % GATHER_DEPTH, t % 2
            inv_copy(t + GATHER_DEPTH, s).start()
            inv_copy(t + GATHER_DEPTH - 1, s_req).wait()
            _rows_wait(hx_hbm, xbuf.at[s], MOE_TILE, gsem.at[s])

            @pl.when(t >= 2)
            def _():
                y_copy(t - 2, s_y).wait()

            x = _load_rows(xbuf.at[s], MOE_TILE)
            for r in range(MOE_TILE):
                gather_start(s_req, r)
            xn = _rms(x, g_ref[...]).astype(BF16)
            hg = jnp.dot(xn, wg_bf[...], preferred_element_type=F32)
            hu = jnp.dot(xn, wu_bf[...], preferred_element_type=F32)
            hid = (hg / (1.0 + jnp.exp(-hg)) * hu).astype(BF16)
            _store_rows(ybuf.at[s_y], jnp.dot(hid, wd_bf[...], preferred_element_type=F32))
            y_copy(t, s_y).start()
            return carry

        lax.fori_loop(t0, t1, tile, 0)

    @pl.when(e == pl.num_programs(0) - 1)
    def _():
        @pl.when(n_tiles > 0)
        def _():
            for ahead in range(GATHER_DEPTH - 1):
                s = (n_tiles + ahead) % GATHER_DEPTH
                _rows_wait(hx_hbm, xbuf.at[s], MOE_TILE, gsem.at[s])
            inv_copy(last, last % GATHER_DEPTH).wait()
            y_copy(last, last % 2).wait()

        @pl.when(n_tiles > 1)
        def _():
            y_copy(last - 1, (last - 1) % 2).wait()

        ybuf[0] = jnp.zeros((TILE_ROWS, LANES), F32)

        def fill(t, carry):
            y_copy(t, 0).start()
            y_copy(t, 0).wait()
            return carry

        lax.fori_loop(n_tiles, n_slots, fill, 0)


def _moe_experts(tile_tab, inv, hx_all, g_ffn, w_gate, w_up, w_down, layer, n_steps):
    d = D_MODEL

    def widx(e, tab):
        return (layer, e, 0, 0)

    grid_spec = pltpu.PrefetchScalarGridSpec(
        num_scalar_prefetch=1,
        grid=(N_EXPERTS,),
        in_specs=[pl.BlockSpec(memory_space=pl.ANY),
                  pl.BlockSpec(memory_space=pl.ANY),
                  pl.BlockSpec((1, d), lambda e, tab: (0, 0)),
                  pl.BlockSpec((None, None, d, D_EXPERT), widx),
                  pl.BlockSpec((None, None, d, D_EXPERT), widx),
                  pl.BlockSpec((None, None, D_EXPERT, d), widx)],
        out_specs=pl.BlockSpec(memory_space=pl.ANY),
        scratch_shapes=[pltpu.VMEM((GATHER_DEPTH, TILE_ROWS, LANES), F32),
                        pltpu.VMEM((2, TILE_ROWS, LANES), F32),
                        pltpu.SMEM((GATHER_DEPTH, 1, MOE_TILE), jnp.int32),
                        pltpu.SemaphoreType.DMA((GATHER_DEPTH,)),
                        pltpu.SemaphoreType.DMA((2,)),
                        pltpu.SemaphoreType.DMA((GATHER_DEPTH,)),
                        pltpu.VMEM((d, D_EXPERT), BF16),
                        pltpu.VMEM((d, D_EXPERT), BF16),
                        pltpu.VMEM((D_EXPERT, d), BF16)],
    )
    return pl.pallas_call(
        functools.partial(_experts_kernel, n_slots=n_steps),
        grid_spec=grid_spec,
        out_shape=jax.ShapeDtypeStruct((n_steps * TILE_ROWS, LANES), F32),
        compiler_params=_cparams(("arbitrary",)),
        name="moe_experts",
    )(tile_tab, inv, hx_all, g_ffn, w_gate, w_up, w_down)


PLE_COLS = 512


def _combine_kernel(*refs, tm, n_steps):
    lists = [(refs[k], refs[GATHER_DEPTH + k]) for k in range(GATHER_DEPTH)]
    y_hbm, h_ref, route_ref, p_ref, g_ref, wg_ref, wp_ref, o_ref, ybuf, sems = refs[2 * GATHER_DEPTH:]
    i = pl.program_id(0)
    s, s_req = i % GATHER_DEPTH, (i + GATHER_DEPTH - 1) % GATHER_DEPTH

    def request(p0, p1, sl, r):
        _row_copy(y_hbm, p0[0, 0, r], ybuf.at[sl, 0], r, sems.at[sl]).start()
        _row_copy(y_hbm, p1[0, 0, r], ybuf.at[sl, 1], r, sems.at[sl]).start()

    def wait_tile(sl):
        for c in range(2):
            _rows_wait(y_hbm, ybuf.at[sl, c], tm, sems.at[sl])

    @pl.when(i == 0)
    def _():
        for k in range(GATHER_DEPTH - 1):
            def body(r, carry, k=k):
                request(lists[k][0], lists[k][1], k, r)
                return carry
            lax.fori_loop(0, tm, body, 0, unroll=8)

    wait_tile(s)
    route = route_ref[...]
    y0, y1 = _load_rows(ybuf.at[s, 0], tm), _load_rows(ybuf.at[s, 1], tm)
    for r in range(tm):
        request(lists[-1][0], lists[-1][1], s_req, r)
    h2 = h_ref[...] + (route[:, 2:3] * y0 + route[:, 3:4] * y1)
    xn = _rms(h2, g_ref[...]).astype(BF16)
    pb = p_ref[...].astype(BF16)
    for j in range(h2.shape[1] // PLE_COLS):
        cols = slice(j * PLE_COLS, (j + 1) * PLE_COLS)
        z = jnp.dot(xn, wg_ref[:, cols], preferred_element_type=F32)
        proj = jnp.dot(pb, wp_ref[:, cols], preferred_element_type=F32)
        o_ref[:, cols] = h2[:, cols] + proj * (1.0 / (1.0 + jnp.exp(-z)))

    @pl.when(i == n_steps - 1)
    def _():
        for ahead in range(GATHER_DEPTH - 1):
            wait_tile((n_steps + ahead) % GATHER_DEPTH)


def _moe_combine_ple(pos0, pos1, y_sorted, h_all, route_all, p, g_ple, w_gate, w_proj, *, layer, tm, rows,
                     row_block0, name):
    d = h_all.shape[1]
    n = rows // tm
    last = n - 1
    ahead = [pl.BlockSpec((1, 1, tm), lambda i, k=k: (jnp.minimum(i + k, last), 0, 0), memory_space=pltpu.SMEM)
             for k in range(GATHER_DEPTH)]
    return pl.pallas_call(
        functools.partial(_combine_kernel, tm=tm, n_steps=n),
        grid=(n,),
        in_specs=ahead + ahead + [
                  pl.BlockSpec(memory_space=pl.ANY),
                  pl.BlockSpec((tm, d), lambda i: (row_block0 + i, 0)),
                  pl.BlockSpec((tm, ROUTE_LANES), lambda i: (row_block0 + i, 0)),
                  pl.BlockSpec((None, tm, PLE_DIM), lambda i: (layer, i, 0)),
                  pl.BlockSpec((1, d), lambda i: (0, 0)),
                  pl.BlockSpec((d, d), lambda i: (0, 0)),
                  pl.BlockSpec((PLE_DIM, d), lambda i: (0, 0))],
        out_specs=pl.BlockSpec((tm, d), lambda i: (i, 0)),
        out_shape=jax.ShapeDtypeStruct((rows, d), F32),
        scratch_shapes=[pltpu.VMEM((GATHER_DEPTH, 2, tm * ROW_PITCH, LANES), F32),
                        pltpu.SemaphoreType.DMA((GATHER_DEPTH,))],
        compiler_params=_cparams(("arbitrary",)),
        name=name,
    )(*([pos0] * GATHER_DEPTH + [pos1] * GATHER_DEPTH), y_sorted, h_all, route_all, p, g_ple, w_gate, w_proj)


def _ret_head(q, k, v, gt, state, gn, lg, c_real):
    c = q.shape[0]
    k = k * (RET_DK ** -0.5)
    ri = lax.broadcasted_iota(jnp.int32, (c, c), 0)
    ci = lax.broadcasted_iota(jnp.int32, (c, c), 1)
    diff = (ri - ci).astype(F32)
    decay = jnp.where(diff >= 0, jnp.exp(lg * jnp.maximum(diff, 0.0)), 0.0)
    idx = lax.broadcasted_iota(jnp.int32, (c, 1), 0).astype(F32)
    qk = lax.dot_general(q.astype(BF16), k.astype(BF16), (((1,), (1,)), ((), ())), preferred_element_type=F32)
    inner = qk * decay
    q_dec = q * jnp.exp(lg * (idx + 1.0))
    k_dec = k * jnp.exp(lg * (c_real - 1.0 - idx))
    vb = v.astype(BF16)
    out = (jnp.dot(inner.astype(BF16), vb, preferred_element_type=F32)
           + jnp.dot(q_dec.astype(BF16), state.astype(BF16), preferred_element_type=F32))
    new_state = math.exp(lg * c_real) * state + lax.dot_general(
        k_dec.astype(BF16), vb, (((0,), (0,)), ((), ())), preferred_element_type=F32)
    mu = jnp.mean(out, axis=-1, keepdims=True)
    cen = out - mu
    var = jnp.mean(cen * cen, axis=-1, keepdims=True)
    on = cen * lax.rsqrt(var + GN_EPS) * gn
    return gt / (1.0 + jnp.exp(-gt)) * on, new_state


def _ret_prompt_kernel(q_ref, k_ref, v_ref, g_ref, gn_ref, y_ref, so_ref, st_ref):
    c = pl.program_id(1)

    @pl.when(c == 0)
    def _():
        st_ref[...] = jnp.zeros_like(st_ref)

    for h in range(RET_HEADS):
        ks, vs = slice(h * RET_DK, (h + 1) * RET_DK), slice(h * RET_DV, (h + 1) * RET_DV)
        y, new_state = _ret_head(q_ref[:, ks], k_ref[:, ks], v_ref[:, vs], g_ref[:, vs], st_ref[h],
                                 gn_ref[:, vs], RET_LOG_GAMMA[h], RET_CHUNK)
        st_ref[h] = new_state
        y_ref[:, vs] = y.astype(y_ref.dtype)

    @pl.when(c == pl.num_programs(1) - 1)
    def _():
        so_ref[0] = st_ref[...]


def _ret_prompt(qkvg, g_norm, batch):
    rows = qkvg.shape[0]
    nc = rows // batch // RET_CHUNK
    return pl.pallas_call(
        _ret_prompt_kernel,
        grid=(batch, nc),
        in_specs=[pl.BlockSpec((RET_CHUNK, RET_K_WIDTH), lambda b, c: (b * nc + c, 0)),
                  pl.BlockSpec((RET_CHUNK, RET_K_WIDTH), lambda b, c: (b * nc + c, 1)),
                  pl.BlockSpec((RET_CHUNK, RET_V_WIDTH), lambda b, c: (b * nc + c, 1)),
                  pl.BlockSpec((RET_CHUNK, RET_V_WIDTH), lambda b, c: (b * nc + c, 2)),
                  pl.BlockSpec((1, RET_V_WIDTH), lambda b, c: (0, 0))],
        out_specs=[pl.BlockSpec((RET_CHUNK, RET_V_WIDTH), lambda b, c: (b * nc + c, 0)),
                   pl.BlockSpec((1, RET_HEADS, RET_DK, RET_DV), lambda b, c: (b, 0, 0, 0))],
        out_shape=[jax.ShapeDtypeStruct((rows, RET_V_WIDTH), BF16),
                   jax.ShapeDtypeStruct((batch, RET_HEADS, RET_DK, RET_DV), F32)],
        scratch_shapes=[pltpu.VMEM((RET_HEADS, RET_DK, RET_DV), F32)],
        compiler_params=_cparams(("parallel", "arbitrary")),
        name="ret_prompt",
    )(qkvg, qkvg, qkvg, qkvg, g_norm)


def _ret_decode_kernel(x_ref, s_ref, gn_ref, y_ref, so_ref, *, t_new):
    x = x_ref[0]
    for h in range(RET_HEADS):
        q = x[:, h * RET_DK:(h + 1) * RET_DK]
        k = x[:, RET_K_WIDTH + h * RET_DK:RET_K_WIDTH + (h + 1) * RET_DK]
        v0 = 2 * RET_K_WIDTH + h * RET_DV
        g0 = 2 * RET_K_WIDTH + RET_V_WIDTH + h * RET_DV
        y, new_state = _ret_head(q, k, x[:, v0:v0 + RET_DV], x[:, g0:g0 + RET_DV], s_ref[0, h],
                                 gn_ref[:, h * RET_DV:(h + 1) * RET_DV], RET_LOG_GAMMA[h], t_new)
        so_ref[0, h] = new_state
        y_ref[0, :, h * RET_DV:(h + 1) * RET_DV] = y


def _ret_decode(qkvg, state, g_norm, t_new):
    b = qkvg.shape[0]
    return pl.pallas_call(
        functools.partial(_ret_decode_kernel, t_new=t_new),
        grid=(b,),
        in_specs=[pl.BlockSpec((1, DEC_ROWS, RET_IN_WIDTH), lambda i: (i, 0, 0)),
                  pl.BlockSpec((1, RET_HEADS, RET_DK, RET_DV), lambda i: (i, 0, 0, 0)),
                  pl.BlockSpec((1, RET_V_WIDTH), lambda i: (0, 0))],
        out_specs=[pl.BlockSpec((1, DEC_ROWS, RET_V_WIDTH), lambda i: (i, 0, 0)),
                   pl.BlockSpec((1, RET_HEADS, RET_DK, RET_DV), lambda i: (i, 0, 0, 0))],
        out_shape=[jax.ShapeDtypeStruct((b, DEC_ROWS, RET_V_WIDTH), F32),
                   jax.ShapeDtypeStruct((b, RET_HEADS, RET_DK, RET_DV), F32)],
        compiler_params=_cparams(("parallel",)),
        name="ret_decode",
    )(qkvg, state, g_norm)


def _pad_rows(x, b, t):
    return jnp.pad(x.reshape(b, t, x.shape[-1]), ((0, 0), (0, DEC_ROWS - t), (0, 0)))


def _moe_layer(h_all, route_all, hx_all, p_prompt, p_sample, g_ffn, g_ple, w_gate_e, w_up_e, w_down_e, w_ple_gate,
               w_ple_proj, layer, n_prompt, n_sample):
    n_tok = n_prompt + n_sample
    n_pairs = PAIR_ROWS * LANES
    eid = route_all[:, :2].astype(jnp.int32).T.reshape(-1)
    eid = jnp.pad(eid, (0, n_pairs - 2 * n_tok), constant_values=-1).reshape(PAIR_ROWS, LANES)
    pos, tile_tab = _moe_meta(eid)
    inv = _moe_invert(pos.reshape(1, n_pairs), n_tok).reshape(MAX_TILES, 1, MOE_TILE)
    max_tiles = (2 * n_tok + N_EXPERTS * (MOE_TILE - 1)) // MOE_TILE
    y_sorted = _moe_experts(tile_tab[0], inv, hx_all, g_ffn, w_gate_e, w_up_e, w_down_e, layer, max_tiles)
    pos2 = pos.reshape(-1)[:2 * n_tok].reshape(2, n_tok)
    tm_p, tm_s = 256, n_sample
    outs = []
    for lo, rows, tm, p, nm in ((0, n_prompt, tm_p, p_prompt, "combine_prompt"),
                                (n_prompt, n_sample, tm_s, p_sample, "combine_sample")):
        pos0 = pos2[0, lo:lo + rows].reshape(rows // tm, 1, tm)
        pos1 = pos2[1, lo:lo + rows].reshape(rows // tm, 1, tm)
        outs.append(_moe_combine_ple(pos0, pos1, y_sorted, h_all, route_all, p, g_ple, w_ple_gate, w_ple_proj,
                                     layer=layer, tm=tm, rows=rows, row_block0=lo // tm, name=nm))
    return outs


def kernel(x_prompt, x_sample, cache_k_swa, cache_v_swa, state_ret, p_prompt, p_sample, g_mix, g_ffn, g_ple,
           swa_w_qkv, swa_g_q, swa_g_k, swa_sinks, swa_w_o, ret_w_in, ret_g_norm, ret_w_o, moe_w_group,
           moe_b_group, moe_w_expert, moe_b_expert, moe_w_gate, moe_w_up, moe_w_down, ple_w_proj, ple_w_gate):
    batch, seq, d = x_prompt.shape
    dec_b, dec_t, _ = x_sample.shape
    n_prompt, n_sample = batch * seq, dec_b * dec_t
    n_tok = n_prompt + n_sample
    assert 2 * n_tok <= PAIR_ROWS * LANES and 2 * n_tok + N_EXPERTS * (MOE_TILE - 1) <= (MAX_TILES - 1) * MOE_TILE
    assert dec_t <= DEC_ROWS and n_prompt % 2048 == 0 and n_sample % 8 == 0

    xp = x_prompt.reshape(n_prompt, d)
    xs = x_sample.reshape(n_sample, d)
    pp = p_prompt.reshape(DEPTH, n_prompt, PLE_DIM)
    ps = p_sample.reshape(DEPTH, n_sample, PLE_DIM)
    row = lambda a: a.reshape(1, -1)

    def router_w(i):
        w = jnp.concatenate([moe_w_group[i], moe_w_expert[i]], axis=1)
        b = jnp.concatenate([moe_b_group[i], moe_b_expert[i]])
        pad = ROUTE_LANES - w.shape[1]
        return jnp.pad(w, ((0, 0), (0, pad))), jnp.pad(b, (0, pad)).reshape(1, ROUTE_LANES)

    g_q2, g_k2 = row(jnp.tile(swa_g_q[0], 2)), row(jnp.tile(swa_g_k[0], 2))
    w_r, b_r = router_w(0)
    qkv_p = _norm_proj(xp, row(g_mix[0]), swa_w_qkv[0], tm=2048, tn=256, hi=False, name="qkv_prompt")
    sinks2 = swa_sinks[0] * LOG2E
    o_p, k_p, v_p = _swa_prompt(qkv_p, g_q2, g_k2, sinks2, batch)

    qkv_s = _norm_proj(xs, row(g_mix[0]), swa_w_qkv[0], tm=n_sample, tn=512, hi=True, name="qkv_sample")
    ck = cache_k_swa[0].reshape(dec_b, WINDOW, KV_WIDTH)
    cv = cache_v_swa[0].reshape(dec_b, WINDOW, KV_WIDTH)
    o_s, k_s, v_s = _swa_decode(_pad_rows(qkv_s, dec_b, dec_t), ck, cv, g_q2, g_k2, sinks2, dec_t)
    o_s = o_s[:, :dec_t].reshape(n_sample, Q_WIDTH)
    tail = _outproj_router(o_s, swa_w_o[0], xs, row(g_ffn[0]), w_r, b_r, tm=n_sample, tk=2048, hi=True,
                           tail=None, name="swa_out_sample")
    h_all, route_all, hx_all = _outproj_router(o_p, swa_w_o[0].astype(BF16), xp, row(g_ffn[0]), w_r.astype(BF16),
                                               b_r, tm=512, tk=2048, hi=False, tail=tail, name="swa_out_prompt")

    h_p, h_s = _moe_layer(h_all, route_all, hx_all, pp, ps, row(g_ffn[0]), row(g_ple[0]), moe_w_gate, moe_w_up,
                          moe_w_down, ple_w_gate[0].astype(BF16), ple_w_proj[0].astype(BF16), 0, n_prompt, n_sample)

    w_o = ret_w_o[0].astype(BF16)
    w_r, b_r = router_w(1)
    w_r = w_r.astype(BF16)
    qkvg_p = _norm_proj(h_p, row(g_mix[1]), ret_w_in[0], tm=2048, tn=256, hi=False, name="ret_in_prompt")
    y_p, s_p = _ret_prompt(qkvg_p, row(ret_g_norm[0]), batch)

    qkvg_s = _norm_proj(h_s, row(g_mix[1]), ret_w_in[0], tm=n_sample, tn=1024, hi=False, name="ret_in_sample")
    y_s, s_s = _ret_decode(_pad_rows(qkvg_s, dec_b, dec_t), state_ret[0], row(ret_g_norm[0]), dec_t)
    y_s = y_s[:, :dec_t].reshape(n_sample, RET_V_WIDTH)
    tail = _outproj_router(y_s, w_o, h_s, row(g_ffn[1]), w_r, b_r, tm=n_sample, tk=2048, hi=False, tail=None,
                           name="ret_out_sample")
    h_all, route_all, hx_all = _outproj_router(y_p, w_o, h_p, row(g_ffn[1]), w_r, b_r, tm=512, tk=2048, hi=False,
                                               tail=tail, name="ret_out_prompt")

    y_prompt, y_sample = _moe_layer(h_all, route_all, hx_all, pp, ps, row(g_ffn[1]), row(g_ple[1]), moe_w_gate,
                                    moe_w_up, moe_w_down, ple_w_gate[1].astype(BF16), ple_w_proj[1].astype(BF16), 1,
                                    n_prompt, n_sample)

    kv_shape = (1, -1, WINDOW, SWA_KV_HEADS, SWA_HEAD_DIM)
    return (y_prompt.reshape(batch, seq, d), y_sample.reshape(dec_b, dec_t, d),
            k_p.reshape(kv_shape), v_p.reshape(kv_shape), s_p[None],
            k_s.reshape(kv_shape), v_s.reshape(kv_shape), s_s[None])
```

```python
import functools
import math

import jax
import jax.numpy as jnp
from jax import lax
from jax.experimental import pallas as pl
from jax.experimental.pallas import tpu as pltpu

D_MODEL = 2048
DEPTH = 2
SWA_HEADS = 32
SWA_KV_HEADS = 8
SWA_HEAD_DIM = 64
Q_WIDTH = SWA_HEADS * SWA_HEAD_DIM
KV_WIDTH = SWA_KV_HEADS * SWA_HEAD_DIM
QKV_WIDTH = Q_WIDTH + 2 * KV_WIDTH
WINDOW = 128
ATTN_SCALE = SWA_HEAD_DIM ** -0.5
LOG2E = math.log2(math.e)
RET_HEADS = 8
RET_DK = D_MODEL // RET_HEADS
RET_DV = 2 * D_MODEL // RET_HEADS
RET_K_WIDTH = RET_HEADS * RET_DK
RET_V_WIDTH = RET_HEADS * RET_DV
RET_IN_WIDTH = 2 * RET_K_WIDTH + 2 * RET_V_WIDTH
RET_CHUNK = 256
N_GROUPS = 4
EXPERTS_PER_GROUP = 8
N_EXPERTS = N_GROUPS * EXPERTS_PER_GROUP
D_EXPERT = D_MODEL // 4
PLE_DIM = 256
NORM_EPS = 1e-6
GN_EPS = 1e-5
NEG_INF = -1e30

LANES = 128
ROUTE_LANES = LANES
EXPERT_LANE0 = N_GROUPS
MOE_TILE = 256
PAIR_ROWS = 256
MAX_TILES = LANES
VMEM_LIMIT = 60 * 1024 * 1024
ROW_TILES = D_MODEL // LANES
ROW_PITCH = ROW_TILES + 4

F32 = jnp.float32
BF16 = jnp.bfloat16
HIGHEST = lax.Precision.HIGHEST

ALIBI_SLOPES = tuple(2.0 ** (-8.0 * (h + 1) / SWA_HEADS) for h in range(SWA_HEADS))
RET_LOG_GAMMA = tuple(math.log(1.0 - 2.0 ** (-5.0 - h)) for h in range(RET_HEADS))


def _cparams(sem):
    return pltpu.CompilerParams(dimension_semantics=sem, vmem_limit_bytes=VMEM_LIMIT)


def _mm_dtype(hi):
    return F32 if hi else BF16


def _mm_prec(hi):
    return HIGHEST if hi else None


def _rms(x, g):
    return x * lax.rsqrt(jnp.mean(x * x, axis=-1, keepdims=True) + NORM_EPS) * g


def _store_rows(ref, x):
    n = x.shape[0]
    for s in range(ROW_PITCH):
        piece = x[:, s * LANES:(s + 1) * LANES] if s < ROW_TILES else jnp.zeros((n, LANES), F32)
        ref[pl.ds(s, n, stride=ROW_PITCH), :] = piece


def _load_rows(ref, n):
    return jnp.concatenate([ref[pl.ds(s, n, stride=ROW_PITCH), :] for s in range(ROW_TILES)], axis=1)


def _row_copy(src_hbm, src_row, dst, dst_row, sem):
    return pltpu.make_async_copy(src_hbm.at[pl.ds(src_row * ROW_PITCH, ROW_TILES)],
                                 dst.at[pl.ds(dst_row * ROW_PITCH, ROW_TILES)], sem)


def _rows_wait(src_hbm, dst, n, sem):
    pltpu.make_async_copy(src_hbm.at[pl.ds(0, n * ROW_TILES)], dst.at[pl.ds(0, n * ROW_TILES)], sem).wait()


def _proj_kernel(x_ref, g_ref, w_ref, o_ref, a_ref, *, hi):
    @pl.when(pl.program_id(1) == 0)
    def _():
        a_ref[...] = _rms(x_ref[...], g_ref[...]).astype(a_ref.dtype)

    o_ref[...] = jnp.dot(a_ref[...], w_ref[...].astype(a_ref.dtype), preferred_element_type=F32,
                         precision=_mm_prec(hi)).astype(o_ref.dtype)


def _norm_proj(x, g, w, *, tm, tn, hi, name):
    m, k = x.shape
    n = w.shape[1]
    return pl.pallas_call(
        functools.partial(_proj_kernel, hi=hi),
        grid=(m // tm, n // tn),
        in_specs=[pl.BlockSpec((tm, k), lambda i, j: (i, 0)),
                  pl.BlockSpec((1, k), lambda i, j: (0, 0)),
                  pl.BlockSpec((k, tn), lambda i, j: (0, j))],
        out_specs=pl.BlockSpec((tm, tn), lambda i, j: (i, j)),
        out_shape=jax.ShapeDtypeStruct((m, n), F32),
        scratch_shapes=[pltpu.VMEM((tm, k), _mm_dtype(hi))],
        compiler_params=_cparams(("parallel", "arbitrary")),
        name=name,
    )(x, g, w)


def _head_norm(x, g2):
    lane = lax.broadcasted_iota(jnp.int32, (1, LANES), 1)
    lo = lane < SWA_HEAD_DIM
    outs = []
    for j in range(x.shape[1] // LANES):
        xb = x[:, j * LANES:(j + 1) * LANES]
        x2 = xb * xb
        s_lo = jnp.sum(jnp.where(lo, x2, 0.0), axis=-1, keepdims=True)
        s_hi = jnp.sum(jnp.where(lo, 0.0, x2), axis=-1, keepdims=True)
        r = jnp.where(lo, lax.rsqrt(s_lo * (1.0 / SWA_HEAD_DIM) + NORM_EPS),
                      lax.rsqrt(s_hi * (1.0 / SWA_HEAD_DIM) + NORM_EPS))
        outs.append(xb * r * g2)
    return jnp.concatenate(outs, axis=1)


def _mm3(a, b, dims):
    (ca,), (cb,) = dims
    assert ca == 1
    m, n = a.shape[0], b.shape[1 - cb]
    a_hi, b_hi = a.astype(BF16), b.astype(BF16)
    a_lo = (a - a_hi.astype(F32)).astype(BF16)
    b_lo = (b - b_hi.astype(F32)).astype(BF16)
    a2 = jnp.concatenate([a_hi, a_lo], axis=0)
    b2 = jnp.concatenate([b_hi, b_lo], axis=1 - cb)
    r = lax.dot_general(a2, b2, (dims, ((), ())), preferred_element_type=F32)
    return r[:m, :n] + (r[:m, n:] + r[m:, :n])


def _attn_heads(q, g_q2, kn, v, bias_ref, sink_ref, hi, emit):
    nq, nk = q.shape[0], kn.shape[0]
    cdt = _mm_dtype(hi)
    lane = lax.broadcasted_iota(jnp.int32, (1, LANES), 1)
    lo = lane < SWA_HEAD_DIM
    group = SWA_HEADS // SWA_KV_HEADS
    ones_bd = (lo == (lax.broadcasted_iota(jnp.int32, (2 * nk, LANES), 0) < nk)).astype(cdt)
    for m in range(KV_WIDTH // LANES):
        kb = kn[:, m * LANES:(m + 1) * LANES]
        vb = v[:, m * LANES:(m + 1) * LANES]
        kr = pltpu.roll(kb, SWA_HEAD_DIM, 1)
        vr = pltpu.roll(vb, SWA_HEAD_DIM, 1)
        for half in range(2):
            kv = 2 * m + half
            k_lo, k_hi = (kb, kr) if half == 0 else (kr, kb)
            v_lo, v_hi = (vb, vr) if half == 0 else (vr, vb)
            kbd = jnp.concatenate([jnp.where(lo, k_lo, 0.0), jnp.where(lo, 0.0, k_hi)], axis=0).astype(cdt)
            vbd = jnp.concatenate([jnp.where(lo, v_lo, 0.0), jnp.where(lo, 0.0, v_hi)], axis=0).astype(cdt)
            c0 = kv * group * SWA_HEAD_DIM
            qs = _head_norm(q[:, c0:c0 + 2 * LANES], g_q2) * (ATTN_SCALE * LOG2E)
            q2 = jnp.concatenate([qs[:, :LANES], qs[:, LANES:]], axis=0).astype(cdt)
            if hi:
                s_all = _mm3(q2, kbd, ((1,), (1,)))
            else:
                s_all = lax.dot_general(q2, kbd, (((1,), (1,)), ((), ())), preferred_element_type=F32)
            e_rows, sink_e = [], []
            for r in range(2):
                e_cols = []
                for c in range(2):
                    h = kv * group + 2 * r + c
                    s = s_all[r * nq:(r + 1) * nq, c * nk:(c + 1) * nk] + bias_ref[h]
                    sk = sink_ref[h]
                    mx = jnp.maximum(jnp.max(s, axis=-1, keepdims=True), sk)
                    e_cols.append(jnp.exp2(s - mx).astype(cdt))
                    sink_e.append(jnp.exp2(sk - mx))
                e_rows.append(jnp.concatenate(e_cols, axis=1))
            p2 = jnp.concatenate(e_rows, axis=0)
            if hi:
                o2 = _mm3(p2, vbd, ((1,), (0,)))
                sums = [jnp.sum(p2[r * nq:(r + 1) * nq, c * nk:(c + 1) * nk], axis=-1, keepdims=True)
                        for r in range(2) for c in range(2)]
            else:
                o_den = jnp.dot(p2, jnp.concatenate([vbd, ones_bd], axis=1), preferred_element_type=F32)
                o2, den2 = o_den[:, :LANES], o_den[:, LANES:]
            for r in range(2):
                sums_r = jnp.where(lo, sums[2 * r], sums[2 * r + 1]) if hi else den2[r * nq:(r + 1) * nq]
                den = sums_r + jnp.where(lo, sink_e[2 * r], sink_e[2 * r + 1])
                emit(2 * kv + r, o2[r * nq:(r + 1) * nq] / den)


def _swa_bias(nq, q_pos0, n_keys, first_key):
    t = jnp.arange(nq)[:, None] + q_pos0
    s = jnp.arange(2 * WINDOW)[None, :]
    dist = t - s
    valid = (dist >= 0) & (dist <= WINDOW) & (s >= first_key) & (s < n_keys)
    slopes = jnp.asarray(ALIBI_SLOPES, F32)[:, None, None]
    return jnp.where(valid[None], -slopes * dist.astype(F32)[None] * LOG2E, NEG_INF)


def _swa_prompt_kernel(sink_ref, q_ref, k_ref, v_ref, gq_ref, gk_ref, bias_ref, o_ref, ko_ref, vo_ref, kprev, vprev):
    n = pl.program_id(1)

    @pl.when(n == 0)
    def _():
        kprev[...] = jnp.zeros_like(kprev)
        vprev[...] = jnp.zeros_like(vprev)

    kn = _head_norm(k_ref[...], gk_ref[...])
    v = v_ref[...]
    kcat = jnp.concatenate([kprev[...], kn], axis=0)
    vcat = jnp.concatenate([vprev[...], v], axis=0)

    def emit(j, block):
        o_ref[:, j * LANES:(j + 1) * LANES] = block.astype(o_ref.dtype)

    _attn_heads(q_ref[...], gq_ref[...], kcat, vcat, bias_ref, sink_ref, False, emit)
    kprev[...] = kn
    vprev[...] = v

    @pl.when(n == pl.num_programs(1) - 1)
    def _():
        ko_ref[0] = kn
        vo_ref[0] = v


def _swa_prompt(qkv, g_q, g_k, sinks, batch):
    rows = qkv.shape[0]
    nb = rows // batch // WINDOW
    kblk = Q_WIDTH // KV_WIDTH
    bias = jnp.stack([_swa_bias(WINDOW, WINDOW, 2 * WINDOW, WINDOW), _swa_bias(WINDOW, WINDOW, 2 * WINDOW, 0)])
    return pl.pallas_call(
        _swa_prompt_kernel,
        grid=(batch, nb),
        in_specs=[pl.BlockSpec(memory_space=pltpu.SMEM),
                  pl.BlockSpec((WINDOW, Q_WIDTH), lambda b, n: (b * nb + n, 0)),
                  pl.BlockSpec((WINDOW, KV_WIDTH), lambda b, n: (b * nb + n, kblk)),
                  pl.BlockSpec((WINDOW, KV_WIDTH), lambda b, n: (b * nb + n, kblk + 1)),
                  pl.BlockSpec((1, LANES), lambda b, n: (0, 0)),
                  pl.BlockSpec((1, LANES), lambda b, n: (0, 0)),
                  pl.BlockSpec((None, SWA_HEADS, WINDOW, 2 * WINDOW), lambda b, n: (jnp.minimum(n, 1), 0, 0, 0))],
        out_specs=[pl.BlockSpec((WINDOW, Q_WIDTH), lambda b, n: (b * nb + n, 0)),
                   pl.BlockSpec((1, WINDOW, KV_WIDTH), lambda b, n: (b, 0, 0)),
                   pl.BlockSpec((1, WINDOW, KV_WIDTH), lambda b, n: (b, 0, 0))],
        out_shape=[jax.ShapeDtypeStruct((rows, Q_WIDTH), BF16),
                   jax.ShapeDtypeStruct((batch, WINDOW, KV_WIDTH), F32),
                   jax.ShapeDtypeStruct((batch, WINDOW, KV_WIDTH), F32)],
        scratch_shapes=[pltpu.VMEM((WINDOW, KV_WIDTH), F32), pltpu.VMEM((WINDOW, KV_WIDTH), F32)],
        compiler_params=_cparams(("parallel", "arbitrary")),
        name="swa_prompt",
    )(sinks, qkv, qkv, qkv, g_q, g_k, bias)


DEC_ROWS = 8
DEC_GROUP = 4


def _swa_decode_kernel(sink_ref, qkv_ref, ck_ref, cv_ref, gq_ref, gk_ref, bias_ref, o_ref, ko_ref, vo_ref, *, t_new):
    fill = jnp.zeros((WINDOW - DEC_ROWS, KV_WIDTH), F32)
    is_new = lax.broadcasted_iota(jnp.int32, (WINDOW, 1), 0) >= WINDOW - t_new

    def slide(cache, new):
        new_at_end = pltpu.roll(jnp.concatenate([new, fill], axis=0), WINDOW - t_new, 0)
        return jnp.where(is_new, new_at_end, pltpu.roll(cache, WINDOW - t_new, 0))

    for b in range(qkv_ref.shape[0]):
        x = qkv_ref[b]
        kn = _head_norm(x[:, Q_WIDTH:Q_WIDTH + KV_WIDTH], gk_ref[...])
        v = x[:, Q_WIDTH + KV_WIDTH:]
        ck, cv = ck_ref[b], cv_ref[b]
        kcat = jnp.concatenate([ck, kn, fill], axis=0)
        vcat = jnp.concatenate([cv, v, fill], axis=0)

        def emit(j, block, b=b):
            o_ref[b, :, j * LANES:(j + 1) * LANES] = block

        _attn_heads(x[:, :Q_WIDTH], gq_ref[...], kcat, vcat, bias_ref, sink_ref, True, emit)
        ko_ref[b] = slide(ck, kn)
        vo_ref[b] = slide(cv, v)


def _swa_decode(qkv, cache_k, cache_v, g_q, g_k, sinks, t_new):
    b = qkv.shape[0]
    g = DEC_GROUP if b % DEC_GROUP == 0 else 1
    bias = _swa_bias(DEC_ROWS, WINDOW, WINDOW + t_new, 0)
    return pl.pallas_call(
        functools.partial(_swa_decode_kernel, t_new=t_new),
        grid=(b // g,),
        in_specs=[pl.BlockSpec(memory_space=pltpu.SMEM),
                  pl.BlockSpec((g, DEC_ROWS, QKV_WIDTH), lambda i: (i, 0, 0)),
                  pl.BlockSpec((g, WINDOW, KV_WIDTH), lambda i: (i, 0, 0)),
                  pl.BlockSpec((g, WINDOW, KV_WIDTH), lambda i: (i, 0, 0)),
                  pl.BlockSpec((1, LANES), lambda i: (0, 0)),
                  pl.BlockSpec((1, LANES), lambda i: (0, 0)),
                  pl.BlockSpec((SWA_HEADS, DEC_ROWS, 2 * WINDOW), lambda i: (0, 0, 0))],
        out_specs=[pl.BlockSpec((g, DEC_ROWS, Q_WIDTH), lambda i: (i, 0, 0)),
                   pl.BlockSpec((g, WINDOW, KV_WIDTH), lambda i: (i, 0, 0)),
                   pl.BlockSpec((g, WINDOW, KV_WIDTH), lambda i: (i, 0, 0))],
        out_shape=[jax.ShapeDtypeStruct((b, DEC_ROWS, Q_WIDTH), F32),
                   jax.ShapeDtypeStruct((b, WINDOW, KV_WIDTH), F32),
                   jax.ShapeDtypeStruct((b, WINDOW, KV_WIDTH), F32)],
        compiler_params=_cparams(("parallel",)),
        name="swa_decode",
    )(sinks, qkv, cache_k, cache_v, g_q, g_k, bias)


def _route(logits):
    lane = lax.broadcasted_iota(jnp.int32, logits.shape, 1).astype(F32)
    big = float(ROUTE_LANES)
    lg = jnp.where(lane < N_GROUPS, logits, NEG_INF)
    mg = jnp.max(lg, axis=-1, keepdims=True)
    gsel = jnp.min(jnp.where(lg == mg, lane, big), axis=-1, keepdims=True)
    pg_sel = 1.0 / jnp.sum(jnp.where(lane < N_GROUPS, jnp.exp(logits - mg), 0.0), axis=-1, keepdims=True)
    first = EXPERT_LANE0 + EXPERTS_PER_GROUP * gsel
    le = jnp.where((lane >= first) & (lane < first + EXPERTS_PER_GROUP), logits, NEG_INF)
    m1 = jnp.max(le, axis=-1, keepdims=True)
    i1 = jnp.min(jnp.where(le == m1, lane, big), axis=-1, keepdims=True)
    le2 = jnp.where(lane == i1, NEG_INF, le)
    m2 = jnp.max(le2, axis=-1, keepdims=True)
    i2 = jnp.min(jnp.where(le2 == m2, lane, big), axis=-1, keepdims=True)
    t = jnp.exp(m2 - m1)
    w0 = pg_sel / (1.0 + t)
    w1 = pg_sel * t / (1.0 + t)
    e0 = i1 - EXPERT_LANE0
    e1 = i2 - EXPERT_LANE0
    return jnp.where(lane == 0, e0, jnp.where(lane == 1, e1, jnp.where(lane == 2, w0, jnp.where(lane == 3, w1, 0.0))))


ROUTE_ROWS = 8


def _outproj_kernel(*refs, hi, n_tail, n_k):
    o_ref, w_ref, h_ref, g_ref, wr_ref, br_ref = refs[:6]
    route_ref, h1x_ref, rt_ref = refs[6 + n_tail:9 + n_tail]
    acc_ref = refs[-1]
    i, k = pl.program_id(0), pl.program_id(1)
    n_main = pl.num_programs(0) - (1 if n_tail else 0)
    cdt, prec = _mm_dtype(hi), _mm_prec(hi)

    def finish(h1):
        _store_rows(h1x_ref, h1)
        xn = _rms(h1, g_ref[...]).astype(cdt)
        logits = jnp.dot(xn, wr_ref[...], preferred_element_type=F32, precision=prec) + br_ref[...]
        rec = _route(logits)
        route_ref[...] = rec
        rt_ref[...] = rec.T[:ROUTE_ROWS]

    @pl.when(i < n_main)
    def _():
        def part():
            return jnp.dot(o_ref[...].astype(cdt), w_ref[...], preferred_element_type=F32, precision=prec)

        if n_k == 1:
            finish(h_ref[...] + part())
        else:
            @pl.when(k == 0)
            def _():
                acc_ref[...] = h_ref[...]

            acc_ref[...] += part()

            @pl.when(k == n_k - 1)
            def _():
                finish(acc_ref[...])

    if n_tail:
        @pl.when((i == n_main) & (k == n_k - 1))
        def _():
            for src, dst in zip(refs[6:8], (route_ref, h1x_ref)):
                rows = src.shape[0]
                dst[:rows] = src[...]
                dst[rows:] = jnp.zeros((dst.shape[0] - rows, dst.shape[1]), F32)
            cols = refs[8].shape[1]
            rt_ref[:, :cols] = refs[8][...]
            rt_ref[:, cols:] = jnp.zeros((ROUTE_ROWS, rt_ref.shape[1] - cols), F32)


def _outproj_router(o, w, h, g_ffn, w_r, b_r, *, tm, tk, hi, tail, name):
    m, kdim = o.shape
    d = w.shape[1]
    n_main, n_k = m // tm, kdim // tk
    n_tail = 0 if tail is None else 3
    rows_i = lambda i: jnp.minimum(i, n_main - 1)
    k_i = lambda i, k: jnp.where(i < n_main, k, n_k - 1)
    in_specs = [pl.BlockSpec((tm, tk), lambda i, k: (rows_i(i), k_i(i, k))),
                pl.BlockSpec((tk, d), lambda i, k: (k_i(i, k), 0)),
                pl.BlockSpec((tm, d), lambda i, k: (rows_i(i), 0)),
                pl.BlockSpec((1, d), lambda i, k: (0, 0)),
                pl.BlockSpec((d, ROUTE_LANES), lambda i, k: (0, 0)),
                pl.BlockSpec((1, ROUTE_LANES), lambda i, k: (0, 0))]
    args = [o, w, h, g_ffn, w_r, b_r]
    total_rows = m
    if tail is not None:
        assert tail[0].shape[0] <= tm
        in_specs += [pl.BlockSpec(t.shape, lambda i, k: (0, 0)) for t in tail]
        args += list(tail)
        total_rows += tail[0].shape[0]
    return pl.pallas_call(
        functools.partial(_outproj_kernel, hi=hi, n_tail=n_tail, n_k=n_k),
        grid=(n_main + (1 if n_tail else 0), n_k),
        in_specs=in_specs,
        out_specs=[pl.BlockSpec((tm, ROUTE_LANES), lambda i, k: (i, 0)),
                   pl.BlockSpec((tm * ROW_PITCH, LANES), lambda i, k: (i, 0)),
                   pl.BlockSpec((ROUTE_ROWS, tm), lambda i, k: (0, i))],
        out_shape=[jax.ShapeDtypeStruct((total_rows, ROUTE_LANES), F32),
                   jax.ShapeDtypeStruct((total_rows * ROW_PITCH, LANES), F32),
                   jax.ShapeDtypeStruct((ROUTE_ROWS, total_rows), F32)],
        scratch_shapes=[pltpu.VMEM((tm, d), F32)] if n_k > 1 else [],
        compiler_params=_cparams(("parallel", "arbitrary")),
        name=name,
    )(*args)


def _meta_kernel(eid_ref, pos_ref, tile_ref):
    eid = eid_ref[...]
    r_i = lax.broadcasted_iota(jnp.int32, (LANES, LANES), 0)
    c_i = lax.broadcasted_iota(jnp.int32, (LANES, LANES), 1)
    upper = (r_i <= c_i).astype(BF16)
    ones = jnp.ones((LANES, LANES), BF16)
    rr = lax.broadcasted_iota(jnp.int32, (PAIR_ROWS, PAIR_ROWS), 0)
    cc = lax.broadcasted_iota(jnp.int32, (PAIR_ROWS, PAIR_ROWS), 1)
    below = (cc < rr).astype(BF16)
    lane = lax.broadcasted_iota(jnp.int32, (1, LANES), 1)

    pos = jnp.zeros((PAIR_ROWS, LANES), F32)
    hit = jnp.zeros((PAIR_ROWS, LANES), F32)
    start = jnp.zeros((1, LANES), F32)
    starts = jnp.zeros((1, LANES), F32)
    for e in range(N_EXPERTS):
        mf = (eid == e).astype(F32)
        mb = mf.astype(BF16)
        incl = jnp.dot(mb, upper, preferred_element_type=F32)
        row_tot = jnp.dot(mb, ones, preferred_element_type=F32)
        row_off = jnp.dot(below, row_tot.astype(BF16), preferred_element_type=F32)
        rank = incl - mf + row_off
        cnt = row_off[PAIR_ROWS - 1:PAIR_ROWS, :] + row_tot[PAIR_ROWS - 1:PAIR_ROWS, :]
        padded = jnp.floor((cnt + (MOE_TILE - 1)) * (1.0 / MOE_TILE)) * MOE_TILE
        pos = pos + mf * (start + rank)
        hit = hit + mf
        start = start + padded
        starts = jnp.where(lane > e, start, starts)
    pos_ref[...] = jnp.where(hit > 0, pos, -1.0).astype(jnp.int32)
    tile_ref[...] = jnp.broadcast_to(starts * (1.0 / MOE_TILE), (8, LANES)).astype(jnp.int32)


def _moe_meta(eid_pairs):
    return pl.pallas_call(
        _meta_kernel,
        out_shape=[jax.ShapeDtypeStruct((PAIR_ROWS, LANES), jnp.int32),
                   jax.ShapeDtypeStruct((8, LANES), jnp.int32)],
        compiler_params=pltpu.CompilerParams(vmem_limit_bytes=VMEM_LIMIT),
        name="moe_meta",
    )(eid_pairs)


INV_CHUNK = 2048


def _invert_kernel(pos_ref, inv_ref, *, n_tokens):
    n_pairs = pos_ref.shape[1]
    hi_acc = jnp.zeros((MAX_TILES, MOE_TILE), F32)
    lo_acc = jnp.zeros((MAX_TILES, MOE_TILE), F32)
    hit_acc = jnp.zeros((MAX_TILES, MOE_TILE), F32)
    t_iota = lax.broadcasted_iota(jnp.int32, (MAX_TILES, INV_CHUNK), 0)
    r_iota = lax.broadcasted_iota(jnp.int32, (MOE_TILE, INV_CHUNK), 0)
    for c in range(min(n_pairs, -(-2 * n_tokens // INV_CHUNK) * INV_CHUNK) // INV_CHUNK):
        pos = pos_ref[:, c * INV_CHUNK:(c + 1) * INV_CHUNK]
        p = lax.broadcasted_iota(jnp.int32, (1, INV_CHUNK), 1) + c * INV_CHUNK
        tok = jnp.where(p >= n_tokens, p - n_tokens, p)
        in_tile = (pos >> (MOE_TILE.bit_length() - 1)) == t_iota
        a_hi = jnp.where(in_tile, (tok >> 7).astype(F32), 0.0).astype(BF16)
        a_lo = jnp.where(in_tile, (tok & 127).astype(F32), 0.0).astype(BF16)
        b = ((pos & (MOE_TILE - 1)) == r_iota).astype(BF16)
        dn = (((1,), (1,)), ((), ()))
        hi_acc = hi_acc + lax.dot_general(a_hi, b, dn, preferred_element_type=F32)
        lo_acc = lo_acc + lax.dot_general(a_lo, b, dn, preferred_element_type=F32)
        hit_acc = hit_acc + lax.dot_general(in_tile.astype(F32).astype(BF16), b, dn, preferred_element_type=F32)
    slot = (lax.broadcasted_iota(jnp.int32, (MAX_TILES, MOE_TILE), 0) * MOE_TILE
            + lax.broadcasted_iota(jnp.int32, (MAX_TILES, MOE_TILE), 1))
    spread = slot & ((1 << (n_tokens.bit_length() - 1)) - 1)
    inv_ref[...] = jnp.where(hit_acc > 0, (hi_acc * 128.0 + lo_acc).astype(jnp.int32), spread)


def _moe_invert(pos_row, n_tokens):
    return pl.pallas_call(
        functools.partial(_invert_kernel, n_tokens=n_tokens),
        out_shape=jax.ShapeDtypeStruct((MAX_TILES, MOE_TILE), jnp.int32),
        compiler_params=pltpu.CompilerParams(vmem_limit_bytes=VMEM_LIMIT),
        name="moe_invert",
    )(pos_row)


TILE_ROWS = MOE_TILE * ROW_PITCH
GATHER_DEPTH = 3


def _experts_kernel(starts_ref, inv_hbm, hx_hbm, g_ref, wg_ref, wu_ref, wd_ref, y_hbm,
                    xbuf, ybuf, inv_sm, gsem, ysem, isem, wg_bf, wu_bf, wd_bf, *, n_slots):
    e = pl.program_id(0)
    t0, t1 = starts_ref[e], starts_ref[e + 1]
    n_tiles = starts_ref[N_EXPERTS]
    last = n_tiles - 1

    def inv_copy(t, s):
        return pltpu.make_async_copy(inv_hbm.at[jnp.minimum(t, last)], inv_sm.at[s], isem.at[s])

    def y_copy(t, s):
        return pltpu.make_async_copy(ybuf.at[s], y_hbm.at[pl.ds(t * TILE_ROWS, TILE_ROWS)], ysem.at[s])

    def gather_start(s, r):
        _row_copy(hx_hbm, inv_sm[s, 0, r], xbuf.at[s], r, gsem.at[s]).start()

    @pl.when((e == 0) & (n_tiles > 0))
    def _():
        for s in range(GATHER_DEPTH):
            inv_copy(s, s).start()
        for s in range(GATHER_DEPTH - 1):
            inv_copy(s, s).wait()

            def body(r, carry, s=s):
                gather_start(s, r)
                return carry
            lax.fori_loop(0, MOE_TILE, body, 0, unroll=8)

    @pl.when(t1 > t0)
    def _():
        wg_bf[...] = wg_ref[...].astype(BF16)
        wu_bf[...] = wu_ref[...].astype(BF16)
        wd_bf[...] = wd_ref[...].astype(BF16)

        def tile(t, carry):
            s, s_req, s_y = t % GATHER_DEPTH, (t + GATHER_DEPTH - 1) % GATHER_DEPTH, t % 2
            inv_copy(t + GATHER_DEPTH, s).start()
            inv_copy(t + GATHER_DEPTH - 1, s_req).wait()
            _rows_wait(hx_hbm, xbuf.at[s], MOE_TILE, gsem.at[s])

            @pl.when(t >= 2)
            def _():
                y_copy(t - 2, s_y).wait()

            x = _load_rows(xbuf.at[s], MOE_TILE)
            for r in range(MOE_TILE):
                gather_start(s_req, r)
            xn = _rms(x, g_ref[...]).astype(BF16)
            hg = jnp.dot(xn, wg_bf[...], preferred_element_type=F32)
            hu = jnp.dot(xn, wu_bf[...], preferred_element_type=F32)
            hid = (hg / (1.0 + jnp.exp(-hg)) * hu).astype(BF16)
            _store_rows(ybuf.at[s_y], jnp.dot(hid, wd_bf[...], preferred_element_type=F32))
            y_copy(t, s_y).start()
            return carry

        lax.fori_loop(t0, t1, tile, 0)

    @pl.when(e == pl.num_programs(0) - 1)
    def _():
        @pl.when(n_tiles > 0)
        def _():
            for ahead in range(GATHER_DEPTH - 1):
                s = (n_tiles + ahead) % GATHER_DEPTH
                _rows_wait(hx_hbm, xbuf.at[s], MOE_TILE, gsem.at[s])
            inv_copy(last, last % GATHER_DEPTH).wait()
            y_copy(last, last % 2).wait()

        @pl.when(n_tiles > 1)
        def _():
            y_copy(last - 1, (last - 1) % 2).wait()

        ybuf[0] = jnp.zeros((TILE_ROWS, LANES), F32)

        def fill(t, carry):
            y_copy(t, 0).start()
            y_copy(t, 0).wait()
            return carry

        lax.fori_loop(n_tiles, n_slots, fill, 0)


def _moe_experts(tile_tab, inv, hx_all, g_ffn, w_gate, w_up, w_down, layer, n_steps):
    d = D_MODEL

    def widx(e, tab):
        return (layer, e, 0, 0)

    grid_spec = pltpu.PrefetchScalarGridSpec(
        num_scalar_prefetch=1,
        grid=(N_EXPERTS,),
        in_specs=[pl.BlockSpec(memory_space=pl.ANY),
                  pl.BlockSpec(memory_space=pl.ANY),
                  pl.BlockSpec((1, d), lambda e, tab: (0, 0)),
                  pl.BlockSpec((None, None, d, D_EXPERT), widx),
                  pl.BlockSpec((None, None, d, D_EXPERT), widx),
                  pl.BlockSpec((None, None, D_EXPERT, d), widx)],
        out_specs=pl.BlockSpec(memory_space=pl.ANY),
        scratch_shapes=[pltpu.VMEM((GATHER_DEPTH, TILE_ROWS, LANES), F32),
                        pltpu.VMEM((2, TILE_ROWS, LANES), F32),
                        pltpu.SMEM((GATHER_DEPTH, 1, MOE_TILE), jnp.int32),
                        pltpu.SemaphoreType.DMA((GATHER_DEPTH,)),
                        pltpu.SemaphoreType.DMA((2,)),
                        pltpu.SemaphoreType.DMA((GATHER_DEPTH,)),
                        pltpu.VMEM((d, D_EXPERT), BF16),
                        pltpu.VMEM((d, D_EXPERT), BF16),
                        pltpu.VMEM((D_EXPERT, d), BF16)],
    )
    return pl.pallas_call(
        functools.partial(_experts_kernel, n_slots=n_steps),
        grid_spec=grid_spec,
        out_shape=jax.ShapeDtypeStruct((n_steps * TILE_ROWS, LANES), F32),
        compiler_params=_cparams(("arbitrary",)),
        name="moe_experts",
    )(tile_tab, inv, hx_all, g_ffn, w_gate, w_up, w_down)


PLE_COLS = 512


def _combine_kernel(*refs, tm, n_steps):
    lists = [(refs[k], refs[GATHER_DEPTH + k]) for k in range(GATHER_DEPTH)]
    y_hbm, h_ref, route_ref, p_ref, g_ref, wg_ref, wp_ref, o_ref, ybuf, sems = refs[2 * GATHER_DEPTH:]
    i = pl.program_id(0)
    s, s_req = i % GATHER_DEPTH, (i + GATHER_DEPTH - 1) % GATHER_DEPTH

    def request(p0, p1, sl, r):
        _row_copy(y_hbm, p0[0, 0, r], ybuf.at[sl, 0], r, sems.at[sl]).start()
        _row_copy(y_hbm, p1[0, 0, r], ybuf.at[sl, 1], r, sems.at[sl]).start()

    def wait_tile(sl):
        for c in range(2):
            _rows_wait(y_hbm, ybuf.at[sl, c], tm, sems.at[sl])

    @pl.when(i == 0)
    def _():
        for k in range(GATHER_DEPTH - 1):
            def body(r, carry, k=k):
                request(lists[k][0], lists[k][1], k, r)
                return carry
            lax.fori_loop(0, tm, body, 0, unroll=8)

    wait_tile(s)
    route = route_ref[...]
    y0, y1 = _load_rows(ybuf.at[s, 0], tm), _load_rows(ybuf.at[s, 1], tm)
    for r in range(tm):
        request(lists[-1][0], lists[-1][1], s_req, r)
    h2 = _load_rows(h_ref, tm) + (route[:, 2:3] * y0 + route[:, 3:4] * y1)
    xn = _rms(h2, g_ref[...]).astype(BF16)
    pb = p_ref[...].astype(BF16)
    for j in range(h2.shape[1] // PLE_COLS):
        cols = slice(j * PLE_COLS, (j + 1) * PLE_COLS)
        z = jnp.dot(xn, wg_ref[:, cols], preferred_element_type=F32)
        proj = jnp.dot(pb, wp_ref[:, cols], preferred_element_type=F32)
        o_ref[:, cols] = h2[:, cols] + proj * (1.0 / (1.0 + jnp.exp(-z)))

    @pl.when(i == n_steps - 1)
    def _():
        for ahead in range(GATHER_DEPTH - 1):
            wait_tile((n_steps + ahead) % GATHER_DEPTH)


def _moe_combine_ple(pos0, pos1, y_sorted, hx_all, route_all, p, g_ple, w_gate, w_proj, *, layer, tm, rows,
                     row_block0, name):
    d = D_MODEL
    n = rows // tm
    last = n - 1
    ahead = [pl.BlockSpec((1, 1, tm), lambda i, k=k: (jnp.minimum(i + k, last), 0, 0), memory_space=pltpu.SMEM)
             for k in range(GATHER_DEPTH)]
    return pl.pallas_call(
        functools.partial(_combine_kernel, tm=tm, n_steps=n),
        grid=(n,),
        in_specs=ahead + ahead + [
                  pl.BlockSpec(memory_space=pl.ANY),
                  pl.BlockSpec((tm * ROW_PITCH, LANES), lambda i: (row_block0 + i, 0)),
                  pl.BlockSpec((tm, ROUTE_LANES), lambda i: (row_block0 + i, 0)),
                  pl.BlockSpec((None, tm, PLE_DIM), lambda i: (layer, i, 0)),
                  pl.BlockSpec((1, d), lambda i: (0, 0)),
                  pl.BlockSpec((d, d), lambda i: (0, 0)),
                  pl.BlockSpec((PLE_DIM, d), lambda i: (0, 0))],
        out_specs=pl.BlockSpec((tm, d), lambda i: (i, 0)),
        out_shape=jax.ShapeDtypeStruct((rows, d), F32),
        scratch_shapes=[pltpu.VMEM((GATHER_DEPTH, 2, tm * ROW_PITCH, LANES), F32),
                        pltpu.SemaphoreType.DMA((GATHER_DEPTH,))],
        compiler_params=_cparams(("arbitrary",)),
        name=name,
    )(*([pos0] * GATHER_DEPTH + [pos1] * GATHER_DEPTH), y_sorted, hx_all, route_all, p, g_ple, w_gate, w_proj)


def _ret_head(q, k, v, gt, state, gn, lg, c_real):
    c = q.shape[0]
    k = k * (RET_DK ** -0.5)
    ri = lax.broadcasted_iota(jnp.int32, (c, c), 0)
    ci = lax.broadcasted_iota(jnp.int32, (c, c), 1)
    diff = (ri - ci).astype(F32)
    decay = jnp.where(diff >= 0, jnp.exp(lg * jnp.maximum(diff, 0.0)), 0.0)
    idx = lax.broadcasted_iota(jnp.int32, (c, 1), 0).astype(F32)
    qk = lax.dot_general(q.astype(BF16), k.astype(BF16), (((1,), (1,)), ((), ())), preferred_element_type=F32)
    inner = qk * decay
    q_dec = q * jnp.exp(lg * (idx + 1.0))
    k_dec = k * jnp.exp(lg * (c_real - 1.0 - idx))
    vb = v.astype(BF16)
    out = (jnp.dot(inner.astype(BF16), vb, preferred_element_type=F32)
           + jnp.dot(q_dec.astype(BF16), state.astype(BF16), preferred_element_type=F32))
    new_state = math.exp(lg * c_real) * state + lax.dot_general(
        k_dec.astype(BF16), vb, (((0,), (0,)), ((), ())), preferred_element_type=F32)
    mu = jnp.mean(out, axis=-1, keepdims=True)
    cen = out - mu
    var = jnp.mean(cen * cen, axis=-1, keepdims=True)
    on = cen * lax.rsqrt(var + GN_EPS) * gn
    return gt / (1.0 + jnp.exp(-gt)) * on, new_state


def _ret_prompt_kernel(q_ref, k_ref, v_ref, g_ref, gn_ref, y_ref, so_ref, st_ref):
    c = pl.program_id(1)

    @pl.when(c == 0)
    def _():
        st_ref[...] = jnp.zeros_like(st_ref)

    for h in range(RET_HEADS):
        ks, vs = slice(h * RET_DK, (h + 1) * RET_DK), slice(h * RET_DV, (h + 1) * RET_DV)
        y, new_state = _ret_head(q_ref[:, ks], k_ref[:, ks], v_ref[:, vs], g_ref[:, vs], st_ref[h],
                                 gn_ref[:, vs], RET_LOG_GAMMA[h], RET_CHUNK)
        st_ref[h] = new_state
        y_ref[:, vs] = y.astype(y_ref.dtype)

    @pl.when(c == pl.num_programs(1) - 1)
    def _():
        so_ref[0] = st_ref[...]


def _ret_prompt(qkvg, g_norm, batch):
    rows = qkvg.shape[0]
    nc = rows // batch // RET_CHUNK
    return pl.pallas_call(
        _ret_prompt_kernel,
        grid=(batch, nc),
        in_specs=[pl.BlockSpec((RET_CHUNK, RET_K_WIDTH), lambda b, c: (b * nc + c, 0)),
                  pl.BlockSpec((RET_CHUNK, RET_K_WIDTH), lambda b, c: (b * nc + c, 1)),
                  pl.BlockSpec((RET_CHUNK, RET_V_WIDTH), lambda b, c: (b * nc + c, 1)),
                  pl.BlockSpec((RET_CHUNK, RET_V_WIDTH), lambda b, c: (b * nc + c, 2)),
                  pl.BlockSpec((1, RET_V_WIDTH), lambda b, c: (0, 0))],
        out_specs=[pl.BlockSpec((RET_CHUNK, RET_V_WIDTH), lambda b, c: (b * nc + c, 0)),
                   pl.BlockSpec((1, RET_HEADS, RET_DK, RET_DV), lambda b, c: (b, 0, 0, 0))],
        out_shape=[jax.ShapeDtypeStruct((rows, RET_V_WIDTH), BF16),
                   jax.ShapeDtypeStruct((batch, RET_HEADS, RET_DK, RET_DV), F32)],
        scratch_shapes=[pltpu.VMEM((RET_HEADS, RET_DK, RET_DV), F32)],
        compiler_params=_cparams(("parallel", "arbitrary")),
        name="ret_prompt",
    )(qkvg, qkvg, qkvg, qkvg, g_norm)


RET_DEC_GROUP = 2


def _ret_decode_kernel(x_ref, s_ref, gn_ref, y_ref, so_ref, *, t_new):
    for b in range(x_ref.shape[0]):
        x = x_ref[b]
        for h in range(RET_HEADS):
            q = x[:, h * RET_DK:(h + 1) * RET_DK]
            k = x[:, RET_K_WIDTH + h * RET_DK:RET_K_WIDTH + (h + 1) * RET_DK]
            v0 = 2 * RET_K_WIDTH + h * RET_DV
            g0 = 2 * RET_K_WIDTH + RET_V_WIDTH + h * RET_DV
            y, new_state = _ret_head(q, k, x[:, v0:v0 + RET_DV], x[:, g0:g0 + RET_DV], s_ref[b, h],
                                     gn_ref[:, h * RET_DV:(h + 1) * RET_DV], RET_LOG_GAMMA[h], t_new)
            so_ref[b, h] = new_state
            y_ref[b, :, h * RET_DV:(h + 1) * RET_DV] = y


def _ret_decode(qkvg, state, g_norm, t_new):
    b = qkvg.shape[0]
    g = RET_DEC_GROUP if b % RET_DEC_GROUP == 0 else 1
    return pl.pallas_call(
        functools.partial(_ret_decode_kernel, t_new=t_new),
        grid=(b // g,),
        in_specs=[pl.BlockSpec((g, DEC_ROWS, RET_IN_WIDTH), lambda i: (i, 0, 0)),
                  pl.BlockSpec((g, RET_HEADS, RET_DK, RET_DV), lambda i: (i, 0, 0, 0)),
                  pl.BlockSpec((1, RET_V_WIDTH), lambda i: (0, 0))],
        out_specs=[pl.BlockSpec((g, DEC_ROWS, RET_V_WIDTH), lambda i: (i, 0, 0)),
                   pl.BlockSpec((g, RET_HEADS, RET_DK, RET_DV), lambda i: (i, 0, 0, 0))],
        out_shape=[jax.ShapeDtypeStruct((b, DEC_ROWS, RET_V_WIDTH), F32),
                   jax.ShapeDtypeStruct((b, RET_HEADS, RET_DK, RET_DV), F32)],
        compiler_params=_cparams(("parallel",)),
        name="ret_decode",
    )(qkvg, state, g_norm)


def _pad_rows(x, b, t):
    return jnp.pad(x.reshape(b, t, x.shape[-1]), ((0, 0), (0, DEC_ROWS - t), (0, 0)))


def _moe_layer(routed, p_prompt, p_sample, g_ffn, g_ple, w_gate_e, w_up_e, w_down_e, w_ple_gate,
               w_ple_proj, layer, n_prompt, n_sample):
    route_all, hx_all, route_t = routed
    n_tok = n_prompt + n_sample
    n_pairs = PAIR_ROWS * LANES
    eid = route_t[:2].astype(jnp.int32).reshape(-1)
    eid = jnp.pad(eid, (0, n_pairs - 2 * n_tok), constant_values=-1).reshape(PAIR_ROWS, LANES)
    pos, tile_tab = _moe_meta(eid)
    inv = _moe_invert(pos.reshape(1, n_pairs), n_tok).reshape(MAX_TILES, 1, MOE_TILE)
    max_tiles = (2 * n_tok + N_EXPERTS * (MOE_TILE - 1)) // MOE_TILE
    y_sorted = _moe_experts(tile_tab[0], inv, hx_all, g_ffn, w_gate_e, w_up_e, w_down_e, layer, max_tiles)
    pos2 = pos.reshape(-1)[:2 * n_tok].reshape(2, n_tok)
    tm_p, tm_s = 256, n_sample
    outs = []
    for lo, rows, tm, p, nm in ((0, n_prompt, tm_p, p_prompt, "combine_prompt"),
                                (n_prompt, n_sample, tm_s, p_sample, "combine_sample")):
        pos0 = pos2[0, lo:lo + rows].reshape(rows // tm, 1, tm)
        pos1 = pos2[1, lo:lo + rows].reshape(rows // tm, 1, tm)
        outs.append(_moe_combine_ple(pos0, pos1, y_sorted, hx_all, route_all, p, g_ple, w_ple_gate, w_ple_proj,
                                     layer=layer, tm=tm, rows=rows, row_block0=lo // tm, name=nm))
    return outs


def kernel(x_prompt, x_sample, cache_k_swa, cache_v_swa, state_ret, p_prompt, p_sample, g_mix, g_ffn, g_ple,
           swa_w_qkv, swa_g_q, swa_g_k, swa_sinks, swa_w_o, ret_w_in, ret_g_norm, ret_w_o, moe_w_group,
           moe_b_group, moe_w_expert, moe_b_expert, moe_w_gate, moe_w_up, moe_w_down, ple_w_proj, ple_w_gate):
    batch, seq, d = x_prompt.shape
    dec_b, dec_t, _ = x_sample.shape
    n_prompt, n_sample = batch * seq, dec_b * dec_t
    n_tok = n_prompt + n_sample
    assert 2 * n_tok <= PAIR_ROWS * LANES and 2 * n_tok + N_EXPERTS * (MOE_TILE - 1) <= (MAX_TILES - 1) * MOE_TILE
    assert dec_t <= DEC_ROWS and n_prompt % 2048 == 0 and n_sample % 8 == 0

    xp = x_prompt.reshape(n_prompt, d)
    xs = x_sample.reshape(n_sample, d)
    pp = p_prompt.reshape(DEPTH, n_prompt, PLE_DIM)
    ps = p_sample.reshape(DEPTH, n_sample, PLE_DIM)
    row = lambda a: a.reshape(1, -1)

    def router_w(i):
        w = jnp.concatenate([moe_w_group[i], moe_w_expert[i]], axis=1)
        b = jnp.concatenate([moe_b_group[i], moe_b_expert[i]])
        pad = ROUTE_LANES - w.shape[1]
        return jnp.pad(w, ((0, 0), (0, pad))), jnp.pad(b, (0, pad)).reshape(1, ROUTE_LANES)

    g_q2, g_k2 = row(jnp.tile(swa_g_q[0], 2)), row(jnp.tile(swa_g_k[0], 2))
    w_r, b_r = router_w(0)
    qkv_p = _norm_proj(xp, row(g_mix[0]), swa_w_qkv[0], tm=2048, tn=256, hi=False, name="qkv_prompt")
    sinks2 = swa_sinks[0] * LOG2E
    o_p, k_p, v_p = _swa_prompt(qkv_p, g_q2, g_k2, sinks2, batch)

    qkv_s = _norm_proj(xs, row(g_mix[0]), swa_w_qkv[0], tm=n_sample, tn=512, hi=True, name="qkv_sample")
    ck = cache_k_swa[0].reshape(dec_b, WINDOW, KV_WIDTH)
    cv = cache_v_swa[0].reshape(dec_b, WINDOW, KV_WIDTH)
    o_s, k_s, v_s = _swa_decode(_pad_rows(qkv_s, dec_b, dec_t), ck, cv, g_q2, g_k2, sinks2, dec_t)
    o_s = o_s[:, :dec_t].reshape(n_sample, Q_WIDTH)
    tail = _outproj_router(o_s, swa_w_o[0], xs, row(g_ffn[0]), w_r, b_r, tm=n_sample, tk=2048, hi=True,
                           tail=None, name="swa_out_sample")
    routed = _outproj_router(o_p, swa_w_o[0].astype(BF16), xp, row(g_ffn[0]), w_r.astype(BF16), b_r, tm=512,
                             tk=2048, hi=False, tail=tail, name="swa_out_prompt")

    h_p, h_s = _moe_layer(routed, pp, ps, row(g_ffn[0]), row(g_ple[0]), moe_w_gate, moe_w_up, moe_w_down,
                          ple_w_gate[0].astype(BF16), ple_w_proj[0].astype(BF16), 0, n_prompt, n_sample)

    w_o = ret_w_o[0].astype(BF16)
    w_r, b_r = router_w(1)
    w_r = w_r.astype(BF16)
    qkvg_p = _norm_proj(h_p, row(g_mix[1]), ret_w_in[0], tm=2048, tn=256, hi=False, name="ret_in_prompt")
    y_p, s_p = _ret_prompt(qkvg_p, row(ret_g_norm[0]), batch)

    qkvg_s = _norm_proj(h_s, row(g_mix[1]), ret_w_in[0], tm=n_sample, tn=1024, hi=False, name="ret_in_sample")
    y_s, s_s = _ret_decode(_pad_rows(qkvg_s, dec_b, dec_t), state_ret[0], row(ret_g_norm[0]), dec_t)
    y_s = y_s[:, :dec_t].reshape(n_sample, RET_V_WIDTH)
    tail = _outproj_router(y_s, w_o, h_s, row(g_ffn[1]), w_r, b_r, tm=n_sample, tk=2048, hi=False, tail=None,
                           name="ret_out_sample")
    routed = _outproj_router(y_p, w_o, h_p, row(g_ffn[1]), w_r, b_r, tm=512, tk=2048, hi=False, tail=tail,
                             name="ret_out_prompt")

    y_prompt, y_sample = _moe_layer(routed, pp, ps, row(g_ffn[1]), row(g_ple[1]), moe_w_gate, moe_w_up, moe_w_down,
                                    ple_w_gate[1].astype(BF16), ple_w_proj[1].astype(BF16), 1, n_prompt, n_sample)

    kv_shape = (1, -1, WINDOW, SWA_KV_HEADS, SWA_HEAD_DIM)
    return (y_prompt.reshape(batch, seq, d), y_sample.reshape(dec_b, dec_t, d),
            k_p.reshape(kv_shape), v_p.reshape(kv_shape), s_p[None],
            k_s.reshape(kv_shape), v_s.reshape(kv_shape), s_s[None])
```

```python
import functools
import math

import jax
import jax.numpy as jnp
from jax import lax
from jax.experimental import pallas as pl
from jax.experimental.pallas import tpu as pltpu

D_MODEL = 2048
DEPTH = 2
SWA_HEADS = 32
SWA_KV_HEADS = 8
SWA_HEAD_DIM = 64
Q_WIDTH = SWA_HEADS * SWA_HEAD_DIM
KV_WIDTH = SWA_KV_HEADS * SWA_HEAD_DIM
QKV_WIDTH = Q_WIDTH + 2 * KV_WIDTH
WINDOW = 128
ATTN_SCALE = SWA_HEAD_DIM ** -0.5
LOG2E = math.log2(math.e)
RET_HEADS = 8
RET_DK = D_MODEL // RET_HEADS
RET_DV = 2 * D_MODEL // RET_HEADS
RET_K_WIDTH = RET_HEADS * RET_DK
RET_V_WIDTH = RET_HEADS * RET_DV
RET_IN_WIDTH = 2 * RET_K_WIDTH + 2 * RET_V_WIDTH
RET_CHUNK = 256
N_GROUPS = 4
EXPERTS_PER_GROUP = 8
N_EXPERTS = N_GROUPS * EXPERTS_PER_GROUP
D_EXPERT = D_MODEL // 4
PLE_DIM = 256
NORM_EPS = 1e-6
GN_EPS = 1e-5
NEG_INF = -1e30

LANES = 128
ROUTE_LANES = LANES
EXPERT_LANE0 = N_GROUPS
MOE_TILE = 256
PAIR_ROWS = 256
MAX_TILES = LANES
VMEM_LIMIT = 60 * 1024 * 1024
ROW_TILES = D_MODEL // LANES
ROW_PITCH = ROW_TILES + 4

F32 = jnp.float32
BF16 = jnp.bfloat16

ALIBI_SLOPES = tuple(2.0 ** (-8.0 * (h + 1) / SWA_HEADS) for h in range(SWA_HEADS))
RET_LOG_GAMMA = tuple(math.log(1.0 - 2.0 ** (-5.0 - h)) for h in range(RET_HEADS))


def _cparams(sem):
    return pltpu.CompilerParams(dimension_semantics=sem, vmem_limit_bytes=VMEM_LIMIT)


def _mm_dtype(hi):
    return F32 if hi else BF16


def _mm(a, b, hi):
    if hi:
        return _mm3(a, b, ((1,), (0,)))
    return jnp.dot(a.astype(BF16), b.astype(BF16), preferred_element_type=F32)


def _rms(x, g):
    return x * lax.rsqrt(jnp.mean(x * x, axis=-1, keepdims=True) + NORM_EPS) * g


def _store_rows(ref, x):
    n = x.shape[0]
    for s in range(ROW_PITCH):
        piece = x[:, s * LANES:(s + 1) * LANES] if s < ROW_TILES else jnp.zeros((n, LANES), F32)
        ref[pl.ds(s, n, stride=ROW_PITCH), :] = piece


def _load_rows(ref, n):
    return jnp.concatenate([ref[pl.ds(s, n, stride=ROW_PITCH), :] for s in range(ROW_TILES)], axis=1)


def _row_copy(src_hbm, src_row, dst, dst_row, sem):
    return pltpu.make_async_copy(src_hbm.at[pl.ds(src_row * ROW_PITCH, ROW_TILES)],
                                 dst.at[pl.ds(dst_row * ROW_PITCH, ROW_TILES)], sem)


def _rows_wait(src_hbm, dst, n, sem):
    pltpu.make_async_copy(src_hbm.at[pl.ds(0, n * ROW_TILES)], dst.at[pl.ds(0, n * ROW_TILES)], sem).wait()


def _proj_kernel(x_ref, g_ref, w_ref, o_ref, a_ref, *, hi):
    @pl.when(pl.program_id(1) == 0)
    def _():
        a_ref[...] = _rms(x_ref[...], g_ref[...]).astype(a_ref.dtype)

    o_ref[...] = _mm(a_ref[...], w_ref[...], hi).astype(o_ref.dtype)


def _norm_proj(x, g, w, *, tm, tn, hi, name):
    m, k = x.shape
    n = w.shape[1]
    return pl.pallas_call(
        functools.partial(_proj_kernel, hi=hi),
        grid=(m // tm, n // tn),
        in_specs=[pl.BlockSpec((tm, k), lambda i, j: (i, 0)),
                  pl.BlockSpec((1, k), lambda i, j: (0, 0)),
                  pl.BlockSpec((k, tn), lambda i, j: (0, j))],
        out_specs=pl.BlockSpec((tm, tn), lambda i, j: (i, j)),
        out_shape=jax.ShapeDtypeStruct((m, n), F32),
        scratch_shapes=[pltpu.VMEM((tm, k), _mm_dtype(hi))],
        compiler_params=_cparams(("parallel", "arbitrary")),
        name=name,
    )(x, g, w)


def _head_norm(x, g2):
    lane = lax.broadcasted_iota(jnp.int32, (1, LANES), 1)
    lo = lane < SWA_HEAD_DIM
    outs = []
    for j in range(x.shape[1] // LANES):
        xb = x[:, j * LANES:(j + 1) * LANES]
        x2 = xb * xb
        s_lo = jnp.sum(jnp.where(lo, x2, 0.0), axis=-1, keepdims=True)
        s_hi = jnp.sum(jnp.where(lo, 0.0, x2), axis=-1, keepdims=True)
        r = jnp.where(lo, lax.rsqrt(s_lo * (1.0 / SWA_HEAD_DIM) + NORM_EPS),
                      lax.rsqrt(s_hi * (1.0 / SWA_HEAD_DIM) + NORM_EPS))
        outs.append(xb * r * g2)
    return jnp.concatenate(outs, axis=1)


def _mm3(a, b, dims):
    (ca,), (cb,) = dims
    assert ca == 1
    m, n = a.shape[0], b.shape[1 - cb]
    a_hi, b_hi = a.astype(BF16), b.astype(BF16)
    a_lo = (a - a_hi.astype(F32)).astype(BF16)
    b_lo = (b - b_hi.astype(F32)).astype(BF16)
    a2 = jnp.concatenate([a_hi, a_lo], axis=0)
    b2 = jnp.concatenate([b_hi, b_lo], axis=1 - cb)
    r = lax.dot_general(a2, b2, (dims, ((), ())), preferred_element_type=F32)
    return r[:m, :n] + (r[:m, n:] + r[m:, :n])


def _attn_heads(q, g_q2, kn, v, bias_ref, sink_ref, hi, emit):
    nq, nk = q.shape[0], kn.shape[0]
    cdt = _mm_dtype(hi)
    lane = lax.broadcasted_iota(jnp.int32, (1, LANES), 1)
    lo = lane < SWA_HEAD_DIM
    group = SWA_HEADS // SWA_KV_HEADS
    ones_bd = (lo == (lax.broadcasted_iota(jnp.int32, (2 * nk, LANES), 0) < nk)).astype(cdt)
    for m in range(KV_WIDTH // LANES):
        kb = kn[:, m * LANES:(m + 1) * LANES]
        vb = v[:, m * LANES:(m + 1) * LANES]
        kr = pltpu.roll(kb, SWA_HEAD_DIM, 1)
        vr = pltpu.roll(vb, SWA_HEAD_DIM, 1)
        for half in range(2):
            kv = 2 * m + half
            k_lo, k_hi = (kb, kr) if half == 0 else (kr, kb)
            v_lo, v_hi = (vb, vr) if half == 0 else (vr, vb)
            kbd = jnp.concatenate([jnp.where(lo, k_lo, 0.0), jnp.where(lo, 0.0, k_hi)], axis=0).astype(cdt)
            vbd = jnp.concatenate([jnp.where(lo, v_lo, 0.0), jnp.where(lo, 0.0, v_hi)], axis=0).astype(cdt)
            c0 = kv * group * SWA_HEAD_DIM
            qs = _head_norm(q[:, c0:c0 + 2 * LANES], g_q2) * (ATTN_SCALE * LOG2E)
            q2 = jnp.concatenate([qs[:, :LANES], qs[:, LANES:]], axis=0).astype(cdt)
            if hi:
                s_all = _mm3(q2, kbd, ((1,), (1,)))
            else:
                s_all = lax.dot_general(q2, kbd, (((1,), (1,)), ((), ())), preferred_element_type=F32)
            e_rows, sink_e = [], []
            for r in range(2):
                e_cols = []
                for c in range(2):
                    h = kv * group + 2 * r + c
                    s = s_all[r * nq:(r + 1) * nq, c * nk:(c + 1) * nk] + bias_ref[h]
                    sk = sink_ref[h]
                    mx = jnp.maximum(jnp.max(s, axis=-1, keepdims=True), sk)
                    e_cols.append(jnp.exp2(s - mx).astype(cdt))
                    sink_e.append(jnp.exp2(sk - mx))
                e_rows.append(jnp.concatenate(e_cols, axis=1))
            p2 = jnp.concatenate(e_rows, axis=0)
            if hi:
                o2 = _mm3(p2, vbd, ((1,), (0,)))
                sums = [jnp.sum(p2[r * nq:(r + 1) * nq, c * nk:(c + 1) * nk], axis=-1, keepdims=True)
                        for r in range(2) for c in range(2)]
            else:
                o_den = jnp.dot(p2, jnp.concatenate([vbd, ones_bd], axis=1), preferred_element_type=F32)
                o2, den2 = o_den[:, :LANES], o_den[:, LANES:]
            for r in range(2):
                sums_r = jnp.where(lo, sums[2 * r], sums[2 * r + 1]) if hi else den2[r * nq:(r + 1) * nq]
                den = sums_r + jnp.where(lo, sink_e[2 * r], sink_e[2 * r + 1])
                emit(2 * kv + r, o2[r * nq:(r + 1) * nq] / den)


def _swa_bias(nq, q_pos0, n_keys, first_key):
    t = jnp.arange(nq)[:, None] + q_pos0
    s = jnp.arange(2 * WINDOW)[None, :]
    dist = t - s
    valid = (dist >= 0) & (dist <= WINDOW) & (s >= first_key) & (s < n_keys)
    slopes = jnp.asarray(ALIBI_SLOPES, F32)[:, None, None]
    return jnp.where(valid[None], -slopes * dist.astype(F32)[None] * LOG2E, NEG_INF)


def _swa_prompt_kernel(sink_ref, q_ref, k_ref, v_ref, gq_ref, gk_ref, bias_ref, o_ref, ko_ref, vo_ref, kprev, vprev):
    n = pl.program_id(1)

    @pl.when(n == 0)
    def _():
        kprev[...] = jnp.zeros_like(kprev)
        vprev[...] = jnp.zeros_like(vprev)

    kn = _head_norm(k_ref[...], gk_ref[...])
    v = v_ref[...]
    kcat = jnp.concatenate([kprev[...], kn], axis=0)
    vcat = jnp.concatenate([vprev[...], v], axis=0)

    def emit(j, block):
        o_ref[:, j * LANES:(j + 1) * LANES] = block.astype(o_ref.dtype)

    _attn_heads(q_ref[...], gq_ref[...], kcat, vcat, bias_ref, sink_ref, False, emit)
    kprev[...] = kn
    vprev[...] = v

    @pl.when(n == pl.num_programs(1) - 1)
    def _():
        ko_ref[0] = kn
        vo_ref[0] = v


def _swa_prompt(qkv, g_q, g_k, sinks, batch):
    rows = qkv.shape[0]
    nb = rows // batch // WINDOW
    kblk = Q_WIDTH // KV_WIDTH
    bias = jnp.stack([_swa_bias(WINDOW, WINDOW, 2 * WINDOW, WINDOW), _swa_bias(WINDOW, WINDOW, 2 * WINDOW, 0)])
    return pl.pallas_call(
        _swa_prompt_kernel,
        grid=(batch, nb),
        in_specs=[pl.BlockSpec(memory_space=pltpu.SMEM),
                  pl.BlockSpec((WINDOW, Q_WIDTH), lambda b, n: (b * nb + n, 0)),
                  pl.BlockSpec((WINDOW, KV_WIDTH), lambda b, n: (b * nb + n, kblk)),
                  pl.BlockSpec((WINDOW, KV_WIDTH), lambda b, n: (b * nb + n, kblk + 1)),
                  pl.BlockSpec((1, LANES), lambda b, n: (0, 0)),
                  pl.BlockSpec((1, LANES), lambda b, n: (0, 0)),
                  pl.BlockSpec((None, SWA_HEADS, WINDOW, 2 * WINDOW), lambda b, n: (jnp.minimum(n, 1), 0, 0, 0))],
        out_specs=[pl.BlockSpec((WINDOW, Q_WIDTH), lambda b, n: (b * nb + n, 0)),
                   pl.BlockSpec((1, WINDOW, KV_WIDTH), lambda b, n: (b, 0, 0)),
                   pl.BlockSpec((1, WINDOW, KV_WIDTH), lambda b, n: (b, 0, 0))],
        out_shape=[jax.ShapeDtypeStruct((rows, Q_WIDTH), BF16),
                   jax.ShapeDtypeStruct((batch, WINDOW, KV_WIDTH), F32),
                   jax.ShapeDtypeStruct((batch, WINDOW, KV_WIDTH), F32)],
        scratch_shapes=[pltpu.VMEM((WINDOW, KV_WIDTH), F32), pltpu.VMEM((WINDOW, KV_WIDTH), F32)],
        compiler_params=_cparams(("parallel", "arbitrary")),
        name="swa_prompt",
    )(sinks, qkv, qkv, qkv, g_q, g_k, bias)


DEC_ROWS = 8
DEC_GROUP = 4


def _swa_decode_kernel(sink_ref, qkv_ref, ck_ref, cv_ref, gq_ref, gk_ref, bias_ref, o_ref, ko_ref, vo_ref, *, t_new):
    fill = jnp.zeros((WINDOW - DEC_ROWS, KV_WIDTH), F32)
    is_new = lax.broadcasted_iota(jnp.int32, (WINDOW, 1), 0) >= WINDOW - t_new

    def slide(cache, new):
        new_at_end = pltpu.roll(jnp.concatenate([new, fill], axis=0), WINDOW - t_new, 0)
        return jnp.where(is_new, new_at_end, pltpu.roll(cache, WINDOW - t_new, 0))

    for b in range(qkv_ref.shape[0]):
        x = qkv_ref[b]
        kn = _head_norm(x[:, Q_WIDTH:Q_WIDTH + KV_WIDTH], gk_ref[...])
        v = x[:, Q_WIDTH + KV_WIDTH:]
        ck, cv = ck_ref[b], cv_ref[b]
        kcat = jnp.concatenate([ck, kn, fill], axis=0)
        vcat = jnp.concatenate([cv, v, fill], axis=0)

        def emit(j, block, b=b):
            o_ref[b, :, j * LANES:(j + 1) * LANES] = block

        _attn_heads(x[:, :Q_WIDTH], gq_ref[...], kcat, vcat, bias_ref, sink_ref, True, emit)
        ko_ref[b] = slide(ck, kn)
        vo_ref[b] = slide(cv, v)


def _swa_decode(qkv, cache_k, cache_v, g_q, g_k, sinks, t_new):
    b = qkv.shape[0]
    g = DEC_GROUP if b % DEC_GROUP == 0 else 1
    bias = _swa_bias(DEC_ROWS, WINDOW, WINDOW + t_new, 0)
    return pl.pallas_call(
        functools.partial(_swa_decode_kernel, t_new=t_new),
        grid=(b // g,),
        in_specs=[pl.BlockSpec(memory_space=pltpu.SMEM),
                  pl.BlockSpec((g, DEC_ROWS, QKV_WIDTH), lambda i: (i, 0, 0)),
                  pl.BlockSpec((g, WINDOW, KV_WIDTH), lambda i: (i, 0, 0)),
                  pl.BlockSpec((g, WINDOW, KV_WIDTH), lambda i: (i, 0, 0)),
                  pl.BlockSpec((1, LANES), lambda i: (0, 0)),
                  pl.BlockSpec((1, LANES), lambda i: (0, 0)),
                  pl.BlockSpec((SWA_HEADS, DEC_ROWS, 2 * WINDOW), lambda i: (0, 0, 0))],
        out_specs=[pl.BlockSpec((g, DEC_ROWS, Q_WIDTH), lambda i: (i, 0, 0)),
                   pl.BlockSpec((g, WINDOW, KV_WIDTH), lambda i: (i, 0, 0)),
                   pl.BlockSpec((g, WINDOW, KV_WIDTH), lambda i: (i, 0, 0))],
        out_shape=[jax.ShapeDtypeStruct((b, DEC_ROWS, Q_WIDTH), F32),
                   jax.ShapeDtypeStruct((b, WINDOW, KV_WIDTH), F32),
                   jax.ShapeDtypeStruct((b, WINDOW, KV_WIDTH), F32)],
        compiler_params=_cparams(("parallel",)),
        name="swa_decode",
    )(sinks, qkv, cache_k, cache_v, g_q, g_k, bias)


def _route(logits):
    lane = lax.broadcasted_iota(jnp.int32, logits.shape, 1).astype(F32)
    big = float(ROUTE_LANES)
    lg = jnp.where(lane < N_GROUPS, logits, NEG_INF)
    mg = jnp.max(lg, axis=-1, keepdims=True)
    gsel = jnp.min(jnp.where(lg == mg, lane, big), axis=-1, keepdims=True)
    pg_sel = 1.0 / jnp.sum(jnp.where(lane < N_GROUPS, jnp.exp(logits - mg), 0.0), axis=-1, keepdims=True)
    first = EXPERT_LANE0 + EXPERTS_PER_GROUP * gsel
    le = jnp.where((lane >= first) & (lane < first + EXPERTS_PER_GROUP), logits, NEG_INF)
    m1 = jnp.max(le, axis=-1, keepdims=True)
    i1 = jnp.min(jnp.where(le == m1, lane, big), axis=-1, keepdims=True)
    le2 = jnp.where(lane == i1, NEG_INF, le)
    m2 = jnp.max(le2, axis=-1, keepdims=True)
    i2 = jnp.min(jnp.where(le2 == m2, lane, big), axis=-1, keepdims=True)
    t = jnp.exp(m2 - m1)
    w0 = pg_sel / (1.0 + t)
    w1 = pg_sel * t / (1.0 + t)
    e0 = i1 - EXPERT_LANE0
    e1 = i2 - EXPERT_LANE0
    return jnp.where(lane == 0, e0, jnp.where(lane == 1, e1, jnp.where(lane == 2, w0, jnp.where(lane == 3, w1, 0.0))))


ROUTE_ROWS = 8


def _outproj_kernel(*refs, hi, n_tail, n_k):
    o_ref, w_ref, h_ref, g_ref, wr_ref, br_ref = refs[:6]
    route_ref, h1x_ref, rt_ref = refs[6 + n_tail:9 + n_tail]
    acc_ref = refs[-1]
    i, k = pl.program_id(0), pl.program_id(1)
    n_main = pl.num_programs(0) - (1 if n_tail else 0)

    def finish(h1):
        _store_rows(h1x_ref, h1)
        logits = _mm(_rms(h1, g_ref[...]), wr_ref[...], hi) + br_ref[...]
        rec = _route(logits)
        route_ref[...] = rec
        rt_ref[...] = rec.T[:ROUTE_ROWS]

    @pl.when(i < n_main)
    def _():
        def part():
            return _mm(o_ref[...], w_ref[...], hi)

        if n_k == 1:
            finish(h_ref[...] + part())
        else:
            @pl.when(k == 0)
            def _():
                acc_ref[...] = h_ref[...]

            acc_ref[...] += part()

            @pl.when(k == n_k - 1)
            def _():
                finish(acc_ref[...])

    if n_tail:
        @pl.when((i == n_main) & (k == n_k - 1))
        def _():
            for src, dst in zip(refs[6:8], (route_ref, h1x_ref)):
                rows = src.shape[0]
                dst[:rows] = src[...]
                dst[rows:] = jnp.zeros((dst.shape[0] - rows, dst.shape[1]), F32)
            cols = refs[8].shape[1]
            rt_ref[:, :cols] = refs[8][...]
            rt_ref[:, cols:] = jnp.zeros((ROUTE_ROWS, rt_ref.shape[1] - cols), F32)


def _outproj_router(o, w, h, g_ffn, w_r, b_r, *, tm, tk, hi, tail, name):
    m, kdim = o.shape
    d = w.shape[1]
    n_main, n_k = m // tm, kdim // tk
    n_tail = 0 if tail is None else 3
    rows_i = lambda i: jnp.minimum(i, n_main - 1)
    k_i = lambda i, k: jnp.where(i < n_main, k, n_k - 1)
    in_specs = [pl.BlockSpec((tm, tk), lambda i, k: (rows_i(i), k_i(i, k))),
                pl.BlockSpec((tk, d), lambda i, k: (k_i(i, k), 0)),
                pl.BlockSpec((tm, d), lambda i, k: (rows_i(i), 0)),
                pl.BlockSpec((1, d), lambda i, k: (0, 0)),
                pl.BlockSpec((d, ROUTE_LANES), lambda i, k: (0, 0)),
                pl.BlockSpec((1, ROUTE_LANES), lambda i, k: (0, 0))]
    args = [o, w, h, g_ffn, w_r, b_r]
    total_rows = m
    if tail is not None:
        assert tail[0].shape[0] <= tm
        in_specs += [pl.BlockSpec(t.shape, lambda i, k: (0, 0)) for t in tail]
        args += list(tail)
        total_rows += tail[0].shape[0]
    return pl.pallas_call(
        functools.partial(_outproj_kernel, hi=hi, n_tail=n_tail, n_k=n_k),
        grid=(n_main + (1 if n_tail else 0), n_k),
        in_specs=in_specs,
        out_specs=[pl.BlockSpec((tm, ROUTE_LANES), lambda i, k: (i, 0)),
                   pl.BlockSpec((tm * ROW_PITCH, LANES), lambda i, k: (i, 0)),
                   pl.BlockSpec((ROUTE_ROWS, tm), lambda i, k: (0, i))],
        out_shape=[jax.ShapeDtypeStruct((total_rows, ROUTE_LANES), F32),
                   jax.ShapeDtypeStruct((total_rows * ROW_PITCH, LANES), F32),
                   jax.ShapeDtypeStruct((ROUTE_ROWS, total_rows), F32)],
        scratch_shapes=[pltpu.VMEM((tm, d), F32)] if n_k > 1 else [],
        compiler_params=_cparams(("parallel", "arbitrary")),
        name=name,
    )(*args)


def _meta_kernel(eid_ref, pos_ref, tile_ref):
    eid = eid_ref[...]
    r_i = lax.broadcasted_iota(jnp.int32, (LANES, LANES), 0)
    c_i = lax.broadcasted_iota(jnp.int32, (LANES, LANES), 1)
    upper = (r_i <= c_i).astype(BF16)
    ones = jnp.ones((LANES, LANES), BF16)
    rr = lax.broadcasted_iota(jnp.int32, (PAIR_ROWS, PAIR_ROWS), 0)
    cc = lax.broadcasted_iota(jnp.int32, (PAIR_ROWS, PAIR_ROWS), 1)
    below = (cc < rr).astype(BF16)
    lane = lax.broadcasted_iota(jnp.int32, (1, LANES), 1)

    pos = jnp.zeros((PAIR_ROWS, LANES), F32)
    hit = jnp.zeros((PAIR_ROWS, LANES), F32)
    start = jnp.zeros((1, LANES), F32)
    starts = jnp.zeros((1, LANES), F32)
    for e in range(N_EXPERTS):
        mf = (eid == e).astype(F32)
        mb = mf.astype(BF16)
        incl = jnp.dot(mb, upper, preferred_element_type=F32)
        row_tot = jnp.dot(mb, ones, preferred_element_type=F32)
        row_off = jnp.dot(below, row_tot.astype(BF16), preferred_element_type=F32)
        rank = incl - mf + row_off
        cnt = row_off[PAIR_ROWS - 1:PAIR_ROWS, :] + row_tot[PAIR_ROWS - 1:PAIR_ROWS, :]
        padded = jnp.floor((cnt + (MOE_TILE - 1)) * (1.0 / MOE_TILE)) * MOE_TILE
        pos = pos + mf * (start + rank)
        hit = hit + mf
        start = start + padded
        starts = jnp.where(lane > e, start, starts)
    pos_ref[...] = jnp.where(hit > 0, pos, -1.0).astype(jnp.int32)
    tile_ref[...] = jnp.broadcast_to(starts * (1.0 / MOE_TILE), (8, LANES)).astype(jnp.int32)


def _moe_meta(eid_pairs):
    return pl.pallas_call(
        _meta_kernel,
        out_shape=[jax.ShapeDtypeStruct((PAIR_ROWS, LANES), jnp.int32),
                   jax.ShapeDtypeStruct((8, LANES), jnp.int32)],
        compiler_params=pltpu.CompilerParams(vmem_limit_bytes=VMEM_LIMIT),
        name="moe_meta",
    )(eid_pairs)


INV_CHUNK = 2048


def _invert_kernel(pos_ref, inv_ref, *, n_tokens):
    n_pairs = pos_ref.shape[1]
    hi_acc = jnp.zeros((MAX_TILES, MOE_TILE), F32)
    lo_acc = jnp.zeros((MAX_TILES, MOE_TILE), F32)
    hit_acc = jnp.zeros((MAX_TILES, MOE_TILE), F32)
    t_iota = lax.broadcasted_iota(jnp.int32, (MAX_TILES, INV_CHUNK), 0)
    r_iota = lax.broadcasted_iota(jnp.int32, (MOE_TILE, INV_CHUNK), 0)
    for c in range(min(n_pairs, -(-2 * n_tokens // INV_CHUNK) * INV_CHUNK) // INV_CHUNK):
        pos = pos_ref[:, c * INV_CHUNK:(c + 1) * INV_CHUNK]
        p = lax.broadcasted_iota(jnp.int32, (1, INV_CHUNK), 1) + c * INV_CHUNK
        tok = jnp.where(p >= n_tokens, p - n_tokens, p)
        in_tile = (pos >> (MOE_TILE.bit_length() - 1)) == t_iota
        a_hi = jnp.where(in_tile, (tok >> 7).astype(F32), 0.0).astype(BF16)
        a_lo = jnp.where(in_tile, (tok & 127).astype(F32), 0.0).astype(BF16)
        b = ((pos & (MOE_TILE - 1)) == r_iota).astype(BF16)
        dn = (((1,), (1,)), ((), ()))
        hi_acc = hi_acc + lax.dot_general(a_hi, b, dn, preferred_element_type=F32)
        lo_acc = lo_acc + lax.dot_general(a_lo, b, dn, preferred_element_type=F32)
        hit_acc = hit_acc + lax.dot_general(in_tile.astype(F32).astype(BF16), b, dn, preferred_element_type=F32)
    slot = (lax.broadcasted_iota(jnp.int32, (MAX_TILES, MOE_TILE), 0) * MOE_TILE
            + lax.broadcasted_iota(jnp.int32, (MAX_TILES, MOE_TILE), 1))
    spread = slot & ((1 << (n_tokens.bit_length() - 1)) - 1)
    inv_ref[...] = jnp.where(hit_acc > 0, (hi_acc * 128.0 + lo_acc).astype(jnp.int32), spread)


def _moe_invert(pos_row, n_tokens):
    return pl.pallas_call(
        functools.partial(_invert_kernel, n_tokens=n_tokens),
        out_shape=jax.ShapeDtypeStruct((MAX_TILES, MOE_TILE), jnp.int32),
        compiler_params=pltpu.CompilerParams(vmem_limit_bytes=VMEM_LIMIT),
        name="moe_invert",
    )(pos_row)


TILE_ROWS = MOE_TILE * ROW_PITCH
GATHER_DEPTH = 3


def _experts_kernel(starts_ref, inv_hbm, hx_hbm, g_ref, wg_ref, wu_ref, wd_ref, y_hbm,
                    xbuf, ybuf, inv_sm, gsem, ysem, isem, wg_bf, wu_bf, wd_bf, *, n_slots):
    e = pl.program_id(0)
    t0, t1 = starts_ref[e], starts_ref[e + 1]
    n_tiles = starts_ref[N_EXPERTS]
    last = n_tiles - 1

    def inv_copy(t, s):
        return pltpu.make_async_copy(inv_hbm.at[jnp.minimum(t, last)], inv_sm.at[s], isem.at[s])

    def y_copy(t, s):
        return pltpu.make_async_copy(ybuf.at[s], y_hbm.at[pl.ds(t * TILE_ROWS, TILE_ROWS)], ysem.at[s])

    def gather_start(s, r, priority=0):
        _row_copy(hx_hbm, inv_sm[s, 0, r], xbuf.at[s], r, gsem.at[s]).start(priority=priority)

    @pl.when((e == 0) & (n_tiles > 0))
    def _():
        for s in range(GATHER_DEPTH):
            inv_copy(s, s).start()
        for s in range(GATHER_DEPTH - 1):
            inv_copy(s, s).wait()

            def body(r, carry, s=s):
                gather_start(s, r)
                return carry
            lax.fori_loop(0, MOE_TILE, body, 0, unroll=8)

    @pl.when(t1 > t0)
    def _():
        wg_bf[...] = wg_ref[...].astype(BF16)
        wu_bf[...] = wu_ref[...].astype(BF16)
        wd_bf[...] = wd_ref[...].astype(BF16)

        def tile(t, carry):
            s, s_req, s_y = t % GATHER_DEPTH, (t + GATHER_DEPTH - 1) % GATHER_DEPTH, t % 2
            inv_copy(t + GATHER_DEPTH, s).start()
            inv_copy(t + GATHER_DEPTH - 1, s_req).wait()
            _rows_wait(hx_hbm, xbuf.at[s], MOE_TILE, gsem.at[s])

            @pl.when(t >= 2)
            def _():
                y_copy(t - 2, s_y).wait()

            x = _load_rows(xbuf.at[s], MOE_TILE)
            for r in range(MOE_TILE):
                gather_start(s_req, r, priority=r % 2)
            xn = _rms(x, g_ref[...]).astype(BF16)
            hg = jnp.dot(xn, wg_bf[...], preferred_element_type=F32)
            hu = jnp.dot(xn, wu_bf[...], preferred_element_type=F32)
            hid = (hg / (1.0 + jnp.exp(-hg)) * hu).astype(BF16)
            _store_rows(ybuf.at[s_y], jnp.dot(hid, wd_bf[...], preferred_element_type=F32))
            y_copy(t, s_y).start()
            return carry

        lax.fori_loop(t0, t1, tile, 0)

    @pl.when(e == pl.num_programs(0) - 1)
    def _():
        @pl.when(n_tiles > 0)
        def _():
            for ahead in range(GATHER_DEPTH - 1):
                s = (n_tiles + ahead) % GATHER_DEPTH
                _rows_wait(hx_hbm, xbuf.at[s], MOE_TILE, gsem.at[s])
            inv_copy(last, last % GATHER_DEPTH).wait()
            y_copy(last, last % 2).wait()

        @pl.when(n_tiles > 1)
        def _():
            y_copy(last - 1, (last - 1) % 2).wait()

        ybuf[0] = jnp.zeros((TILE_ROWS, LANES), F32)

        def fill(t, carry):
            y_copy(t, 0).start()
            y_copy(t, 0).wait()
            return carry

        lax.fori_loop(n_tiles, n_slots, fill, 0)


def _moe_experts(tile_tab, inv, hx_all, g_ffn, w_gate, w_up, w_down, layer, n_steps):
    d = D_MODEL

    def widx(e, tab):
        return (layer, e, 0, 0)

    grid_spec = pltpu.PrefetchScalarGridSpec(
        num_scalar_prefetch=1,
        grid=(N_EXPERTS,),
        in_specs=[pl.BlockSpec(memory_space=pl.ANY),
                  pl.BlockSpec(memory_space=pl.ANY),
                  pl.BlockSpec((1, d), lambda e, tab: (0, 0)),
                  pl.BlockSpec((None, None, d, D_EXPERT), widx),
                  pl.BlockSpec((None, None, d, D_EXPERT), widx),
                  pl.BlockSpec((None, None, D_EXPERT, d), widx)],
        out_specs=pl.BlockSpec(memory_space=pl.ANY),
        scratch_shapes=[pltpu.VMEM((GATHER_DEPTH, TILE_ROWS, LANES), F32),
                        pltpu.VMEM((2, TILE_ROWS, LANES), F32),
                        pltpu.SMEM((GATHER_DEPTH, 1, MOE_TILE), jnp.int32),
                        pltpu.SemaphoreType.DMA((GATHER_DEPTH,)),
                        pltpu.SemaphoreType.DMA((2,)),
                        pltpu.SemaphoreType.DMA((GATHER_DEPTH,)),
                        pltpu.VMEM((d, D_EXPERT), BF16),
                        pltpu.VMEM((d, D_EXPERT), BF16),
                        pltpu.VMEM((D_EXPERT, d), BF16)],
    )
    return pl.pallas_call(
        functools.partial(_experts_kernel, n_slots=n_steps),
        grid_spec=grid_spec,
        out_shape=jax.ShapeDtypeStruct((n_steps * TILE_ROWS, LANES), F32),
        compiler_params=_cparams(("arbitrary",)),
        name="moe_experts",
    )(tile_tab, inv, hx_all, g_ffn, w_gate, w_up, w_down)


PLE_COLS = 512


def _combine_kernel(*refs, tm, n_steps):
    lists = [(refs[k], refs[GATHER_DEPTH + k]) for k in range(GATHER_DEPTH)]
    y_hbm, h_ref, route_ref, p_ref, g_ref, wg_ref, wp_ref, o_ref, ybuf, sems = refs[2 * GATHER_DEPTH:]
    i = pl.program_id(0)
    s, s_req = i % GATHER_DEPTH, (i + GATHER_DEPTH - 1) % GATHER_DEPTH

    def request(p0, p1, sl, r):
        _row_copy(y_hbm, p0[0, 0, r], ybuf.at[sl, 0], r, sems.at[sl]).start(priority=0)
        _row_copy(y_hbm, p1[0, 0, r], ybuf.at[sl, 1], r, sems.at[sl]).start(priority=1)

    def wait_tile(sl):
        for c in range(2):
            _rows_wait(y_hbm, ybuf.at[sl, c], tm, sems.at[sl])

    @pl.when(i == 0)
    def _():
        for k in range(GATHER_DEPTH - 1):
            def body(r, carry, k=k):
                request(lists[k][0], lists[k][1], k, r)
                return carry
            lax.fori_loop(0, tm, body, 0, unroll=8)

    wait_tile(s)
    route = route_ref[...]
    y0, y1 = _load_rows(ybuf.at[s, 0], tm), _load_rows(ybuf.at[s, 1], tm)
    for r in range(tm):
        request(lists[-1][0], lists[-1][1], s_req, r)
    h2 = _load_rows(h_ref, tm) + (route[:, 2:3] * y0 + route[:, 3:4] * y1)
    xn = _rms(h2, g_ref[...]).astype(BF16)
    pb = p_ref[...].astype(BF16)
    for j in range(h2.shape[1] // PLE_COLS):
        cols = slice(j * PLE_COLS, (j + 1) * PLE_COLS)
        z = jnp.dot(xn, wg_ref[:, cols], preferred_element_type=F32)
        proj = jnp.dot(pb, wp_ref[:, cols], preferred_element_type=F32)
        o_ref[:, cols] = h2[:, cols] + proj * (1.0 / (1.0 + jnp.exp(-z)))

    @pl.when(i == n_steps - 1)
    def _():
        for ahead in range(GATHER_DEPTH - 1):
            wait_tile((n_steps + ahead) % GATHER_DEPTH)


def _moe_combine_ple(pos0, pos1, y_sorted, hx_all, route_all, p, g_ple, w_gate, w_proj, *, layer, tm, rows,
                     row_block0, name):
    d = D_MODEL
    n = rows // tm
    last = n - 1
    ahead = [pl.BlockSpec((1, 1, tm), lambda i, k=k: (jnp.minimum(i + k, last), 0, 0), memory_space=pltpu.SMEM)
             for k in range(GATHER_DEPTH)]
    return pl.pallas_call(
        functools.partial(_combine_kernel, tm=tm, n_steps=n),
        grid=(n,),
        in_specs=ahead + ahead + [
                  pl.BlockSpec(memory_space=pl.ANY),
                  pl.BlockSpec((tm * ROW_PITCH, LANES), lambda i: (row_block0 + i, 0)),
                  pl.BlockSpec((tm, ROUTE_LANES), lambda i: (row_block0 + i, 0)),
                  pl.BlockSpec((None, tm, PLE_DIM), lambda i: (layer, i, 0)),
                  pl.BlockSpec((1, d), lambda i: (0, 0)),
                  pl.BlockSpec((d, d), lambda i: (0, 0)),
                  pl.BlockSpec((PLE_DIM, d), lambda i: (0, 0))],
        out_specs=pl.BlockSpec((tm, d), lambda i: (i, 0)),
        out_shape=jax.ShapeDtypeStruct((rows, d), F32),
        scratch_shapes=[pltpu.VMEM((GATHER_DEPTH, 2, tm * ROW_PITCH, LANES), F32),
                        pltpu.SemaphoreType.DMA((GATHER_DEPTH,))],
        compiler_params=_cparams(("arbitrary",)),
        name=name,
    )(*([pos0] * GATHER_DEPTH + [pos1] * GATHER_DEPTH), y_sorted, hx_all, route_all, p, g_ple, w_gate, w_proj)


def _ret_head(q, k, v, gt, state, gn, lg, c_real):
    c = q.shape[0]
    k = k * (RET_DK ** -0.5)
    ri = lax.broadcasted_iota(jnp.int32, (c, c), 0)
    ci = lax.broadcasted_iota(jnp.int32, (c, c), 1)
    diff = (ri - ci).astype(F32)
    decay = jnp.where(diff >= 0, jnp.exp(lg * jnp.maximum(diff, 0.0)), 0.0)
    idx = lax.broadcasted_iota(jnp.int32, (c, 1), 0).astype(F32)
    qk = lax.dot_general(q.astype(BF16), k.astype(BF16), (((1,), (1,)), ((), ())), preferred_element_type=F32)
    inner = qk * decay
    q_dec = q * jnp.exp(lg * (idx + 1.0))
    k_dec = k * jnp.exp(lg * (c_real - 1.0 - idx))
    vb = v.astype(BF16)
    out = (jnp.dot(inner.astype(BF16), vb, preferred_element_type=F32)
           + jnp.dot(q_dec.astype(BF16), state.astype(BF16), preferred_element_type=F32))
    new_state = math.exp(lg * c_real) * state + lax.dot_general(
        k_dec.astype(BF16), vb, (((0,), (0,)), ((), ())), preferred_element_type=F32)
    mu = jnp.mean(out, axis=-1, keepdims=True)
    cen = out - mu
    var = jnp.mean(cen * cen, axis=-1, keepdims=True)
    on = cen * lax.rsqrt(var + GN_EPS) * gn
    return gt / (1.0 + jnp.exp(-gt)) * on, new_state


def _ret_prompt_kernel(q_ref, k_ref, v_ref, g_ref, gn_ref, y_ref, so_ref, st_ref):
    c = pl.program_id(1)

    @pl.when(c == 0)
    def _():
        st_ref[...] = jnp.zeros_like(st_ref)

    for h in range(RET_HEADS):
        ks, vs = slice(h * RET_DK, (h + 1) * RET_DK), slice(h * RET_DV, (h + 1) * RET_DV)
        y, new_state = _ret_head(q_ref[:, ks], k_ref[:, ks], v_ref[:, vs], g_ref[:, vs], st_ref[h],
                                 gn_ref[:, vs], RET_LOG_GAMMA[h], RET_CHUNK)
        st_ref[h] = new_state
        y_ref[:, vs] = y.astype(y_ref.dtype)

    @pl.when(c == pl.num_programs(1) - 1)
    def _():
        so_ref[0] = st_ref[...]


def _ret_prompt(qkvg, g_norm, batch):
    rows = qkvg.shape[0]
    nc = rows // batch // RET_CHUNK
    return pl.pallas_call(
        _ret_prompt_kernel,
        grid=(batch, nc),
        in_specs=[pl.BlockSpec((RET_CHUNK, RET_K_WIDTH), lambda b, c: (b * nc + c, 0)),
                  pl.BlockSpec((RET_CHUNK, RET_K_WIDTH), lambda b, c: (b * nc + c, 1)),
                  pl.BlockSpec((RET_CHUNK, RET_V_WIDTH), lambda b, c: (b * nc + c, 1)),
                  pl.BlockSpec((RET_CHUNK, RET_V_WIDTH), lambda b, c: (b * nc + c, 2)),
                  pl.BlockSpec((1, RET_V_WIDTH), lambda b, c: (0, 0))],
        out_specs=[pl.BlockSpec((RET_CHUNK, RET_V_WIDTH), lambda b, c: (b * nc + c, 0)),
                   pl.BlockSpec((1, RET_HEADS, RET_DK, RET_DV), lambda b, c: (b, 0, 0, 0))],
        out_shape=[jax.ShapeDtypeStruct((rows, RET_V_WIDTH), BF16),
                   jax.ShapeDtypeStruct((batch, RET_HEADS, RET_DK, RET_DV), F32)],
        scratch_shapes=[pltpu.VMEM((RET_HEADS, RET_DK, RET_DV), F32)],
        compiler_params=_cparams(("parallel", "arbitrary")),
        name="ret_prompt",
    )(qkvg, qkvg, qkvg, qkvg, g_norm)


RET_DEC_GROUP = 2


def _ret_decode_kernel(x_ref, s_ref, gn_ref, y_ref, so_ref, *, t_new):
    for b in range(x_ref.shape[0]):
        x = x_ref[b]
        for h in range(RET_HEADS):
            q = x[:, h * RET_DK:(h + 1) * RET_DK]
            k = x[:, RET_K_WIDTH + h * RET_DK:RET_K_WIDTH + (h + 1) * RET_DK]
            v0 = 2 * RET_K_WIDTH + h * RET_DV
            g0 = 2 * RET_K_WIDTH + RET_V_WIDTH + h * RET_DV
            y, new_state = _ret_head(q, k, x[:, v0:v0 + RET_DV], x[:, g0:g0 + RET_DV], s_ref[b, h],
                                     gn_ref[:, h * RET_DV:(h + 1) * RET_DV], RET_LOG_GAMMA[h], t_new)
            so_ref[b, h] = new_state
            y_ref[b, :, h * RET_DV:(h + 1) * RET_DV] = y


def _ret_decode(qkvg, state, g_norm, t_new):
    b = qkvg.shape[0]
    g = RET_DEC_GROUP if b % RET_DEC_GROUP == 0 else 1
    return pl.pallas_call(
        functools.partial(_ret_decode_kernel, t_new=t_new),
        grid=(b // g,),
        in_specs=[pl.BlockSpec((g, DEC_ROWS, RET_IN_WIDTH), lambda i: (i, 0, 0)),
                  pl.BlockSpec((g, RET_HEADS, RET_DK, RET_DV), lambda i: (i, 0, 0, 0)),
                  pl.BlockSpec((1, RET_V_WIDTH), lambda i: (0, 0))],
        out_specs=[pl.BlockSpec((g, DEC_ROWS, RET_V_WIDTH), lambda i: (i, 0, 0)),
                   pl.BlockSpec((g, RET_HEADS, RET_DK, RET_DV), lambda i: (i, 0, 0, 0))],
        out_shape=[jax.ShapeDtypeStruct((b, DEC_ROWS, RET_V_WIDTH), F32),
                   jax.ShapeDtypeStruct((b, RET_HEADS, RET_DK, RET_DV), F32)],
        compiler_params=_cparams(("parallel",)),
        name="ret_decode",
    )(qkvg, state, g_norm)


def _pad_rows(x, b, t):
    return jnp.pad(x.reshape(b, t, x.shape[-1]), ((0, 0), (0, DEC_ROWS - t), (0, 0)))


def _moe_layer(routed, p_prompt, p_sample, g_ffn, g_ple, w_gate_e, w_up_e, w_down_e, w_ple_gate,
               w_ple_proj, layer, n_prompt, n_sample):
    route_all, hx_all, route_t = routed
    n_tok = n_prompt + n_sample
    n_pairs = PAIR_ROWS * LANES
    eid = route_t[:2].astype(jnp.int32).reshape(-1)
    eid = jnp.pad(eid, (0, n_pairs - 2 * n_tok), constant_values=-1).reshape(PAIR_ROWS, LANES)
    pos, tile_tab = _moe_meta(eid)
    inv = _moe_invert(pos.reshape(1, n_pairs), n_tok).reshape(MAX_TILES, 1, MOE_TILE)
    max_tiles = (2 * n_tok + N_EXPERTS * (MOE_TILE - 1)) // MOE_TILE
    y_sorted = _moe_experts(tile_tab[0], inv, hx_all, g_ffn, w_gate_e, w_up_e, w_down_e, layer, max_tiles)
    pos2 = pos.reshape(-1)[:2 * n_tok].reshape(2, n_tok)
    tm_p, tm_s = 256, n_sample
    outs = []
    for lo, rows, tm, p, nm in ((0, n_prompt, tm_p, p_prompt, "combine_prompt"),
                                (n_prompt, n_sample, tm_s, p_sample, "combine_sample")):
        pos0 = pos2[0, lo:lo + rows].reshape(rows // tm, 1, tm)
        pos1 = pos2[1, lo:lo + rows].reshape(rows // tm, 1, tm)
        outs.append(_moe_combine_ple(pos0, pos1, y_sorted, hx_all, route_all, p, g_ple, w_ple_gate, w_ple_proj,
                                     layer=layer, tm=tm, rows=rows, row_block0=lo // tm, name=nm))
    return outs


def kernel(x_prompt, x_sample, cache_k_swa, cache_v_swa, state_ret, p_prompt, p_sample, g_mix, g_ffn, g_ple,
           swa_w_qkv, swa_g_q, swa_g_k, swa_sinks, swa_w_o, ret_w_in, ret_g_norm, ret_w_o, moe_w_group,
           moe_b_group, moe_w_expert, moe_b_expert, moe_w_gate, moe_w_up, moe_w_down, ple_w_proj, ple_w_gate):
    batch, seq, d = x_prompt.shape
    dec_b, dec_t, _ = x_sample.shape
    n_prompt, n_sample = batch * seq, dec_b * dec_t
    n_tok = n_prompt + n_sample
    assert 2 * n_tok <= PAIR_ROWS * LANES and 2 * n_tok + N_EXPERTS * (MOE_TILE - 1) <= (MAX_TILES - 1) * MOE_TILE
    assert dec_t <= DEC_ROWS and n_prompt % 2048 == 0 and n_sample % 8 == 0

    xp = x_prompt.reshape(n_prompt, d)
    xs = x_sample.reshape(n_sample, d)
    pp = p_prompt.reshape(DEPTH, n_prompt, PLE_DIM)
    ps = p_sample.reshape(DEPTH, n_sample, PLE_DIM)
    row = lambda a: a.reshape(1, -1)

    def router_w(i):
        w = jnp.concatenate([moe_w_group[i], moe_w_expert[i]], axis=1)
        b = jnp.concatenate([moe_b_group[i], moe_b_expert[i]])
        pad = ROUTE_LANES - w.shape[1]
        return jnp.pad(w, ((0, 0), (0, pad))), jnp.pad(b, (0, pad)).reshape(1, ROUTE_LANES)

    g_q2, g_k2 = row(jnp.tile(swa_g_q[0], 2)), row(jnp.tile(swa_g_k[0], 2))
    w_r, b_r = router_w(0)
    qkv_p = _norm_proj(xp, row(g_mix[0]), swa_w_qkv[0], tm=2048, tn=256, hi=False, name="qkv_prompt")
    sinks2 = swa_sinks[0] * LOG2E
    o_p, k_p, v_p = _swa_prompt(qkv_p, g_q2, g_k2, sinks2, batch)

    qkv_s = _norm_proj(xs, row(g_mix[0]), swa_w_qkv[0], tm=n_sample, tn=512, hi=True, name="qkv_sample")
    ck = cache_k_swa[0].reshape(dec_b, WINDOW, KV_WIDTH)
    cv = cache_v_swa[0].reshape(dec_b, WINDOW, KV_WIDTH)
    o_s, k_s, v_s = _swa_decode(_pad_rows(qkv_s, dec_b, dec_t), ck, cv, g_q2, g_k2, sinks2, dec_t)
    o_s = o_s[:, :dec_t].reshape(n_sample, Q_WIDTH)
    tail = _outproj_router(o_s, swa_w_o[0], xs, row(g_ffn[0]), w_r, b_r, tm=n_sample, tk=512, hi=True,
                           tail=None, name="swa_out_sample")
    routed = _outproj_router(o_p, swa_w_o[0].astype(BF16), xp, row(g_ffn[0]), w_r.astype(BF16), b_r, tm=512,
                             tk=2048, hi=False, tail=tail, name="swa_out_prompt")

    h_p, h_s = _moe_layer(routed, pp, ps, row(g_ffn[0]), row(g_ple[0]), moe_w_gate, moe_w_up, moe_w_down,
                          ple_w_gate[0].astype(BF16), ple_w_proj[0].astype(BF16), 0, n_prompt, n_sample)

    w_o = ret_w_o[0].astype(BF16)
    w_r, b_r = router_w(1)
    w_r = w_r.astype(BF16)
    qkvg_p = _norm_proj(h_p, row(g_mix[1]), ret_w_in[0], tm=2048, tn=256, hi=False, name="ret_in_prompt")
    y_p, s_p = _ret_prompt(qkvg_p, row(ret_g_norm[0]), batch)

    qkvg_s = _norm_proj(h_s, row(g_mix[1]), ret_w_in[0], tm=n_sample, tn=1024, hi=False, name="ret_in_sample")
    y_s, s_s = _ret_decode(_pad_rows(qkvg_s, dec_b, dec_t), state_ret[0], row(ret_g_norm[0]), dec_t)
    y_s = y_s[:, :dec_t].reshape(n_sample, RET_V_WIDTH)
    tail = _outproj_router(y_s, w_o, h_s, row(g_ffn[1]), w_r, b_r, tm=n_sample, tk=2048, hi=False, tail=None,
                           name="ret_out_sample")
    routed = _outproj_router(y_p, w_o, h_p, row(g_ffn[1]), w_r, b_r, tm=512, tk=2048, hi=False, tail=tail,
                             name="ret_out_prompt")

    y_prompt, y_sample = _moe_layer(routed, pp, ps, row(g_ffn[1]), row(g_ple[1]), moe_w_gate, moe_w_up, moe_w_down,
                                    ple_w_gate[1].astype(BF16), ple_w_proj[1].astype(BF16), 1, n_prompt, n_sample)

    kv_shape = (1, -1, WINDOW, SWA_KV_HEADS, SWA_HEAD_DIM)
    return (y_prompt.reshape(batch, seq, d), y_sample.reshape(dec_b, dec_t, d),
            k_p.reshape(kv_shape), v_p.reshape(kv_shape), s_p[None],
            k_s.reshape(kv_shape), v_s.reshape(kv_shape), s_s[None])
```
